```python
import jax, jax.numpy as jnp
from jax import lax
import numpy as np

D_MODEL = 1024
BATCH = 1
SEQ = 16384
DEPTH = 2
DEC_BATCH = 32
DEC_SEQ = 1
PAST_LEN = 16384
PAGE_SIZE = 128

PLE_DIM = 256
D_FF = 2816
RMS_EPS = 1e-6
NEG_BIG = -1e30
MIN_F = 1e-30
A_HEADS = 8
A_HEAD_DIM = 64
A_CONFIGS = ((128, 1), (512, 4), (2048, 16))
A_MAX_WINDOW = 2048
B_HEADS = 4
B_KEY_DIM = 128
B_VAL_DIM = 64
C_HEADS = 4
C_KEY_DIM = 32
C_VAL_DIM = 64
C_GATE_RANK = 16
C_GATE_TEMP = 16.0
CHUNK = 64

A_WIDTH = A_HEADS * A_HEAD_DIM
B_WIDTH = B_HEADS * B_VAL_DIM
C_WIDTH = C_HEADS * C_VAL_DIM
D_MIX = A_WIDTH + B_WIDTH + C_WIDTH
PROJ_SIZES = (A_WIDTH, A_WIDTH, A_WIDTH,
              B_HEADS * B_KEY_DIM, B_HEADS * B_KEY_DIM, B_WIDTH, B_WIDTH,
              C_HEADS * C_KEY_DIM, C_HEADS * C_KEY_DIM, C_WIDTH, C_GATE_RANK, C_WIDTH)
PROJ_WIDTH = sum(PROJ_SIZES)

kernel_name = "hybrid_dilated_hgrn2_gla_decoder_step"


def rms_norm(x, g):
    xf = x.astype(jnp.float32)
    y = xf * lax.rsqrt(jnp.mean(xf * xf, axis=-1, keepdims=True) + RMS_EPS)
    return (y * g.astype(jnp.float32)).astype(x.dtype)


def swiglu(x, wg, wu, wd):
    return (jax.nn.silu(x @ wg) * (x @ wu)) @ wd


def alibi_slopes():
    return 2.0 ** (-8.0 * jnp.arange(1, A_HEADS + 1, dtype=jnp.float32) / A_HEADS)


def split_projection(z):
    cuts = [int(c) for c in np.cumsum(PROJ_SIZES)[:-1]]
    return jnp.split(z, cuts, axis=-1)


def dilated_attn_prompt(q, k, v, window, dil, slopes):
    b, t, h, e = q.shape
    sub = window // dil
    blk = sub
    span = dil * blk
    t_pad = -(-t // span) * span
    pad = ((0, 0), (0, t_pad - t), (0, 0), (0, 0))
    q, k, v = jnp.pad(q, pad), jnp.pad(k, pad), jnp.pad(v, pad)
    nb = t_pad // span
    shp = (b, nb, blk, dil, h, e)
    qb, kb, vb = q.reshape(shp), k.reshape(shp), v.reshape(shp)

    def with_prev(a):
        prev = jnp.pad(a, ((0, 0), (1, 0), (0, 0), (0, 0), (0, 0), (0, 0)))[:, :-1]
        return jnp.concatenate([prev, a], axis=2)

    kk, vv = with_prev(kb), with_prev(vb)
    s = jnp.einsum('bnqrhe,bnkrhe->bnrhqk', qb, kk).astype(jnp.float32)
    qi = jnp.arange(blk)[:, None]
    ki = jnp.arange(2 * blk)[None, :]
    j = qi + blk - ki
    band = (j >= 0) & (j <= sub)
    key_sub = jnp.arange(nb)[:, None, None] * blk + ki[None] - blk
    valid = band[None] & (key_sub >= 0)
    bias = -slopes[:, None, None] * (j * dil).astype(jnp.float32)[None]
    s = jnp.where(valid[None, :, None, None], s + bias, NEG_BIG)
    mx = jnp.max(s, axis=-1, keepdims=True)
    p = jnp.exp(s - mx)
    den = jnp.sum(p, axis=-1, keepdims=True)
    o = jnp.einsum('bnrhqk,bnkrhe->bnqrhe', p / den, vv.astype(jnp.float32))
    lse = (mx + jnp.log(den))[..., 0]
    o = o.reshape(b, t_pad, h, e)[:, :t]
    lse = jnp.transpose(lse, (0, 1, 4, 2, 3)).reshape(b, t_pad, h)[:, :t]
    return o, lse


def dilated_attn_sample(q, k_all, v_all, n_past, window, dil, slopes):
    s_len = q.shape[1]
    sub = window // dil
    jj = jnp.arange(sub + 1)
    idx = n_past + jnp.arange(s_len)[:, None] - jj[None] * dil
    valid = idx >= 0
    idx = jnp.maximum(idx, 0)
    kg = k_all[:, idx]
    vg = v_all[:, idx]
    sc = jnp.einsum('bqhe,bqjhe->bqhj', q, kg).astype(jnp.float32)
    sc = sc - slopes[:, None] * (jj * dil).astype(jnp.float32)[None]
    sc = jnp.where(valid[None, :, None, :], sc, NEG_BIG)
    mx = jnp.max(sc, axis=-1, keepdims=True)
    p = jnp.exp(sc - mx)
    den = jnp.sum(p, axis=-1, keepdims=True)
    o = jnp.einsum('bqhj,bqjhe->bqhe', p / den, vg.astype(jnp.float32))
    lse = (mx + jnp.log(den))[..., 0]
    return o, lse


def merge_by_denominator(outs, lses):
    w = jax.nn.softmax(jnp.stack(lses, axis=0), axis=0)
    return jnp.einsum('cbth,cbthe->bthe', w, jnp.stack(outs, axis=0))


def gated_linear_recurrence(q, k, v, log_f, s0):
    b, t, h, dk = q.shape
    dv = v.shape[-1]
    c = min(CHUNK, t)
    t_pad = -(-t // c) * c
    nc = t_pad // c

    def chunks(a):
        a = jnp.pad(a.astype(jnp.float32), ((0, 0), (0, t_pad - t), (0, 0), (0, 0)))
        return jnp.moveaxis(a.reshape(b, nc, c, h, a.shape[-1]), 1, 0)

    causal = jnp.tril(jnp.ones((c, c), dtype=bool))

    def step(state, inp):
        qc, kc, vc, gc = inp
        g_cum = jnp.cumsum(gc, axis=1)
        diff = g_cum[:, :, None] - g_cum[:, None, :]
        decay = jnp.exp(jnp.where(causal[None, :, :, None, None], diff, NEG_BIG))
        att = jnp.einsum('bthk,btshk->bths', qc, decay * kc[:, None])
        o = (jnp.einsum('bths,bshv->bthv', att, vc)
             + jnp.einsum('bthk,bhkv->bthv', qc * jnp.exp(g_cum), state))
        g_last = g_cum[:, -1]
        new_state = (jnp.exp(g_last)[..., None] * state
                     + jnp.einsum('bshk,bshv->bhkv', kc * jnp.exp(g_last[:, None] - g_cum), vc))
        return new_state, o

    s_fin, o = lax.scan(step, s0.astype(jnp.float32),
                        (chunks(q), chunks(k), chunks(v), chunks(log_f)))
    o = jnp.moveaxis(o, 0, 1).reshape(b, t_pad, h, dv)[:, :t]
    return o.astype(v.dtype), s_fin


def mixing(hn, w_in, lb, b_norm, c_w_gate, c_gate_bias, c_norm, w_out, k_past, v_past, b_state0, c_state0):
    bsz, t, _ = hn.shape
    aq, ak, av, bq, bf, bi, bg, cq, ck, cv, clr, cg = split_projection(hn @ w_in)

    def heads(a, n):
        return a.reshape(bsz, t, n, -1)

    aq = heads(aq, A_HEADS) * (A_HEAD_DIM ** -0.5)
    ak = heads(ak, A_HEADS)
    av = heads(av, A_HEADS)
    slopes = alibi_slopes()
    if k_past is None:
        res = [dilated_attn_prompt(aq, ak, av, w, d, slopes) for (w, d) in A_CONFIGS]
    else:
        k_all = jnp.concatenate([k_past.astype(ak.dtype), ak], axis=1)
        v_all = jnp.concatenate([v_past.astype(av.dtype), av], axis=1)
        n_past = k_past.shape[1]
        res = [dilated_attn_sample(aq, k_all, v_all, n_past, w, d, slopes) for (w, d) in A_CONFIGS]
    o_a = merge_by_denominator([r[0] for r in res], [r[1] for r in res])
    o_a = o_a.reshape(bsz, t, A_WIDTH).astype(hn.dtype)

    z = heads(bf, B_HEADS).astype(jnp.float32)
    f_b = lb + (1.0 - lb) * jax.nn.sigmoid(z)
    log_f = jnp.log(jnp.maximum(f_b, MIN_F))
    k_b = 1.0 - f_b
    q_b = jax.nn.silu(heads(bq, B_HEADS))
    i_b = heads(bi, B_HEADS)
    o_b, s_b = gated_linear_recurrence(q_b, k_b, i_b, log_f, b_state0)
    o_b = rms_norm(o_b, b_norm.reshape(B_HEADS, B_VAL_DIM)) * jax.nn.sigmoid(heads(bg, B_HEADS))

    q_c = heads(cq, C_HEADS) * (C_KEY_DIM ** -0.5)
    k_c = heads(ck, C_HEADS)
    v_c = heads(cv, C_HEADS)
    g_c = jax.nn.log_sigmoid((clr @ c_w_gate + c_gate_bias).astype(jnp.float32)) / C_GATE_TEMP
    o_c, s_c = gated_linear_recurrence(q_c, k_c, v_c, heads(g_c, C_HEADS), c_state0)
    o_c = rms_norm(o_c, c_norm.reshape(C_HEADS, C_VAL_DIM)) * jax.nn.silu(heads(cg, C_HEADS))

    mixed = jnp.concatenate([o_a, o_b.reshape(bsz, t, B_WIDTH), o_c.reshape(bsz, t, C_WIDTH)], axis=-1)
    return mixed @ w_out, ak, av, s_b, s_c


def run_trunk(x, p, prm, lb_all, k_cache, v_cache, s_b_in, s_c_in):
    bsz, t, _ = x.shape
    h = x
    k_rows, v_rows, sb_out, sc_out = [], [], [], []
    for i in range(DEPTH):
        h = h + 0.5 * swiglu(rms_norm(h, prm['ffn1_norm'][i]), prm['ffn1_w_gate'][i],
                             prm['ffn1_w_up'][i], prm['ffn1_w_down'][i])
        if k_cache is None:
            k_past, v_past = None, None
            sb0 = jnp.zeros((bsz, B_HEADS, B_KEY_DIM, B_VAL_DIM), jnp.float32)
            sc0 = jnp.zeros((bsz, C_HEADS, C_KEY_DIM, C_VAL_DIM), jnp.float32)
        else:
            k_past, v_past, sb0, sc0 = k_cache[i], v_cache[i], s_b_in[i], s_c_in[i]
        mixed, ak, av, sb, sc = mixing(rms_norm(h, prm['mix_norm'][i]), prm['w_in'][i], lb_all[i],
                                       prm['b_out_norm'][i], prm['c_w_gate'][i], prm['c_gate_bias'][i],
                                       prm['c_out_norm'][i], prm['w_out'][i], k_past, v_past, sb0, sc0)
        h = h + mixed
        h = h + 0.5 * swiglu(rms_norm(h, prm['ffn2_norm'][i]), prm['ffn2_w_gate'][i],
                             prm['ffn2_w_up'][i], prm['ffn2_w_down'][i])
        hn = rms_norm(h, prm['ple_norm'][i])
        h = h + jax.nn.sigmoid(hn @ prm['ple_w_gate'][i]) * (p[i].astype(h.dtype) @ prm['ple_w_proj'][i])
        if k_cache is None:
            keep = min(A_MAX_WINDOW, t)
            ak, av = ak[:, t - keep:], av[:, t - keep:]
        k_rows.append(ak)
        v_rows.append(av)
        sb_out.append(sb)
        sc_out.append(sc)
    y = rms_norm(h, prm['final_norm'])
    return y, jnp.stack(k_rows), jnp.stack(v_rows), jnp.stack(sb_out), jnp.stack(sc_out)


def setup_inputs(seed: int = 0) -> dict:
    key = jax.random.key(seed)
    keys = list(jax.random.split(key, 40))

    def nrm(shape, scale):
        return scale * jax.random.normal(keys.pop(), shape, jnp.float32)

    n_buf = min(A_MAX_WINDOW, PAST_LEN)
    L = DEPTH
    return {
        'x_prompt': nrm((BATCH, SEQ, D_MODEL), 1.0),
        'x_sample': nrm((DEC_BATCH, DEC_SEQ, D_MODEL), 1.0),
        'cache_k_a': nrm((L, DEC_BATCH, n_buf, A_HEADS, A_HEAD_DIM), 1.0),
        'cache_v_a': nrm((L, DEC_BATCH, n_buf, A_HEADS, A_HEAD_DIM), 1.0),
        'state_b': nrm((L, DEC_BATCH, B_HEADS, B_KEY_DIM, B_VAL_DIM), 0.5),
        'state_c': nrm((L, DEC_BATCH, C_HEADS, C_KEY_DIM, C_VAL_DIM), 0.5),
        'p_prompt': nrm((L, BATCH, SEQ, PLE_DIM), 1.0),
        'p_sample': nrm((L, DEC_BATCH, DEC_SEQ, PLE_DIM), 1.0),
        'ffn1_norm': 1.0 + nrm((L, D_MODEL), 0.02),
        'ffn1_w_gate': nrm((L, D_MODEL, D_FF), D_MODEL ** -0.5),
        'ffn1_w_up': nrm((L, D_MODEL, D_FF), D_MODEL ** -0.5),
        'ffn1_w_down': nrm((L, D_FF, D_MODEL), D_FF ** -0.5),
        'mix_norm': 1.0 + nrm((L, D_MODEL), 0.02),
        'w_in': nrm((L, D_MODEL, PROJ_WIDTH), D_MODEL ** -0.5),
        'lb_logits': nrm((L, B_HEADS, B_KEY_DIM), 1.0),
        'b_out_norm': 1.0 + nrm((L, B_WIDTH), 0.02),
        'c_w_gate': nrm((L, C_GATE_RANK, C_HEADS * C_KEY_DIM), C_GATE_RANK ** -0.5),
        'c_gate_bias': nrm((L, C_HEADS * C_KEY_DIM), 0.1),
        'c_out_norm': 1.0 + nrm((L, C_WIDTH), 0.02),
        'w_out': nrm((L, D_MIX, D_MODEL), D_MIX ** -0.5),
        'ffn2_norm': 1.0 + nrm((L, D_MODEL), 0.02),
        'ffn2_w_gate': nrm((L, D_MODEL, D_FF), D_MODEL ** -0.5),
        'ffn2_w_up': nrm((L, D_MODEL, D_FF), D_MODEL ** -0.5),
        'ffn2_w_down': nrm((L, D_FF, D_MODEL), D_FF ** -0.5),
        'ple_norm': 1.0 + nrm((L, D_MODEL), 0.02),
        'ple_w_gate': nrm((L, D_MODEL, D_MODEL), D_MODEL ** -0.5),
        'ple_w_proj': nrm((L, PLE_DIM, D_MODEL), PLE_DIM ** -0.5),
        'final_norm': 1.0 + nrm((D_MODEL,), 0.02),
    }


def reference(x_prompt, x_sample, cache_k_a, cache_v_a, state_b, state_c, p_prompt, p_sample,
              ffn1_norm, ffn1_w_gate, ffn1_w_up, ffn1_w_down, mix_norm, w_in, lb_logits, b_out_norm,
              c_w_gate, c_gate_bias, c_out_norm, w_out, ffn2_norm, ffn2_w_gate, ffn2_w_up, ffn2_w_down,
              ple_norm, ple_w_gate, ple_w_proj, final_norm):
    prm = dict(ffn1_norm=ffn1_norm, ffn1_w_gate=ffn1_w_gate, ffn1_w_up=ffn1_w_up, ffn1_w_down=ffn1_w_down,
               mix_norm=mix_norm, w_in=w_in, b_out_norm=b_out_norm, c_w_gate=c_w_gate,
               c_gate_bias=c_gate_bias, c_out_norm=c_out_norm, w_out=w_out, ffn2_norm=ffn2_norm,
               ffn2_w_gate=ffn2_w_gate, ffn2_w_up=ffn2_w_up, ffn2_w_down=ffn2_w_down, ple_norm=ple_norm,
               ple_w_gate=ple_w_gate, ple_w_proj=ple_w_proj, final_norm=final_norm)
    sm = jax.nn.softmax(lb_logits.astype(jnp.float32), axis=0)
    lb_all = jnp.maximum(jnp.cumsum(sm, axis=0) - sm[0], 0.0)
    y_prompt, k_a_p, v_a_p, s_b_p, s_c_p = run_trunk(x_prompt, p_prompt, prm, lb_all, None, None, None, None)
    y_sample, k_a_s, v_a_s, s_b_s, s_c_s = run_trunk(x_sample, p_sample, prm, lb_all,
                                                     cache_k_a, cache_v_a, state_b, state_c)
    return (y_prompt, y_sample, k_a_p, v_a_p, s_b_p, s_c_p, k_a_s, v_a_s, s_b_s, s_c_s)
```

```python
import functools
import math

import numpy as np
import jax
import jax.numpy as jnp
from jax import lax
from jax.experimental import pallas as pl
from jax.experimental.pallas import tpu as pltpu

F32 = jnp.float32
BF16 = jnp.bfloat16

RMS_EPS = 1e-6
NEG_BIG = -1e30
MIN_F = 1e-30
A_HEADS = 8
A_HEAD_DIM = 64
A_CONFIGS = ((128, 1), (512, 4), (2048, 16))
A_MAX_WINDOW = 2048
A_SUB = 128
B_HEADS = 4
B_KEY_DIM = 128
B_VAL_DIM = 64
C_HEADS = 4
C_KEY_DIM = 32
C_VAL_DIM = 64
C_GATE_RANK = 16
C_GATE_TEMP = 16.0
A_WIDTH = A_HEADS * A_HEAD_DIM
B_KW = B_HEADS * B_KEY_DIM
B_WIDTH = B_HEADS * B_VAL_DIM
C_KW = C_HEADS * C_KEY_DIM
C_WIDTH = C_HEADS * C_VAL_DIM
ZB_WIDTH = 2 * B_KW + 2 * B_WIDTH
ZC_WIDTH = 2 * C_KW + 2 * C_WIDTH + 128
PROJ_PAD = 3 * A_WIDTH + ZB_WIDTH + ZC_WIDTH

LANES = 128
VMEM_LIMIT_BYTES = 56 * 1024 * 1024

ROW_TILE = 512
FF_CHUNK = 1408
ATT_ROWS = 512
REC_CHUNK = 128
DEC_PAD = 16


def _params(n_axes):
    return pltpu.CompilerParams(dimension_semantics=("arbitrary",) * n_axes,
                                vmem_limit_bytes=VMEM_LIMIT_BYTES)


def _const_spec(shape):
    zeros = (0,) * len(shape)
    return pl.BlockSpec(shape, lambda *_: zeros, pipeline_mode=pl.Buffered(1))


def _dot(a, b):
    return jnp.dot(a, b, preferred_element_type=F32)


def _dot_nt(a, b):
    return lax.dot_general(a, b, (((1,), (1,)), ((), ())), preferred_element_type=F32)


def _dot_tn(a, b):
    return lax.dot_general(a, b, (((0,), (0,)), ((), ())), preferred_element_type=F32)


def _dot_split2(x, w):
    hi = x.astype(BF16)
    lo = (x - hi.astype(F32)).astype(BF16)
    return _dot(hi, w) + _dot(lo, w)


def _rms(x, g):
    ms = jnp.mean(x * x, axis=-1, keepdims=True)
    return x * lax.rsqrt(ms + RMS_EPS) * g


def _sigmoid(x):
    return 1.0 / (1.0 + jnp.exp(-x))


def _ffn_body(*refs, n_chunks, chunk, with_ple, with_final):
    refs = list(refs)
    h_ref, nrm_ref, wg_ref, wu_ref, wd_ref = refs[:5]
    pos = 5
    if with_ple:
        p_ref, pn_ref, pwg_ref, pwp_ref = refs[pos:pos + 4]
        pos += 4
    if with_final:
        fn_ref = refs[pos]
        pos += 1
    o_ref = refs[pos]
    h = h_ref[...]
    xn = _rms(h, nrm_ref[...]).astype(BF16)
    acc = None
    for c in range(n_chunks):
        lo = c * chunk
        g = _dot(xn, wg_ref[:, lo:lo + chunk])
        u = _dot(xn, wu_ref[:, lo:lo + chunk])
        a = (g * _sigmoid(g) * u).astype(BF16)
        d = _dot(a, wd_ref[lo:lo + chunk, :])
        acc = d if acc is None else acc + d
    h = h + 0.5 * acc
    if with_ple:
        hn = _rms(h, pn_ref[...]).astype(BF16)
        gate = _sigmoid(_dot(hn, pwg_ref[...]))
        proj = _dot(p_ref[...].astype(BF16), pwp_ref[...])
        h = h + gate * proj
    o_ref[...] = h
    if with_final:
        refs[pos + 1][...] = _rms(h, fn_ref[...])


def _ffn(h, nrm, wg, wu, wd, ple=None, final_norm=None):
    m, d = h.shape
    dff = wg.shape[1]
    tm = min(ROW_TILE, m)
    chunk = min(FF_CHUNK, dff)
    assert m % tm == 0 and dff % chunk == 0
    row = lambda w: pl.BlockSpec((tm, w), lambda i: (i, 0))
    ins = [h, nrm, wg, wu, wd]
    specs = [row(d), _const_spec((1, d)), _const_spec((d, dff)), _const_spec((d, dff)), _const_spec((dff, d))]
    if ple is not None:
        p, pn, pwg, pwp = ple
        ins += [p, pn, pwg, pwp]
        specs += [row(p.shape[1]), _const_spec((1, d)), _const_spec((d, d)), _const_spec(pwp.shape)]
    if final_norm is not None:
        ins.append(final_norm)
        specs.append(_const_spec((1, d)))
    n_out = 2 if final_norm is not None else 1
    out = pl.pallas_call(
        functools.partial(_ffn_body, n_chunks=dff // chunk, chunk=chunk, with_ple=ple is not None,
                          with_final=final_norm is not None),
        grid=(m // tm,),
        in_specs=specs,
        out_specs=[row(d)] * n_out,
        out_shape=[jax.ShapeDtypeStruct((m, d), F32)] * n_out,
        compiler_params=_params(1),
        name="ffn",
    )(*ins)
    return out


def _proj_body(h_ref, nrm_ref, w_ref, qa_ref, ka_ref, va_ref, kf_ref, vf_ref, zb_ref, zc_ref):
    xn = _rms(h_ref[...], nrm_ref[...]).astype(BF16)
    a = A_WIDTH
    qa_ref[...] = (_dot(xn, w_ref[:, 0:a]) * (A_HEAD_DIM ** -0.5)).astype(BF16)
    k = _dot(xn, w_ref[:, a:2 * a])
    ka_ref[...] = k.astype(BF16)
    kf_ref[...] = k
    v = _dot(xn, w_ref[:, 2 * a:3 * a])
    va_ref[...] = v.astype(BF16)
    vf_ref[...] = v
    b0 = 3 * a
    zb_ref[...] = _dot(xn, w_ref[:, b0:b0 + ZB_WIDTH])
    c0 = b0 + ZB_WIDTH
    zc_ref[:, 0:C_KW] = _dot(xn, w_ref[:, c0:c0 + C_KW]) * (C_KEY_DIM ** -0.5)
    zc_ref[:, C_KW:] = _dot(xn, w_ref[:, c0 + C_KW:c0 + ZC_WIDTH])


def _proj(h, nrm, w, keep):
    m, d = h.shape
    tm = min(ROW_TILE, m)
    assert m % tm == 0 and keep % tm == 0
    nt, nk = m // tm, keep // tm
    row = lambda wd: pl.BlockSpec((tm, wd), lambda i: (i, 0))
    tail = pl.BlockSpec((tm, A_WIDTH), lambda i: (jnp.maximum(i - (nt - nk), 0), 0))
    sds = jax.ShapeDtypeStruct
    return pl.pallas_call(
        _proj_body,
        grid=(nt,),
        in_specs=[row(d), _const_spec((1, d)), _const_spec((d, PROJ_PAD))],
        out_specs=[row(A_WIDTH), row(A_WIDTH), row(A_WIDTH), tail, tail, row(ZB_WIDTH), row(ZC_WIDTH)],
        out_shape=[sds((m, A_WIDTH), BF16), sds((m, A_WIDTH), BF16), sds((m, A_WIDTH), BF16),
                   sds((keep, A_WIDTH), F32), sds((keep, A_WIDTH), F32),
                   sds((m, ZB_WIDTH), F32), sds((m, ZC_WIDTH), F32)],
        compiler_params=_params(1),
        name="proj",
    )(h, nrm, w)


def _attn_bias(dil, qb):
    qi = np.arange(qb)[:, None]
    ki = np.arange(2 * A_SUB)[None, :]
    j = qi + A_SUB - ki
    band = (j >= 0) & (j <= A_SUB)
    slopes = (2.0 ** (-8.0 * np.arange(1, A_HEADS + 1, dtype=np.float32) / A_HEADS)).astype(np.float32)
    bias = -slopes[:, None, None] * (j * dil).astype(np.float32)[None]
    return jnp.asarray(np.where(band[None], bias, np.float32(NEG_BIG)).astype(np.float32))


def _attn_body(q_ref, kp_ref, kc_ref, vp_ref, vc_ref, bias_ref, o_ref, lse_ref, kk, vv, *,
               qb, n_sub, kc_rows, block_axis):
    kk[0:A_SUB, :] = kp_ref[...].astype(BF16)
    vv[0:A_SUB, :] = vp_ref[...].astype(BF16)
    kk[A_SUB:A_SUB + kc_rows, :] = kc_ref[...].astype(BF16)
    vv[A_SUB:A_SUB + kc_rows, :] = vc_ref[...].astype(BF16)
    fill = kk.shape[0] - (A_SUB + kc_rows)
    if fill:
        kk[A_SUB + kc_rows:, :] = jnp.zeros((fill, A_WIDTH), BF16)
        vv[A_SUB + kc_rows:, :] = jnp.zeros((fill, A_WIDTH), BF16)
    lane = lax.broadcasted_iota(jnp.int32, (qb, LANES), 1)
    low = lane < A_HEAD_DIM
    low_f = low.astype(F32)
    head_lanes = (low_f.astype(BF16), (1.0 - low_f).astype(BF16))
    if block_axis is not None:
        first = jnp.where(pl.program_id(block_axis) > 0, 0, A_SUB)
        key_ok = lax.broadcasted_iota(jnp.int32, (qb, 2 * A_SUB), 1) >= first
    for i in range(n_sub):
        rows = slice(i * qb, (i + 1) * qb)
        krows = slice(i * qb, i * qb + 2 * A_SUB)
        lse_tile = jnp.zeros((qb, LANES), F32)
        for hp in range(A_HEADS // 2):
            cols = slice(hp * LANES, (hp + 1) * LANES)
            q2 = q_ref[rows, cols]
            k2 = kk[krows, cols]
            v2 = vv[krows, cols]
            outs = []
            for hh in range(2):
                h = 2 * hp + hh
                s = _dot_nt(q2 * head_lanes[hh], k2) + bias_ref[h]
                if i == 0 and block_axis is not None:
                    s = jnp.where(key_ok, s, NEG_BIG)
                mx = jnp.max(s, axis=-1, keepdims=True)
                p = jnp.exp(s - mx)
                den = jnp.sum(p, axis=-1, keepdims=True)
                outs.append(_dot(p.astype(BF16), v2) / den)
                lse_tile = jnp.where(lane == h, mx + jnp.log(den), lse_tile)
            o_ref[rows, cols] = jnp.where(low, outs[0], outs[1])
        lse_ref[rows, :] = lse_tile


def _attn_prompt(q, k, v, dil):
    t = q.shape[0]
    assert t % (dil * A_SUB) == 0
    tv = t // dil
    bq = min(ATT_ROWS, tv)
    assert tv % bq == 0 and bq % A_SUB == 0
    view = lambda a: a.reshape(tv, dil * A_WIDTH)
    cur = pl.BlockSpec((bq, A_WIDTH), lambda r, n: (n, r))
    prev = pl.BlockSpec((A_SUB, A_WIDTH), lambda r, n: (jnp.maximum(n * (bq // A_SUB) - 1, 0), r))
    o, lse = pl.pallas_call(
        functools.partial(_attn_body, qb=A_SUB, n_sub=bq // A_SUB, kc_rows=bq, block_axis=1),
        grid=(dil, tv // bq),
        in_specs=[cur, prev, cur, prev, cur, _const_spec((A_HEADS, A_SUB, 2 * A_SUB))],
        out_specs=[cur, pl.BlockSpec((bq, LANES), lambda r, n: (n, r))],
        out_shape=[jax.ShapeDtypeStruct((tv, dil * A_WIDTH), F32), jax.ShapeDtypeStruct((tv, dil * LANES), F32)],
        scratch_shapes=[pltpu.VMEM((A_SUB + bq, A_WIDTH), BF16), pltpu.VMEM((A_SUB + bq, A_WIDTH), BF16)],
        compiler_params=_params(2),
        name=f"attn_prompt_d{dil}",
    )(view(q), view(k), view(k), view(v), view(v), _attn_bias(dil, A_SUB))
    return o.reshape(t, A_WIDTH), lse.reshape(t, LANES)


def _attn_decode(q, k_new, v_new, k_cache, v_cache, dil):
    nb, n_past, _ = k_cache.shape
    assert n_past == A_MAX_WINDOW
    tv = n_past // dil
    view = lambda a: a.reshape(nb, tv, dil * A_WIDTH)
    new = pl.BlockSpec((None, DEC_PAD, A_WIDTH), lambda b: (b, 0, 0))
    past = pl.BlockSpec((None, A_SUB, A_WIDTH), lambda b: (b, tv // A_SUB - 1, 0))
    return pl.pallas_call(
        functools.partial(_attn_body, qb=DEC_PAD, n_sub=1, kc_rows=DEC_PAD, block_axis=None),
        grid=(nb,),
        in_specs=[new, past, new, past, new, _const_spec((A_HEADS, DEC_PAD, 2 * A_SUB))],
        out_specs=[new, pl.BlockSpec((None, DEC_PAD, LANES), lambda b: (b, 0, 0))],
        out_shape=[jax.ShapeDtypeStruct((nb, DEC_PAD, A_WIDTH), F32), jax.ShapeDtypeStruct((nb, DEC_PAD, LANES), F32)],
        scratch_shapes=[pltpu.VMEM((2 * A_SUB, A_WIDTH), BF16), pltpu.VMEM((2 * A_SUB, A_WIDTH), BF16)],
        compiler_params=_params(1),
        name=f"attn_decode_d{dil}",
    )(q, view(k_cache), k_new, view(v_cache), v_new, _attn_bias(dil, DEC_PAD))


def _rec_tables(c):
    nl = int(math.log2(c))
    assert 2 ** nl == c
    t = np.arange(c)[:, None]
    i = np.arange(c)[None, :]
    e_rows, masks = [], []
    for lvl in range(nl):
        half = c >> (lvl + 1)
        mid = (t // (2 * half)) * (2 * half) + half
        right = t >= mid
        e_rows.append(np.where(right, (i >= mid) & (i <= t), (i > t) & (i < mid)))
        s_idx = i
        same = (t // (2 * half)) == (s_idx // (2 * half))
        masks.append(same & right & (s_idx < mid))
    masks.append(t == i)
    e_rows.append(i <= t)
    e_rows.append(i > t)
    e = jnp.asarray(np.concatenate(e_rows, 0).astype(np.float32), dtype=BF16)
    m = jnp.asarray(np.stack(masks, 0).astype(np.float32))
    return e, m


def _rec_core(q, k, lf, v, e_ref, m_ref, st_ref, o_ref, *, c, windows):
    nl = int(math.log2(c))
    l1 = lf.astype(BF16)
    r1 = lf - l1.astype(F32)
    l2 = r1.astype(BF16)
    l3 = (r1 - l2.astype(F32)).astype(BF16)
    e = e_ref[...]
    decay = jnp.exp(_dot(e, l1) + _dot(e, l2) + _dot(e, l3))
    seg = lambda n: decay[n * c:(n + 1) * c]
    pq = [(q * seg(n)).astype(BF16) for n in range(nl)] + [q.astype(BF16)]
    pk = [(k * seg(n)).astype(BF16) for n in range(nl)] + [k.astype(BF16)]
    e_cum = seg(nl)
    qg = (q * e_cum).astype(BF16)
    ke = (k * seg(nl + 1)).astype(BF16)
    e_tot = e_cum[c - 1:c, :]
    vb = v.astype(BF16)
    lane = lax.broadcasted_iota(jnp.int32, (c, LANES), 1)
    low = lane < B_VAL_DIM
    for pair in range(2):
        v2 = vb[:, pair * LANES:(pair + 1) * LANES]
        st = st_ref[pair]
        stb = st.astype(BF16)
        outs, new_rows = [], []
        for hh in range(2):
            win, hmask = windows[2 * pair + hh]
            sel = (lambda x: x[:, win]) if hmask is None else (lambda x: x[:, win] * hmask)
            att = None
            for n in range(nl + 1):
                a = _dot_nt(sel(pq[n]), pk[n][:, win]) * m_ref[n]
                att = a if att is None else att + a
            outs.append(_dot(att.astype(BF16), v2) + _dot_nt(sel(qg), stb))
            rows = slice(hh * B_VAL_DIM, (hh + 1) * B_VAL_DIM)
            upd = _dot_tn(v2, sel(ke))
            new_rows.append(st[rows, :] * e_tot[:, win] + upd[rows, :])
        o_ref[:, pair * LANES:(pair + 1) * LANES] = jnp.where(low, outs[0], outs[1])
        st_ref[pair, 0:B_VAL_DIM, :] = new_rows[0]
        st_ref[pair, B_VAL_DIM:, :] = new_rows[1]


def _rec_body(*refs, kind, c, n_valid):
    if kind == "b":
        z_ref, lb_ref, e_ref, m_ref, s0_ref, o_ref, s_ref, st_ref = refs
    else:
        z_ref, cwg_ref, cb_ref, e_ref, m_ref, s0_ref, o_ref, s_ref, st_ref = refs
    chunk = pl.program_id(1)

    @pl.when(chunk == 0)
    def _():
        st_ref[...] = s0_ref[...]

    z = z_ref[...]
    if kind == "b":
        lb = lb_ref[...]
        f = lb + (1.0 - lb) * _sigmoid(z[:, B_KW:2 * B_KW])
        lf = jnp.log(jnp.maximum(f, MIN_F))
        k = 1.0 - f
        zq = z[:, 0:B_KW]
        q = zq * _sigmoid(zq)
        v = z[:, 2 * B_KW:2 * B_KW + B_WIDTH]
        windows = [(slice(h * B_KEY_DIM, (h + 1) * B_KEY_DIM), None) for h in range(B_HEADS)]
    else:
        q = z[:, 0:C_KW]
        k = z[:, C_KW:2 * C_KW]
        v = z[:, 2 * C_KW:2 * C_KW + C_WIDTH]
        x = _dot(z[:, ZC_WIDTH - LANES:].astype(BF16), cwg_ref[...]) + cb_ref[...]
        lf = (jnp.minimum(x, 0.0) - jnp.log1p(jnp.exp(-jnp.abs(x)))) / C_GATE_TEMP
        head = lax.broadcasted_iota(jnp.int32, (1, LANES), 1) // C_KEY_DIM
        windows = [(slice(0, C_KW), (head == h).astype(BF16)) for h in range(C_HEADS)]
    if n_valid < c:
        live = lax.broadcasted_iota(jnp.int32, (c, 1), 0) < n_valid
        q = jnp.where(live, q, 0.0)
        k = jnp.where(live, k, 0.0)
        lf = jnp.where(live, lf, 0.0)
    _rec_core(q, k, lf, v, e_ref, m_ref, st_ref, o_ref, c=c, windows=windows)

    @pl.when(chunk == pl.num_programs(1) - 1)
    def _():
        s_ref[...] = st_ref[...]


def _recurrence(kind, z, extra, s0, n_valid=None):
    nb, t, width = z.shape
    c = min(REC_CHUNK, t)
    assert t % c == 0
    n_valid = c if n_valid is None else n_valid
    e, m = _rec_tables(c)
    st_spec = pl.BlockSpec((None, 2, LANES, LANES), lambda b, j: (b, 0, 0, 0))
    extra_specs = [_const_spec(x.shape) for x in extra]
    return pl.pallas_call(
        functools.partial(_rec_body, kind=kind, c=c, n_valid=n_valid),
        grid=(nb, t // c),
        in_specs=[pl.BlockSpec((None, c, width), lambda b, j: (b, j, 0))] + extra_specs
                 + [_const_spec(e.shape), _const_spec(m.shape), st_spec],
        out_specs=[pl.BlockSpec((None, c, 2 * LANES), lambda b, j: (b, j, 0)), st_spec],
        out_shape=[jax.ShapeDtypeStruct((nb, t, 2 * LANES), F32), jax.ShapeDtypeStruct((nb, 2, LANES, LANES), F32)],
        scratch_shapes=[pltpu.VMEM((2, LANES, LANES), F32)],
        compiler_params=_params(2),
        name=f"recurrence_{kind}",
    )(z, *extra, e, m, s0)


def _mix_body(h_ref, o1_ref, o2_ref, o3_ref, l1_ref, l2_ref, l3_ref, ob_ref, bg_ref, oc_ref, cg_ref,
              bn_ref, cn_ref, ex_ref, bd_ref, w_ref, out_ref, mixed):
    l1, l2, l3 = l1_ref[...], l2_ref[...], l3_ref[...]
    mx = jnp.maximum(jnp.maximum(l1, l2), l3)
    e1, e2, e3 = jnp.exp(l1 - mx), jnp.exp(l2 - mx), jnp.exp(l3 - mx)
    den = e1 + e2 + e3
    ex = ex_ref[...]
    oa = (_dot_split2(e1 / den, ex) * o1_ref[...] + _dot_split2(e2 / den, ex) * o2_ref[...]
          + _dot_split2(e3 / den, ex) * o3_ref[...])
    mixed[:, 0:A_WIDTH] = oa.astype(BF16)
    bd = bd_ref[...]
    ob = ob_ref[...]
    obn = ob * lax.rsqrt(_dot_split2(ob * ob, bd) + RMS_EPS) * bn_ref[...]
    mixed[:, A_WIDTH:A_WIDTH + B_WIDTH] = (obn * _sigmoid(bg_ref[...])).astype(BF16)
    oc = oc_ref[...]
    ocn = oc * lax.rsqrt(_dot_split2(oc * oc, bd) + RMS_EPS) * cn_ref[...]
    cg = cg_ref[...]
    mixed[:, A_WIDTH + B_WIDTH:] = (ocn * (cg * _sigmoid(cg))).astype(BF16)
    out_ref[...] = h_ref[...] + _dot(mixed[...], w_ref[...])


def _mix(h, o_a, lse_a, ob, zb, oc, zc, b_norm, c_norm, w_out):
    m, d = h.shape
    tm = min(ROW_TILE, m)
    assert m % tm == 0
    row = lambda w, blk=0: pl.BlockSpec((tm, w), lambda i: (i, blk))
    expand = np.zeros((LANES, A_WIDTH), np.float32)
    for hd in range(A_HEADS):
        expand[hd, hd * A_HEAD_DIM:(hd + 1) * A_HEAD_DIM] = 1.0
    grp = np.arange(B_WIDTH) // B_VAL_DIM
    blockmean = (grp[:, None] == grp[None, :]).astype(np.float32) / B_VAL_DIM
    d_mix = A_WIDTH + B_WIDTH + C_WIDTH
    return pl.pallas_call(
        _mix_body,
        grid=(m // tm,),
        in_specs=[row(d)] + [row(A_WIDTH)] * 3 + [row(LANES)] * 3
                 + [row(B_WIDTH), row(B_WIDTH, (2 * B_KW + B_WIDTH) // B_WIDTH),
                    row(C_WIDTH), row(C_WIDTH, (2 * C_KW + C_WIDTH) // C_WIDTH)]
                 + [_const_spec((1, B_WIDTH)), _const_spec((1, C_WIDTH)), _const_spec((LANES, A_WIDTH)),
                    _const_spec((B_WIDTH, B_WIDTH)), _const_spec((d_mix, d))],
        out_specs=row(d),
        out_shape=jax.ShapeDtypeStruct((m, d), F32),
        scratch_shapes=[pltpu.VMEM((tm, d_mix), BF16)],
        compiler_params=_params(1),
        name="mix_out",
    )(h, *o_a, *lse_a, ob, zb, oc, zc, b_norm, c_norm,
      jnp.asarray(expand, dtype=BF16), jnp.asarray(blockmean, dtype=BF16), w_out)


def _prep_w_in(w_in):
    sizes = (A_WIDTH, A_WIDTH, A_WIDTH, B_KW, B_KW, B_WIDTH, B_WIDTH, C_KW, C_KW, C_WIDTH, C_GATE_RANK, C_WIDTH)
    cuts = np.cumsum(sizes)[:-1].tolist()
    parts = jnp.split(w_in, cuts, axis=-1)
    clr = jnp.pad(parts[10], ((0, 0), (0, LANES - C_GATE_RANK)))
    return jnp.concatenate(parts[:10] + [parts[11], clr], axis=-1).astype(BF16)


def _state_to_pairs(s):
    nb, nh, dk, dv = s.shape
    st = jnp.swapaxes(s, 2, 3)
    if dk < LANES:
        st = jnp.stack([jnp.pad(st[:, h], ((0, 0), (0, 0), (h * dk, LANES - (h + 1) * dk))) for h in range(nh)], 1)
    return st.reshape(nb, 2, 2 * dv, LANES)


def _pairs_to_state(st, dk):
    nb = st.shape[0]
    st = st.reshape(nb, 4, B_VAL_DIM, LANES)
    if dk < LANES:
        st = jnp.stack([st[:, h, :, h * dk:(h + 1) * dk] for h in range(4)], 1)
    return jnp.swapaxes(st, 2, 3)


def _trunk(x, p, wts, lb_all, final_norm, caches):
    nb, t, d = x.shape
    decode = caches is not None
    assert (t == 1) if decode else (nb == 1)
    m = nb * t
    h = x.reshape(m, d)
    depth = len(wts)
    keep = m if decode else min(A_MAX_WINDOW, t)
    k_rows, v_rows, sb_out, sc_out = [], [], [], []
    y = None
    for i, w in enumerate(wts):
        h, = _ffn(h, w["ffn1_norm"], w["ffn1_w_gate"], w["ffn1_w_up"], w["ffn1_w_down"])
        qa, ka, va, kf, vf, zb, zc = _proj(h, w["mix_norm"], w["w_in"], keep)
        if decode:
            pad = lambda a: jnp.pad(a.reshape(nb, 1, -1), ((0, 0), (0, DEC_PAD - 1), (0, 0)))
            kc = caches[0][i].reshape(nb, A_MAX_WINDOW, A_WIDTH)
            vc = caches[1][i].reshape(nb, A_MAX_WINDOW, A_WIDTH)
            res = [_attn_decode(pad(qa), pad(ka), pad(va), kc, vc, dil) for _, dil in A_CONFIGS]
            o_a = [r[0][:, 0] for r in res]
            lse_a = [r[1][:, 0] for r in res]
            zb3, zc3 = pad(zb), pad(zc)
            sb0, sc0 = _state_to_pairs(caches[2][i]), _state_to_pairs(caches[3][i])
            n_valid = 1
        else:
            res = [_attn_prompt(qa, ka, va, dil) for _, dil in A_CONFIGS]
            o_a = [r[0] for r in res]
            lse_a = [r[1] for r in res]
            zb3, zc3 = zb[None], zc[None]
            sb0 = sc0 = jnp.zeros((1, 2, LANES, LANES), F32)
            n_valid = None
        ob, sb = _recurrence("b", zb3, [lb_all[i]], sb0, n_valid)
        oc, sc = _recurrence("c", zc3, [w["c_w_gate"], w["c_gate_bias"]], sc0, n_valid)
        if decode:
            ob, oc = ob[:, 0], oc[:, 0]
        else:
            ob, oc = ob[0], oc[0]
        h = _mix(h, o_a, lse_a, ob, zb, oc, zc, w["b_out_norm"], w["c_out_norm"], w["w_out"])
        ple = (p[i].reshape(m, -1), w["ple_norm"], w["ple_w_gate"], w["ple_w_proj"])
        if i == depth - 1:
            h, y = _ffn(h, w["ffn2_norm"], w["ffn2_w_gate"], w["ffn2_w_up"], w["ffn2_w_down"], ple, final_norm)
        else:
            h, = _ffn(h, w["ffn2_norm"], w["ffn2_w_gate"], w["ffn2_w_up"], w["ffn2_w_down"], ple)
        k_rows.append(kf.reshape(nb, keep // nb, A_HEADS, A_HEAD_DIM))
        v_rows.append(vf.reshape(nb, keep // nb, A_HEADS, A_HEAD_DIM))
        sb_out.append(_pairs_to_state(sb, B_KEY_DIM))
        sc_out.append(_pairs_to_state(sc, C_KEY_DIM))
    return (y.reshape(nb, t, d), jnp.stack(k_rows), jnp.stack(v_rows), jnp.stack(sb_out), jnp.stack(sc_out))


def kernel(x_prompt, x_sample, cache_k_a, cache_v_a, state_b, state_c, p_prompt, p_sample, ffn1_norm, ffn1_w_gate, ffn1_w_up, ffn1_w_down, mix_norm, w_in, lb_logits, b_out_norm, c_w_gate, c_gate_bias, c_out_norm, w_out, ffn2_norm, ffn2_w_gate, ffn2_w_up, ffn2_w_down, ple_norm, ple_w_gate, ple_w_proj, final_norm):
    depth = w_in.shape[0]
    row = lambda a: a.reshape(1, -1)
    wts = []
    for i in range(depth):
        cwg = jnp.pad(c_w_gate[i], ((0, LANES - C_GATE_RANK), (0, 0))).astype(BF16)
        wts.append(dict(
            ffn1_norm=row(ffn1_norm[i]), ffn1_w_gate=ffn1_w_gate[i].astype(BF16), ffn1_w_up=ffn1_w_up[i].astype(BF16),
            ffn1_w_down=ffn1_w_down[i].astype(BF16), mix_norm=row(mix_norm[i]), w_in=_prep_w_in(w_in[i]),
            b_out_norm=row(b_out_norm[i]), c_w_gate=cwg, c_gate_bias=row(c_gate_bias[i]), c_out_norm=row(c_out_norm[i]),
            w_out=w_out[i].astype(BF16), ffn2_norm=row(ffn2_norm[i]), ffn2_w_gate=ffn2_w_gate[i].astype(BF16),
            ffn2_w_up=ffn2_w_up[i].astype(BF16), ffn2_w_down=ffn2_w_down[i].astype(BF16), ple_norm=row(ple_norm[i]),
            ple_w_gate=ple_w_gate[i].astype(BF16), ple_w_proj=ple_w_proj[i].astype(BF16)))
    sm = jax.nn.softmax(lb_logits.astype(F32), axis=0)
    lb_all = jnp.maximum(jnp.cumsum(sm, axis=0) - sm[0], 0.0).reshape(depth, 1, B_KW)
    fn = row(final_norm)
    y_p, k_p, v_p, sb_p, sc_p = _trunk(x_prompt, p_prompt, wts, lb_all, fn, None)
    y_s, k_s, v_s, sb_s, sc_s = _trunk(x_sample, p_sample, wts, lb_all, fn, (cache_k_a, cache_v_a, state_b, state_c))
    return (y_p, y_s, k_p, v_p, sb_p, sc_p, k_s, v_s, sb_s, sc_s)
```

```python
import functools
import math

import numpy as np
import jax
import jax.numpy as jnp
from jax import lax
from jax.experimental import pallas as pl
from jax.experimental.pallas import tpu as pltpu

F32 = jnp.float32
BF16 = jnp.bfloat16

RMS_EPS = 1e-6
NEG_BIG = -1e30
MIN_F = 1e-30
A_HEADS = 8
A_HEAD_DIM = 64
A_CONFIGS = ((128, 1), (512, 4), (2048, 16))
A_MAX_WINDOW = 2048
A_SUB = 128
B_HEADS = 4
B_KEY_DIM = 128
B_VAL_DIM = 64
C_HEADS = 4
C_KEY_DIM = 32
C_VAL_DIM = 64
C_GATE_RANK = 16
C_GATE_TEMP = 16.0
A_WIDTH = A_HEADS * A_HEAD_DIM
B_KW = B_HEADS * B_KEY_DIM
B_WIDTH = B_HEADS * B_VAL_DIM
C_KW = C_HEADS * C_KEY_DIM
C_WIDTH = C_HEADS * C_VAL_DIM
ZB_WIDTH = 2 * B_KW + 2 * B_WIDTH
ZC_WIDTH = 2 * C_KW + 2 * C_WIDTH + 128
PROJ_PAD = 3 * A_WIDTH + ZB_WIDTH + ZC_WIDTH

LANES = 128
VMEM_LIMIT_BYTES = 56 * 1024 * 1024

ROW_TILE = 512
FF_CHUNK = 1408
ATT_ROWS = 512
REC_CHUNK = 128
DEC_PAD = 16


def _params(n_axes):
    return pltpu.CompilerParams(dimension_semantics=("arbitrary",) * n_axes,
                                vmem_limit_bytes=VMEM_LIMIT_BYTES)


def _const_spec(shape):
    zeros = (0,) * len(shape)
    return pl.BlockSpec(shape, lambda *_: zeros, pipeline_mode=pl.Buffered(1))


def _dot(a, b):
    return jnp.dot(a, b, preferred_element_type=F32)


def _dot_nt(a, b):
    return lax.dot_general(a, b, (((1,), (1,)), ((), ())), preferred_element_type=F32)


def _dot_tn(a, b):
    return lax.dot_general(a, b, (((0,), (0,)), ((), ())), preferred_element_type=F32)


def _dot_split2(x, w):
    hi = x.astype(BF16)
    lo = (x - hi.astype(F32)).astype(BF16)
    return _dot(hi, w) + _dot(lo, w)


def _rms(x, g):
    ms = jnp.mean(x * x, axis=-1, keepdims=True)
    return x * lax.rsqrt(ms + RMS_EPS) * g


def _sigmoid(x):
    return 1.0 / (1.0 + jnp.exp(-x))


def _ffn_body(*refs, n_chunks, chunk, with_ple, with_final):
    refs = list(refs)
    h_ref, nrm_ref, wg_ref, wu_ref, wd_ref = refs[:5]
    pos = 5
    if with_ple:
        p_ref, pn_ref, pwg_ref, pwp_ref = refs[pos:pos + 4]
        pos += 4
    if with_final:
        fn_ref = refs[pos]
        pos += 1
    o_ref = refs[pos]
    h = h_ref[...]
    xn = _rms(h, nrm_ref[...]).astype(BF16)
    acc = None
    for c in range(n_chunks):
        lo = c * chunk
        g = _dot(xn, wg_ref[:, lo:lo + chunk])
        u = _dot(xn, wu_ref[:, lo:lo + chunk])
        a = (g * _sigmoid(g) * u).astype(BF16)
        d = _dot(a, wd_ref[lo:lo + chunk, :])
        acc = d if acc is None else acc + d
    h = h + 0.5 * acc
    if with_ple:
        hn = _rms(h, pn_ref[...]).astype(BF16)
        gate = _sigmoid(_dot(hn, pwg_ref[...]))
        proj = _dot(p_ref[...].astype(BF16), pwp_ref[...])
        h = h + gate * proj
    o_ref[...] = h
    if with_final:
        refs[pos + 1][...] = _rms(h, fn_ref[...])


def _ffn(h, nrm, wg, wu, wd, ple=None, final_norm=None):
    m, d = h.shape
    dff = wg.shape[1]
    tm = min(ROW_TILE, m)
    chunk = min(FF_CHUNK, dff)
    assert m % tm == 0 and dff % chunk == 0
    row = lambda w: pl.BlockSpec((tm, w), lambda i: (i, 0))
    ins = [h, nrm, wg, wu, wd]
    specs = [row(d), _const_spec((1, d)), _const_spec((d, dff)), _const_spec((d, dff)), _const_spec((dff, d))]
    if ple is not None:
        p, pn, pwg, pwp = ple
        ins += [p, pn, pwg, pwp]
        specs += [row(p.shape[1]), _const_spec((1, d)), _const_spec((d, d)), _const_spec(pwp.shape)]
    if final_norm is not None:
        ins.append(final_norm)
        specs.append(_const_spec((1, d)))
    n_out = 2 if final_norm is not None else 1
    out = pl.pallas_call(
        functools.partial(_ffn_body, n_chunks=dff // chunk, chunk=chunk, with_ple=ple is not None,
                          with_final=final_norm is not None),
        grid=(m // tm,),
        in_specs=specs,
        out_specs=[row(d)] * n_out,
        out_shape=[jax.ShapeDtypeStruct((m, d), F32)] * n_out,
        compiler_params=_params(1),
        name="ffn",
    )(*ins)
    return out


def _proj_body(h_ref, nrm_ref, w_ref, *refs, tm, dils):
    n_a = 1 + len(dils)
    a_refs = refs[:n_a]
    kf_ref, vf_ref, zb_ref, zc_ref = refs[n_a:n_a + 4]
    xn = _rms(h_ref[...], nrm_ref[...]).astype(BF16)
    a = A_WIDTH
    for part in range(3):
        cols = slice(part * a, (part + 1) * a)
        z = _dot(xn, w_ref[:, cols])
        if part == 0:
            z = z * (A_HEAD_DIM ** -0.5)
        elif part == 1:
            kf_ref[...] = z
        else:
            vf_ref[...] = z
        a_refs[0][0, :, cols] = z.astype(a_refs[0].dtype)
        if dils:
            stage = refs[-1]
            for cb in range(a // LANES):
                stage[cb] = z[:, cb * LANES:(cb + 1) * LANES]
        for ref, d in zip(a_refs[1:], dils):
            for r in range(d):
                for cb in range(a // LANES):
                    lo = part * a + cb * LANES
                    ref[r, :, lo:lo + LANES] = stage[cb, pl.ds(r, tm // d, stride=d), :].astype(ref.dtype)
    b0 = 3 * a
    zb_ref[...] = _dot(xn, w_ref[:, b0:b0 + ZB_WIDTH])
    c0 = b0 + ZB_WIDTH
    zc_ref[:, 0:C_KW] = _dot(xn, w_ref[:, c0:c0 + C_KW]) * (C_KEY_DIM ** -0.5)
    zc_ref[:, C_KW:] = _dot(xn, w_ref[:, c0 + C_KW:c0 + ZC_WIDTH])


def _proj(h, nrm, w, keep, dils, a_dtype):
    m, d = h.shape
    tm = min(ROW_TILE, m)
    assert m % tm == 0 and keep % tm == 0
    nt, nk = m // tm, keep // tm
    row = lambda wd: pl.BlockSpec((tm, wd), lambda i: (i, 0))
    tail = pl.BlockSpec((tm, A_WIDTH), lambda i: (jnp.maximum(i - (nt - nk), 0), 0))
    sds = jax.ShapeDtypeStruct
    all_d = (1,) + tuple(dils)
    return pl.pallas_call(
        functools.partial(_proj_body, tm=tm, dils=tuple(dils)),
        grid=(nt,),
        in_specs=[row(d), _const_spec((1, d)), _const_spec((d, PROJ_PAD))],
        out_specs=[pl.BlockSpec((dd, tm // dd, 3 * A_WIDTH), lambda i: (0, i, 0)) for dd in all_d]
                  + [tail, tail, row(ZB_WIDTH), row(ZC_WIDTH)],
        out_shape=[sds((dd, m // dd, 3 * A_WIDTH), a_dtype) for dd in all_d]
                  + [sds((keep, A_WIDTH), F32), sds((keep, A_WIDTH), F32),
                     sds((m, ZB_WIDTH), F32), sds((m, ZC_WIDTH), F32)],
        scratch_shapes=[pltpu.VMEM((A_WIDTH // LANES, tm, LANES), F32)] if dils else [],
        compiler_params=_params(1),
        name="proj",
    )(h, nrm, w)


def _alibi_slopes():
    return (2.0 ** (-8.0 * np.arange(1, A_HEADS + 1, dtype=np.float32) / A_HEADS)).astype(np.float32)


def _attn_bias(dil):
    qi = np.arange(A_SUB)[:, None]
    ki = np.arange(2 * A_SUB)[None, :]
    j = qi + A_SUB - ki
    band = (j >= 0) & (j <= A_SUB)
    bias = -_alibi_slopes()[:, None, None] * (j * dil).astype(np.float32)[None]
    return jnp.asarray(np.where(band[None], bias, np.float32(NEG_BIG)).astype(np.float32))


def _attn_body(q_ref, kp_ref, kc_ref, vp_ref, vc_ref, bias_ref, o_ref, lse_ref, kk, vv, *, n_sub):
    qb = A_SUB
    kk[0:A_SUB, :] = kp_ref[...]
    vv[0:A_SUB, :] = vp_ref[...]
    kk[A_SUB:, :] = kc_ref[...]
    vv[A_SUB:, :] = vc_ref[...]
    lane = lax.broadcasted_iota(jnp.int32, (qb, LANES), 1)
    low = lane < A_HEAD_DIM
    low_f = low.astype(F32)
    head_lanes = (low_f.astype(BF16), (1.0 - low_f).astype(BF16))
    first = jnp.where(pl.program_id(1) > 0, 0, A_SUB)
    key_ok = lax.broadcasted_iota(jnp.int32, (qb, 2 * A_SUB), 1) >= first
    for i in range(n_sub):
        rows = slice(i * qb, (i + 1) * qb)
        krows = slice(i * qb, i * qb + 2 * A_SUB)
        lse_tile = jnp.zeros((qb, LANES), F32)
        for hp in range(A_HEADS // 2):
            cols = slice(hp * LANES, (hp + 1) * LANES)
            q2 = q_ref[rows, cols]
            k2 = kk[krows, cols]
            v2 = vv[krows, cols]
            outs = []
            for hh in range(2):
                h = 2 * hp + hh
                s = _dot_nt(q2 * head_lanes[hh], k2) + bias_ref[h]
                if i == 0:
                    s = jnp.where(key_ok, s, NEG_BIG)
                mx = jnp.max(s, axis=-1, keepdims=True)
                p = jnp.exp(s - mx)
                den = jnp.sum(p, axis=-1, keepdims=True)
                outs.append(_dot(p.astype(BF16), v2) / den)
                lse_tile = jnp.where(lane == h, mx + jnp.log(den), lse_tile)
            o_ref[rows, cols] = jnp.where(low, outs[0], outs[1])
        lse_ref[rows, :] = lse_tile


def _attn_prompt(qkv, dil):
    _, tv, _ = qkv.shape
    assert qkv.shape[0] == dil and tv % A_SUB == 0
    bq = min(ATT_ROWS, tv)
    assert tv % bq == 0 and bq % A_SUB == 0
    cur = lambda part: pl.BlockSpec((None, bq, A_WIDTH), lambda r, n: (r, n, part))
    prev = lambda part: pl.BlockSpec((None, A_SUB, A_WIDTH),
                                     lambda r, n: (r, jnp.maximum(n * (bq // A_SUB) - 1, 0), part))
    return pl.pallas_call(
        functools.partial(_attn_body, n_sub=bq // A_SUB),
        grid=(dil, tv // bq),
        in_specs=[cur(0), prev(1), cur(1), prev(2), cur(2), _const_spec((A_HEADS, A_SUB, 2 * A_SUB))],
        out_specs=[cur(0), pl.BlockSpec((None, bq, LANES), lambda r, n: (r, n, 0))],
        out_shape=[jax.ShapeDtypeStruct((dil, tv, A_WIDTH), F32), jax.ShapeDtypeStruct((dil, tv, LANES), F32)],
        scratch_shapes=[pltpu.VMEM((A_SUB + bq, A_WIDTH), BF16), pltpu.VMEM((A_SUB + bq, A_WIDTH), BF16)],
        compiler_params=_params(2),
        name=f"attn_prompt_d{dil}",
    )(qkv, qkv, qkv, qkv, qkv, _attn_bias(dil))


def _decode_tables(n_past):
    dist = n_past - np.arange(n_past)
    mult = np.zeros(n_past, np.float32)
    for window, dil in A_CONFIGS:
        mult += ((dist % dil == 0) & (dist // dil <= window // dil)).astype(np.float32)
    bias = -_alibi_slopes()[:, None] * dist.astype(np.float32)[None, :]
    return jnp.asarray(bias), jnp.asarray(mult[None, :])


def _attn_decode_body(q_ref, kn_ref, vn_ref, kc_ref, vc_ref, bias_ref, mult_ref, o_ref):
    q_t = q_ref[...]
    s_new = jnp.sum(q_t * kn_ref[...], axis=0, keepdims=True)
    mult = mult_ref[...]
    live = mult > 0.0
    n_cfg = float(len(A_CONFIGS))
    lane = lax.broadcasted_iota(jnp.int32, q_t.shape, 1)
    out = jnp.zeros(q_t.shape, F32)
    for h in range(A_HEADS):
        s = jnp.sum(kc_ref[h] * q_t[:, h:h + 1], axis=0, keepdims=True) + bias_ref[h:h + 1, :]
        s = jnp.where(live, s, NEG_BIG)
        s0 = s_new[:, h:h + 1]
        mx = jnp.maximum(jnp.max(s, axis=-1, keepdims=True), s0)
        p = jnp.exp(s - mx) * mult
        p0 = jnp.exp(s0 - mx) * n_cfg
        den = jnp.sum(p, axis=-1, keepdims=True) + p0
        acc = jnp.sum(vc_ref[h] * p, axis=-1, keepdims=True) + p0 * vn_ref[:, h:h + 1]
        out = jnp.where(lane == h, acc / den, out)
    o_ref[...] = out


def _attn_decode(q_t, kn_t, vn_t, k_cache_t, v_cache_t, layer):
    nb = q_t.shape[0]
    n_past = k_cache_t.shape[-1]
    assert n_past == A_MAX_WINDOW
    bias, mult = _decode_tables(n_past)
    new = pl.BlockSpec((None, A_HEAD_DIM, A_HEADS), lambda b: (b, 0, 0))
    past = pl.BlockSpec((None, None, A_HEADS, A_HEAD_DIM, n_past), lambda b: (layer, b, 0, 0, 0))
    return pl.pallas_call(
        _attn_decode_body,
        grid=(nb,),
        in_specs=[new, new, new, past, past, _const_spec(bias.shape), _const_spec(mult.shape)],
        out_specs=new,
        out_shape=jax.ShapeDtypeStruct((nb, A_HEAD_DIM, A_HEADS), F32),
        compiler_params=_params(1),
        name="attn_decode",
    )(q_t, kn_t, vn_t, k_cache_t, v_cache_t, bias, mult)


def _rec_tables(c):
    nl = int(math.log2(c))
    assert 2 ** nl == c
    t = np.arange(c)[:, None]
    i = np.arange(c)[None, :]
    e_rows, masks = [], []
    for lvl in range(nl):
        half = c >> (lvl + 1)
        mid = (t // (2 * half)) * (2 * half) + half
        right = t >= mid
        e_rows.append(np.where(right, (i >= mid) & (i <= t), (i > t) & (i < mid)))
        s_idx = i
        same = (t // (2 * half)) == (s_idx // (2 * half))
        masks.append(same & right & (s_idx < mid))
    masks.append(t == i)
    e_rows.append(i <= t)
    e_rows.append(i > t)
    e = jnp.asarray(np.concatenate(e_rows, 0).astype(np.float32), dtype=BF16)
    m = jnp.asarray(np.stack(masks, 0).astype(np.float32))
    return e, m


def _rec_core(q, k, lf, v, e_ref, m_ref, st_ref, o_ref, *, c, windows):
    nl = int(math.log2(c))
    l1 = lf.astype(BF16)
    r1 = lf - l1.astype(F32)
    l2 = r1.astype(BF16)
    l3 = (r1 - l2.astype(F32)).astype(BF16)
    e = e_ref[...]
    decay = jnp.exp(_dot(e, l1) + _dot(e, l2) + _dot(e, l3))
    seg = lambda n: decay[n * c:(n + 1) * c]
    pq = [(q * seg(n)).astype(BF16) for n in range(nl)] + [q.astype(BF16)]
    pk = [(k * seg(n)).astype(BF16) for n in range(nl)] + [k.astype(BF16)]
    e_cum = seg(nl)
    qg = (q * e_cum).astype(BF16)
    ke = (k * seg(nl + 1)).astype(BF16)
    e_tot = e_cum[c - 1:c, :]
    vb = v.astype(BF16)
    lane = lax.broadcasted_iota(jnp.int32, (c, LANES), 1)
    low = lane < B_VAL_DIM
    for pair in range(2):
        v2 = vb[:, pair * LANES:(pair + 1) * LANES]
        st = st_ref[pair]
        stb = st.astype(BF16)
        outs, new_rows = [], []
        for hh in range(2):
            win, hmask = windows[2 * pair + hh]
            sel = (lambda x: x[:, win]) if hmask is None else (lambda x: x[:, win] * hmask)
            att = None
            for n in range(nl + 1):
                a = _dot_nt(sel(pq[n]), pk[n][:, win]) * m_ref[n]
                att = a if att is None else att + a
            outs.append(_dot(att.astype(BF16), v2) + _dot_nt(sel(qg), stb))
            rows = slice(hh * B_VAL_DIM, (hh + 1) * B_VAL_DIM)
            upd = _dot_tn(v2, sel(ke))
            new_rows.append(st[rows, :] * e_tot[:, win] + upd[rows, :])
        o_ref[:, pair * LANES:(pair + 1) * LANES] = jnp.where(low, outs[0], outs[1])
        st_ref[pair, 0:B_VAL_DIM, :] = new_rows[0]
        st_ref[pair, B_VAL_DIM:, :] = new_rows[1]


def _rec_body(*refs, kind, c, n_valid):
    if kind == "b":
        z_ref, lb_ref, e_ref, m_ref, s0_ref, o_ref, s_ref, st_ref = refs
    else:
        z_ref, cwg_ref, cb_ref, e_ref, m_ref, s0_ref, o_ref, s_ref, st_ref = refs
    chunk = pl.program_id(1)

    @pl.when(chunk == 0)
    def _():
        st_ref[...] = s0_ref[...]

    z = z_ref[...]
    if kind == "b":
        lb = lb_ref[...]
        f = lb + (1.0 - lb) * _sigmoid(z[:, B_KW:2 * B_KW])
        lf = jnp.log(jnp.maximum(f, MIN_F))
        k = 1.0 - f
        zq = z[:, 0:B_KW]
        q = zq * _sigmoid(zq)
        v = z[:, 2 * B_KW:2 * B_KW + B_WIDTH]
        windows = [(slice(h * B_KEY_DIM, (h + 1) * B_KEY_DIM), None) for h in range(B_HEADS)]
    else:
        q = z[:, 0:C_KW]
        k = z[:, C_KW:2 * C_KW]
        v = z[:, 2 * C_KW:2 * C_KW + C_WIDTH]
        x = _dot(z[:, ZC_WIDTH - LANES:].astype(BF16), cwg_ref[...]) + cb_ref[...]
        lf = (jnp.minimum(x, 0.0) - jnp.log1p(jnp.exp(-jnp.abs(x)))) / C_GATE_TEMP
        head = lax.broadcasted_iota(jnp.int32, (1, LANES), 1) // C_KEY_DIM
        windows = [(slice(0, C_KW), (head == h).astype(BF16)) for h in range(C_HEADS)]
    if n_valid < c:
        live = lax.broadcasted_iota(jnp.int32, (c, 1), 0) < n_valid
        q = jnp.where(live, q, 0.0)
        k = jnp.where(live, k, 0.0)
        lf = jnp.where(live, lf, 0.0)
    _rec_core(q, k, lf, v, e_ref, m_ref, st_ref, o_ref, c=c, windows=windows)

    @pl.when(chunk == pl.num_programs(1) - 1)
    def _():
        s_ref[...] = st_ref[...]


def _recurrence(kind, z, extra, s0, n_valid=None):
    nb, t, width = z.shape
    c = min(REC_CHUNK, t)
    assert t % c == 0
    n_valid = c if n_valid is None else n_valid
    e, m = _rec_tables(c)
    st_spec = pl.BlockSpec((None, 2, LANES, LANES), lambda b, j: (b, 0, 0, 0))
    extra_specs = [_const_spec(x.shape) for x in extra]
    return pl.pallas_call(
        functools.partial(_rec_body, kind=kind, c=c, n_valid=n_valid),
        grid=(nb, t // c),
        in_specs=[pl.BlockSpec((None, c, width), lambda b, j: (b, j, 0))] + extra_specs
                 + [_const_spec(e.shape), _const_spec(m.shape), st_spec],
        out_specs=[pl.BlockSpec((None, c, 2 * LANES), lambda b, j: (b, j, 0)), st_spec],
        out_shape=[jax.ShapeDtypeStruct((nb, t, 2 * LANES), F32), jax.ShapeDtypeStruct((nb, 2, LANES, LANES), F32)],
        scratch_shapes=[pltpu.VMEM((2, LANES, LANES), F32)],
        compiler_params=_params(2),
        name=f"recurrence_{kind}",
    )(z, *extra, e, m, s0)


def _mix_body(*refs, tm, dils):
    h_ref = refs[0]
    n_att = 2 * len(dils) if dils else 1
    att_refs = refs[1:1 + n_att]
    ob_ref, bg_ref, oc_ref, cg_ref, bn_ref, cn_ref, ex_ref, bd_ref, w_ref, out_ref, mixed = refs[1 + n_att:12 + n_att]
    if dils:
        o_refs, l_refs = att_refs[:len(dils)], att_refs[len(dils):]
        o_nat, l_nat = refs[12 + n_att], refs[13 + n_att]
        n_cb = A_WIDTH // LANES
        lses = []
        for c, d in enumerate(dils):
            if d == 1:
                lses.append(l_refs[c][0])
                continue
            for r in range(d):
                l_nat[c, pl.ds(r, tm // d, stride=d), :] = l_refs[c][r]
                for cb in range(n_cb):
                    o_nat[c, cb, pl.ds(r, tm // d, stride=d), :] = o_refs[c][r, :, cb * LANES:(cb + 1) * LANES]
            lses.append(l_nat[c])
        mx = functools.reduce(jnp.maximum, lses)
        es = [jnp.exp(l - mx) for l in lses]
        den = functools.reduce(lambda a, b: a + b, es)
        ex = ex_ref[...]
        wts = [_dot_split2(e / den, ex) for e in es]
        for cb in range(n_cb):
            cs = slice(cb * LANES, (cb + 1) * LANES)
            terms = [wts[c][:, cs] * (o_refs[c][0, :, cs] if d == 1 else o_nat[c, cb]) for c, d in enumerate(dils)]
            mixed[:, cs] = functools.reduce(lambda a, b: a + b, terms).astype(BF16)
    else:
        mixed[:, 0:A_WIDTH] = att_refs[0][...].astype(BF16)
    bd = bd_ref[...]
    ob = ob_ref[...]
    obn = ob * lax.rsqrt(_dot_split2(ob * ob, bd) + RMS_EPS) * bn_ref[...]
    mixed[:, A_WIDTH:A_WIDTH + B_WIDTH] = (obn * _sigmoid(bg_ref[...])).astype(BF16)
    oc = oc_ref[...]
    ocn = oc * lax.rsqrt(_dot_split2(oc * oc, bd) + RMS_EPS) * cn_ref[...]
    cg = cg_ref[...]
    mixed[:, A_WIDTH + B_WIDTH:] = (ocn * (cg * _sigmoid(cg))).astype(BF16)
    out_ref[...] = h_ref[...] + _dot(mixed[...], w_ref[...])


def _mix(h, att, dils, ob, zb, oc, zc, b_norm, c_norm, w_out):
    m, d = h.shape
    tm = min(ROW_TILE, m)
    assert m % tm == 0
    row = lambda w, blk=0: pl.BlockSpec((tm, w), lambda i: (i, blk))
    if dils:
        att_specs = [pl.BlockSpec((dd, tm // dd, wd), lambda i: (0, i, 0))
                     for wd in (A_WIDTH, LANES) for dd in dils]
        att_scratch = [pltpu.VMEM((len(dils), A_WIDTH // LANES, tm, LANES), F32),
                       pltpu.VMEM((len(dils), tm, LANES), F32)]
    else:
        att_specs, att_scratch = [row(A_WIDTH)], []
    expand = np.zeros((LANES, A_WIDTH), np.float32)
    for hd in range(A_HEADS):
        expand[hd, hd * A_HEAD_DIM:(hd + 1) * A_HEAD_DIM] = 1.0
    grp = np.arange(B_WIDTH) // B_VAL_DIM
    blockmean = (grp[:, None] == grp[None, :]).astype(np.float32) / B_VAL_DIM
    d_mix = A_WIDTH + B_WIDTH + C_WIDTH
    return pl.pallas_call(
        functools.partial(_mix_body, tm=tm, dils=tuple(dils)),
        grid=(m // tm,),
        in_specs=[row(d)] + att_specs
                 + [row(B_WIDTH), row(B_WIDTH, (2 * B_KW + B_WIDTH) // B_WIDTH),
                    row(C_WIDTH), row(C_WIDTH, (2 * C_KW + C_WIDTH) // C_WIDTH)]
                 + [_const_spec((1, B_WIDTH)), _const_spec((1, C_WIDTH)), _const_spec((LANES, A_WIDTH)),
                    _const_spec((B_WIDTH, B_WIDTH)), _const_spec((d_mix, d))],
        out_specs=row(d),
        out_shape=jax.ShapeDtypeStruct((m, d), F32),
        scratch_shapes=[pltpu.VMEM((tm, d_mix), BF16)] + att_scratch,
        compiler_params=_params(1),
        name="mix_out",
    )(h, *att, ob, zb, oc, zc, b_norm, c_norm,
      jnp.asarray(expand, dtype=BF16), jnp.asarray(blockmean, dtype=BF16), w_out)


def _prep_w_in(w_in):
    sizes = (A_WIDTH, A_WIDTH, A_WIDTH, B_KW, B_KW, B_WIDTH, B_WIDTH, C_KW, C_KW, C_WIDTH, C_GATE_RANK, C_WIDTH)
    cuts = np.cumsum(sizes)[:-1].tolist()
    parts = jnp.split(w_in, cuts, axis=-1)
    clr = jnp.pad(parts[10], ((0, 0), (0, LANES - C_GATE_RANK)))
    return jnp.concatenate(parts[:10] + [parts[11], clr], axis=-1).astype(BF16)


def _state_to_pairs(s):
    nb, nh, dk, dv = s.shape
    st = jnp.swapaxes(s, 2, 3)
    if dk < LANES:
        st = jnp.stack([jnp.pad(st[:, h], ((0, 0), (0, 0), (h * dk, LANES - (h + 1) * dk))) for h in range(nh)], 1)
    return st.reshape(nb, 2, 2 * dv, LANES)


def _pairs_to_state(st, dk):
    nb = st.shape[0]
    st = st.reshape(nb, 4, B_VAL_DIM, LANES)
    if dk < LANES:
        st = jnp.stack([st[:, h, :, h * dk:(h + 1) * dk] for h in range(4)], 1)
    return jnp.swapaxes(st, 2, 3)


def _trunk(x, p, wts, lb_all, final_norm, caches):
    nb, t, d = x.shape
    decode = caches is not None
    assert (t == 1) if decode else (nb == 1)
    m = nb * t
    h = x.reshape(m, d)
    depth = len(wts)
    keep = m if decode else min(A_MAX_WINDOW, t)
    k_rows, v_rows, sb_out, sc_out = [], [], [], []
    y = None
    for i, w in enumerate(wts):
        h, = _ffn(h, w["ffn1_norm"], w["ffn1_w_gate"], w["ffn1_w_up"], w["ffn1_w_down"])
        if decode:
            qkv, kf, vf, zb, zc = _proj(h, w["mix_norm"], w["w_in"], keep, (), F32)
            pad = lambda a: jnp.pad(a.reshape(nb, 1, -1), ((0, 0), (0, DEC_PAD - 1), (0, 0)))
            cols = lambda a: jnp.swapaxes(a.reshape(nb, A_HEADS, A_HEAD_DIM), 1, 2)
            o_t = _attn_decode(cols(qkv[0, :, 0:A_WIDTH]), cols(kf), cols(vf), caches[0], caches[1], i)
            att, dils = [jnp.swapaxes(o_t, 1, 2).reshape(m, A_WIDTH)], ()
            zb3, zc3 = pad(zb), pad(zc)
            sb0, sc0 = _state_to_pairs(caches[2][i]), _state_to_pairs(caches[3][i])
            n_valid = 1
        else:
            dils = tuple(dil for _, dil in A_CONFIGS)
            assert dils[0] == 1
            *qkvs, kf, vf, zb, zc = _proj(h, w["mix_norm"], w["w_in"], keep, dils[1:], BF16)
            res = [_attn_prompt(a, dil) for a, dil in zip(qkvs, dils)]
            att = [r[0] for r in res] + [r[1] for r in res]
            zb3, zc3 = zb[None], zc[None]
            sb0 = sc0 = jnp.zeros((1, 2, LANES, LANES), F32)
            n_valid = None
        ob, sb = _recurrence("b", zb3, [lb_all[i]], sb0, n_valid)
        oc, sc = _recurrence("c", zc3, [w["c_w_gate"], w["c_gate_bias"]], sc0, n_valid)
        if decode:
            ob, oc = ob[:, 0], oc[:, 0]
        else:
            ob, oc = ob[0], oc[0]
        h = _mix(h, att, dils, ob, zb, oc, zc, w["b_out_norm"], w["c_out_norm"], w["w_out"])
        ple = (p[i].reshape(m, -1), w["ple_norm"], w["ple_w_gate"], w["ple_w_proj"])
        if i == depth - 1:
            h, y = _ffn(h, w["ffn2_norm"], w["ffn2_w_gate"], w["ffn2_w_up"], w["ffn2_w_down"], ple, final_norm)
        else:
            h, = _ffn(h, w["ffn2_norm"], w["ffn2_w_gate"], w["ffn2_w_up"], w["ffn2_w_down"], ple)
        k_rows.append(kf.reshape(nb, keep // nb, A_HEADS, A_HEAD_DIM))
        v_rows.append(vf.reshape(nb, keep // nb, A_HEADS, A_HEAD_DIM))
        sb_out.append(_pairs_to_state(sb, B_KEY_DIM))
        sc_out.append(_pairs_to_state(sc, C_KEY_DIM))
    return (y.reshape(nb, t, d), jnp.stack(k_rows), jnp.stack(v_rows), jnp.stack(sb_out), jnp.stack(sc_out))


def kernel(x_prompt, x_sample, cache_k_a, cache_v_a, state_b, state_c, p_prompt, p_sample, ffn1_norm, ffn1_w_gate, ffn1_w_up, ffn1_w_down, mix_norm, w_in, lb_logits, b_out_norm, c_w_gate, c_gate_bias, c_out_norm, w_out, ffn2_norm, ffn2_w_gate, ffn2_w_up, ffn2_w_down, ple_norm, ple_w_gate, ple_w_proj, final_norm):
    depth = w_in.shape[0]
    row = lambda a: a.reshape(1, -1)
    wts = []
    for i in range(depth):
        cwg = jnp.pad(c_w_gate[i], ((0, LANES - C_GATE_RANK), (0, 0))).astype(BF16)
        wts.append(dict(
            ffn1_norm=row(ffn1_norm[i]), ffn1_w_gate=ffn1_w_gate[i].astype(BF16), ffn1_w_up=ffn1_w_up[i].astype(BF16),
            ffn1_w_down=ffn1_w_down[i].astype(BF16), mix_norm=row(mix_norm[i]), w_in=_prep_w_in(w_in[i]),
            b_out_norm=row(b_out_norm[i]), c_w_gate=cwg, c_gate_bias=row(c_gate_bias[i]), c_out_norm=row(c_out_norm[i]),
            w_out=w_out[i].astype(BF16), ffn2_norm=row(ffn2_norm[i]), ffn2_w_gate=ffn2_w_gate[i].astype(BF16),
            ffn2_w_up=ffn2_w_up[i].astype(BF16), ffn2_w_down=ffn2_w_down[i].astype(BF16), ple_norm=row(ple_norm[i]),
            ple_w_gate=ple_w_gate[i].astype(BF16), ple_w_proj=ple_w_proj[i].astype(BF16)))
    sm = jax.nn.softmax(lb_logits.astype(F32), axis=0)
    lb_all = jnp.maximum(jnp.cumsum(sm, axis=0) - sm[0], 0.0).reshape(depth, 1, B_KW)
    fn = row(final_norm)
    y_p, k_p, v_p, sb_p, sc_p = _trunk(x_prompt, p_prompt, wts, lb_all, fn, None)
    k_cache_t = jnp.transpose(cache_k_a, (0, 1, 3, 4, 2))
    v_cache_t = jnp.transpose(cache_v_a, (0, 1, 3, 4, 2))
    y_s, k_s, v_s, sb_s, sc_s = _trunk(x_sample, p_sample, wts, lb_all, fn, (k_cache_t, v_cache_t, state_b, state_c))
    return (y_p, y_s, k_p, v_p, sb_p, sc_p, k_s, v_s, sb_s, sc_s)
```

```python
import functools
import math

import numpy as np
import jax
import jax.numpy as jnp
from jax import lax
from jax.experimental import pallas as pl
from jax.experimental.pallas import tpu as pltpu

F32 = jnp.float32
BF16 = jnp.bfloat16

RMS_EPS = 1e-6
NEG_BIG = -1e30
MIN_F = 1e-30
A_HEADS = 8
A_HEAD_DIM = 64
A_CONFIGS = ((128, 1), (512, 4), (2048, 16))
A_MAX_WINDOW = 2048
A_SUB = 128
B_HEADS = 4
B_KEY_DIM = 128
B_VAL_DIM = 64
C_HEADS = 4
C_KEY_DIM = 32
C_VAL_DIM = 64
C_GATE_RANK = 16
C_GATE_TEMP = 16.0
A_WIDTH = A_HEADS * A_HEAD_DIM
B_KW = B_HEADS * B_KEY_DIM
B_WIDTH = B_HEADS * B_VAL_DIM
C_KW = C_HEADS * C_KEY_DIM
C_WIDTH = C_HEADS * C_VAL_DIM
ZB_WIDTH = 2 * B_KW + 2 * B_WIDTH
ZC_WIDTH = 2 * C_KW + 2 * C_WIDTH + 128
PROJ_PAD = 3 * A_WIDTH + ZB_WIDTH + ZC_WIDTH

LANES = 128
VMEM_LIMIT_BYTES = 56 * 1024 * 1024

ROW_TILE = 512
FF_CHUNK = 1408
ATT_ROWS = 512
REC_CHUNK = 128
DEC_PAD = 16


def _params(n_axes):
    return pltpu.CompilerParams(dimension_semantics=("arbitrary",) * n_axes,
                                vmem_limit_bytes=VMEM_LIMIT_BYTES)


def _const_spec(shape):
    zeros = (0,) * len(shape)
    return pl.BlockSpec(shape, lambda *_: zeros, pipeline_mode=pl.Buffered(1))


def _dot(a, b):
    return jnp.dot(a, b, preferred_element_type=F32)


def _dot_nt(a, b):
    return lax.dot_general(a, b, (((1,), (1,)), ((), ())), preferred_element_type=F32)


def _dot_tn(a, b):
    return lax.dot_general(a, b, (((0,), (0,)), ((), ())), preferred_element_type=F32)


def _dot_split2(x, w):
    hi = x.astype(BF16)
    lo = (x - hi.astype(F32)).astype(BF16)
    return _dot(hi, w) + _dot(lo, w)


def _rms(x, g):
    ms = jnp.mean(x * x, axis=-1, keepdims=True)
    return x * lax.rsqrt(ms + RMS_EPS) * g


def _sigmoid(x):
    return 1.0 / (1.0 + jnp.exp(-x))


def _ffn_body(*refs, n_chunks, chunk, with_ple, with_final):
    refs = list(refs)
    h_ref, nrm_ref, wg_ref, wu_ref, wd_ref = refs[:5]
    pos = 5
    if with_ple:
        p_ref, pn_ref, pwg_ref, pwp_ref = refs[pos:pos + 4]
        pos += 4
    if with_final:
        fn_ref = refs[pos]
        pos += 1
    o_ref = refs[pos]
    h = h_ref[...]
    xn = _rms(h, nrm_ref[...]).astype(BF16)
    acc = None
    for c in range(n_chunks):
        lo = c * chunk
        g = _dot(xn, wg_ref[:, lo:lo + chunk])
        u = _dot(xn, wu_ref[:, lo:lo + chunk])
        a = (g * _sigmoid(g) * u).astype(BF16)
        d = _dot(a, wd_ref[lo:lo + chunk, :])
        acc = d if acc is None else acc + d
    h = h + 0.5 * acc
    if with_ple:
        hn = _rms(h, pn_ref[...]).astype(BF16)
        gate = _sigmoid(_dot(hn, pwg_ref[...]))
        proj = _dot(p_ref[...].astype(BF16), pwp_ref[...])
        h = h + gate * proj
    o_ref[...] = h
    if with_final:
        refs[pos + 1][...] = _rms(h, fn_ref[...])


def _ffn(h, nrm, wg, wu, wd, ple=None, final_norm=None):
    m, d = h.shape
    dff = wg.shape[1]
    tm = min(ROW_TILE, m)
    chunk = min(FF_CHUNK, dff)
    assert m % tm == 0 and dff % chunk == 0
    row = lambda w: pl.BlockSpec((tm, w), lambda i: (i, 0))
    ins = [h, nrm, wg, wu, wd]
    specs = [row(d), _const_spec((1, d)), _const_spec((d, dff)), _const_spec((d, dff)), _const_spec((dff, d))]
    if ple is not None:
        p, pn, pwg, pwp = ple
        ins += [p, pn, pwg, pwp]
        specs += [row(p.shape[1]), _const_spec((1, d)), _const_spec((d, d)), _const_spec(pwp.shape)]
    if final_norm is not None:
        ins.append(final_norm)
        specs.append(_const_spec((1, d)))
    n_out = 2 if final_norm is not None else 1
    out = pl.pallas_call(
        functools.partial(_ffn_body, n_chunks=dff // chunk, chunk=chunk, with_ple=ple is not None,
                          with_final=final_norm is not None),
        grid=(m // tm,),
        in_specs=specs,
        out_specs=[row(d)] * n_out,
        out_shape=[jax.ShapeDtypeStruct((m, d), F32)] * n_out,
        compiler_params=_params(1),
        name="ffn",
    )(*ins)
    return out


def _proj_body(h_ref, nrm_ref, w_ref, *refs, tm, dils):
    n_a = 1 + len(dils)
    a_refs = refs[:n_a]
    kf_ref, vf_ref, zb_ref, zc_ref = refs[n_a:n_a + 4]
    xn = _rms(h_ref[...], nrm_ref[...]).astype(BF16)
    a = A_WIDTH
    for part in range(3):
        cols = slice(part * a, (part + 1) * a)
        z = _dot(xn, w_ref[:, cols])
        if part == 0:
            z = z * (A_HEAD_DIM ** -0.5)
        elif part == 1:
            kf_ref[...] = z
        else:
            vf_ref[...] = z
        a_refs[0][0, :, cols] = z.astype(a_refs[0].dtype)
        if dils:
            stage = refs[-1]
            for cb in range(a // LANES):
                stage[cb] = z[:, cb * LANES:(cb + 1) * LANES]
        for ref, d in zip(a_refs[1:], dils):
            for r in range(d):
                for cb in range(a // LANES):
                    lo = part * a + cb * LANES
                    ref[r, :, lo:lo + LANES] = stage[cb, pl.ds(r, tm // d, stride=d), :].astype(ref.dtype)
    b0 = 3 * a
    zb_ref[...] = _dot(xn, w_ref[:, b0:b0 + ZB_WIDTH])
    c0 = b0 + ZB_WIDTH
    zc_ref[:, 0:C_KW] = _dot(xn, w_ref[:, c0:c0 + C_KW]) * (C_KEY_DIM ** -0.5)
    zc_ref[:, C_KW:] = _dot(xn, w_ref[:, c0 + C_KW:c0 + ZC_WIDTH])


def _proj(h, nrm, w, keep, dils, a_dtype):
    m, d = h.shape
    tm = min(ROW_TILE, m)
    assert m % tm == 0 and keep % tm == 0
    nt, nk = m // tm, keep // tm
    row = lambda wd: pl.BlockSpec((tm, wd), lambda i: (i, 0))
    tail = pl.BlockSpec((tm, A_WIDTH), lambda i: (jnp.maximum(i - (nt - nk), 0), 0))
    sds = jax.ShapeDtypeStruct
    all_d = (1,) + tuple(dils)
    return pl.pallas_call(
        functools.partial(_proj_body, tm=tm, dils=tuple(dils)),
        grid=(nt,),
        in_specs=[row(d), _const_spec((1, d)), _const_spec((d, PROJ_PAD))],
        out_specs=[pl.BlockSpec((dd, tm // dd, 3 * A_WIDTH), lambda i: (0, i, 0)) for dd in all_d]
                  + [tail, tail, row(ZB_WIDTH), row(ZC_WIDTH)],
        out_shape=[sds((dd, m // dd, 3 * A_WIDTH), a_dtype) for dd in all_d]
                  + [sds((keep, A_WIDTH), F32), sds((keep, A_WIDTH), F32),
                     sds((m, ZB_WIDTH), F32), sds((m, ZC_WIDTH), F32)],
        scratch_shapes=[pltpu.VMEM((A_WIDTH // LANES, tm, LANES), F32)] if dils else [],
        compiler_params=_params(1),
        name="proj",
    )(h, nrm, w)


def _alibi_slopes():
    return (2.0 ** (-8.0 * np.arange(1, A_HEADS + 1, dtype=np.float32) / A_HEADS)).astype(np.float32)


def _attn_bias(dil):
    qi = np.arange(A_SUB)[:, None]
    ki = np.arange(2 * A_SUB)[None, :]
    j = qi + A_SUB - ki
    band = (j >= 0) & (j <= A_SUB)
    bias = -_alibi_slopes()[:, None, None] * (j * dil).astype(np.float32)[None]
    return jnp.asarray(np.where(band[None], bias, np.float32(NEG_BIG)).astype(np.float32))


def _attn_body(q_ref, kp_ref, kc_ref, vp_ref, vc_ref, bias_ref, o_ref, lse_ref, kk, vv, *, n_sub):
    qb = A_SUB
    kk[0:A_SUB, :] = kp_ref[...]
    vv[0:A_SUB, :] = vp_ref[...]
    kk[A_SUB:, :] = kc_ref[...]
    vv[A_SUB:, :] = vc_ref[...]
    lane = lax.broadcasted_iota(jnp.int32, (qb, LANES), 1)
    low = lane < A_HEAD_DIM
    low_f = low.astype(F32)
    head_lanes = (low_f.astype(BF16), (1.0 - low_f).astype(BF16))
    first = jnp.where(pl.program_id(1) > 0, 0, A_SUB)
    key_ok = lax.broadcasted_iota(jnp.int32, (qb, 2 * A_SUB), 1) >= first
    for i in range(n_sub):
        rows = slice(i * qb, (i + 1) * qb)
        krows = slice(i * qb, i * qb + 2 * A_SUB)
        lse_tile = jnp.zeros((qb, LANES), F32)
        for hp in range(A_HEADS // 2):
            cols = slice(hp * LANES, (hp + 1) * LANES)
            q2 = q_ref[rows, cols]
            k2 = kk[krows, cols]
            v2 = vv[krows, cols]
            outs = []
            for hh in range(2):
                h = 2 * hp + hh
                s = _dot_nt(q2 * head_lanes[hh], k2) + bias_ref[h]
                if i == 0:
                    s = jnp.where(key_ok, s, NEG_BIG)
                mx = jnp.max(s, axis=-1, keepdims=True)
                p = jnp.exp(s - mx)
                den = jnp.sum(p, axis=-1, keepdims=True)
                outs.append(_dot(p.astype(BF16), v2) / den)
                lse_tile = jnp.where(lane == h, mx + jnp.log(den), lse_tile)
            o_ref[rows, cols] = jnp.where(low, outs[0], outs[1])
        lse_ref[rows, :] = lse_tile


def _attn_prompt(qkv, dil):
    _, tv, _ = qkv.shape
    assert qkv.shape[0] == dil and tv % A_SUB == 0
    bq = min(ATT_ROWS, tv)
    assert tv % bq == 0 and bq % A_SUB == 0
    cur = lambda part: pl.BlockSpec((None, bq, A_WIDTH), lambda r, n: (r, n, part))
    prev = lambda part: pl.BlockSpec((None, A_SUB, A_WIDTH),
                                     lambda r, n: (r, jnp.maximum(n * (bq // A_SUB) - 1, 0), part))
    return pl.pallas_call(
        functools.partial(_attn_body, n_sub=bq // A_SUB),
        grid=(dil, tv // bq),
        in_specs=[cur(0), prev(1), cur(1), prev(2), cur(2), _const_spec((A_HEADS, A_SUB, 2 * A_SUB))],
        out_specs=[cur(0), pl.BlockSpec((None, bq, LANES), lambda r, n: (r, n, 0))],
        out_shape=[jax.ShapeDtypeStruct((dil, tv, A_WIDTH), F32), jax.ShapeDtypeStruct((dil, tv, LANES), F32)],
        scratch_shapes=[pltpu.VMEM((A_SUB + bq, A_WIDTH), BF16), pltpu.VMEM((A_SUB + bq, A_WIDTH), BF16)],
        compiler_params=_params(2),
        name=f"attn_prompt_d{dil}",
    )(qkv, qkv, qkv, qkv, qkv, _attn_bias(dil))


def _decode_tables(n_past):
    dist = n_past - np.arange(n_past)
    mult = np.zeros(n_past, np.float32)
    for window, dil in A_CONFIGS:
        mult += ((dist % dil == 0) & (dist // dil <= window // dil)).astype(np.float32)
    bias = -_alibi_slopes()[:, None] * dist.astype(np.float32)[None, :]
    return jnp.asarray(bias), jnp.asarray(mult[None, :])


def _attn_decode_body(q_ref, kn_ref, vn_ref, kc_ref, vc_ref, bias_ref, mult_ref, o_ref):
    q_t = q_ref[...]
    s_new = jnp.sum(q_t * kn_ref[...], axis=0, keepdims=True)
    mult = mult_ref[...]
    live = mult > 0.0
    n_cfg = float(len(A_CONFIGS))
    lane = lax.broadcasted_iota(jnp.int32, q_t.shape, 1)
    out = jnp.zeros(q_t.shape, F32)
    for h in range(A_HEADS):
        s = jnp.sum(kc_ref[h] * q_t[:, h:h + 1], axis=0, keepdims=True) + bias_ref[h:h + 1, :]
        s = jnp.where(live, s, NEG_BIG)
        s0 = s_new[:, h:h + 1]
        mx = jnp.maximum(jnp.max(s, axis=-1, keepdims=True), s0)
        p = jnp.exp(s - mx) * mult
        p0 = jnp.exp(s0 - mx) * n_cfg
        den = jnp.sum(p, axis=-1, keepdims=True) + p0
        acc = jnp.sum(vc_ref[h] * p, axis=-1, keepdims=True) + p0 * vn_ref[:, h:h + 1]
        out = jnp.where(lane == h, acc / den, out)
    o_ref[...] = out


def _attn_decode(q_t, kn_t, vn_t, k_cache_t, v_cache_t, layer):
    nb = q_t.shape[0]
    n_past = k_cache_t.shape[-1]
    assert n_past == A_MAX_WINDOW
    bias, mult = _decode_tables(n_past)
    new = pl.BlockSpec((None, A_HEAD_DIM, A_HEADS), lambda b: (b, 0, 0))
    past = pl.BlockSpec((None, None, A_HEADS, A_HEAD_DIM, n_past), lambda b: (layer, b, 0, 0, 0))
    return pl.pallas_call(
        _attn_decode_body,
        grid=(nb,),
        in_specs=[new, new, new, past, past, _const_spec(bias.shape), _const_spec(mult.shape)],
        out_specs=new,
        out_shape=jax.ShapeDtypeStruct((nb, A_HEAD_DIM, A_HEADS), F32),
        compiler_params=_params(1),
        name="attn_decode",
    )(q_t, kn_t, vn_t, k_cache_t, v_cache_t, bias, mult)


def _rec_tables(c, single, stacked_heads):
    nl = 0 if single else int(math.log2(c))
    assert single or 2 ** nl == c
    t = np.arange(c)[:, None]
    i = np.arange(c)[None, :]
    e_rows, masks = [], []
    for lvl in range(nl):
        half = c >> (lvl + 1)
        mid = (t // (2 * half)) * (2 * half) + half
        right = t >= mid
        e_rows.append(np.where(right, (i >= mid) & (i <= t), (i > t) & (i < mid)))
        same = (t // (2 * half)) == (i // (2 * half))
        masks.append(same & right & (i < mid))
    masks.append(t == i)
    e_rows.append(i <= t)
    e_rows.append(i > t)
    e = np.concatenate(e_rows, 0).astype(np.float32)
    m = np.stack(masks, 0).astype(np.float32)
    if stacked_heads > 1:
        m = np.tile(m, (1, stacked_heads, 1))
    return jnp.asarray(np.concatenate([e, e], 1), dtype=BF16), jnp.asarray(m)


def _rec_core(q, k, lf, v, e_ref, m_ref, st_ref, o_ref, *, c, kind, single):
    nl = 0 if single else int(math.log2(c))
    n_win = q.shape[1] // LANES
    win = lambda x, w: x[:, w * LANES:(w + 1) * LANES]
    if single:
        e_cum = jnp.exp(lf)
        ke_f = k
    else:
        l1 = lf.astype(BF16)
        l2 = (lf - l1.astype(F32)).astype(BF16)
        decay = jnp.exp(_dot(e_ref[...], jnp.concatenate([l1, l2], axis=0)))
        seg = lambda n: decay[n * c:(n + 1) * c]
        e_cum = seg(nl)
        ke_f = k * seg(nl + 1)
    row = lax.broadcasted_iota(jnp.int32, (c, LANES), 0)

    def level_operand(n, w):
        half = c >> (n + 1)
        qw, kw = win(q, w), win(k, w)
        if half >= 8:
            pieces = [(qw if j % 2 else kw)[j * half:(j + 1) * half] for j in range(c // half)]
            side = jnp.concatenate(pieces, axis=0)
        else:
            side = jnp.where((row & half) != 0, qw, kw)
        return (side * win(seg(n), w)).astype(BF16)

    u = [[level_operand(n, w) for w in range(n_win)] for n in range(nl)]
    uq = [win(q, w).astype(BF16) for w in range(n_win)]
    uk = [win(k, w).astype(BF16) for w in range(n_win)]
    qg = (q * e_cum).astype(BF16)
    ke = ke_f.astype(BF16)
    e_tot = e_cum[0:1, :] if single else e_cum[c - 1:c, :]
    vb = v.astype(BF16)
    lane = lax.broadcasted_iota(jnp.int32, (c, LANES), 1)
    low = lane < B_VAL_DIM
    if kind == "c":
        head = lax.broadcasted_iota(jnp.int32, (1, LANES), 1) // C_KEY_DIM
        hm = [(head == h).astype(F32).astype(BF16) for h in range(C_HEADS)]
        stack = lambda x: jnp.concatenate([x * hm[h] for h in range(C_HEADS)], axis=0)
        att_all = _dot_nt(stack(uq[0]), uk[0]) * m_ref[nl]
        for n in range(nl):
            att_all = att_all + _dot_nt(stack(u[n][0]), u[n][0]) * m_ref[n]
        att_all = att_all.astype(BF16)
        srow = lax.broadcasted_iota(jnp.int32, (LANES, LANES), 0) // B_VAL_DIM
        slane = lax.broadcasted_iota(jnp.int32, (LANES, LANES), 1) // C_KEY_DIM
    for pair in range(2):
        v2 = win(vb, pair)
        st = st_ref[pair]
        stb = st.astype(BF16)
        if kind == "c":
            outs = [_dot(att_all[h * c:(h + 1) * c], v2) for h in (2 * pair, 2 * pair + 1)]
            o_ref[:, pair * LANES:(pair + 1) * LANES] = jnp.where(low, outs[0], outs[1]) + _dot_nt(qg, stb)
            own = (slane == srow + 2 * pair).astype(F32)
            st_ref[pair] = st * e_tot + _dot_tn(v2, ke) * own
            continue
        outs = []
        for hh in range(2):
            h = 2 * pair + hh
            att = _dot_nt(uq[h], uk[h]) * m_ref[nl]
            for n in range(nl):
                att = att + _dot_nt(u[n][h], u[n][h]) * m_ref[n]
            outs.append(_dot(att.astype(BF16), v2) + _dot_nt(win(qg, h), stb))
            rows = slice(hh * B_VAL_DIM, (hh + 1) * B_VAL_DIM)
            upd = _dot_tn(v2, win(ke, h))
            st_ref[pair, rows, :] = st[rows, :] * win(e_tot, h) + upd[rows, :]
        o_ref[:, pair * LANES:(pair + 1) * LANES] = jnp.where(low, outs[0], outs[1])


def _rec_body(*refs, kind, c, n_valid):
    if kind == "b":
        z_ref, lb_ref, e_ref, m_ref, s0_ref, o_ref, s_ref, st_ref = refs
    else:
        z_ref, cwg_ref, cb_ref, e_ref, m_ref, s0_ref, o_ref, s_ref, st_ref = refs
    chunk = pl.program_id(1)

    @pl.when(chunk == 0)
    def _():
        st_ref[...] = s0_ref[...]

    z = z_ref[...]
    if kind == "b":
        lb = lb_ref[...]
        f = lb + (1.0 - lb) * _sigmoid(z[:, B_KW:2 * B_KW])
        lf = jnp.log(jnp.maximum(f, MIN_F))
        k = 1.0 - f
        zq = z[:, 0:B_KW]
        q = zq * _sigmoid(zq)
        v = z[:, 2 * B_KW:2 * B_KW + B_WIDTH]
    else:
        q = z[:, 0:C_KW]
        k = z[:, C_KW:2 * C_KW]
        v = z[:, 2 * C_KW:2 * C_KW + C_WIDTH]
        x = _dot(z[:, ZC_WIDTH - LANES:].astype(BF16), cwg_ref[...]) + cb_ref[...]
        lf = (jnp.minimum(x, 0.0) - jnp.log1p(jnp.exp(-jnp.abs(x)))) / C_GATE_TEMP
    if n_valid < c:
        live = lax.broadcasted_iota(jnp.int32, (c, 1), 0) < n_valid
        q = jnp.where(live, q, 0.0)
        k = jnp.where(live, k, 0.0)
        lf = jnp.where(live, lf, 0.0)
    _rec_core(q, k, lf, v, e_ref, m_ref, st_ref, o_ref, c=c, kind=kind, single=n_valid == 1)

    @pl.when(chunk == pl.num_programs(1) - 1)
    def _():
        s_ref[...] = st_ref[...]


def _recurrence(kind, z, extra, s0, n_valid=None):
    nb, t, width = z.shape
    c = min(REC_CHUNK, t)
    assert t % c == 0
    n_valid = c if n_valid is None else n_valid
    assert n_valid in (1, c)
    e, m = _rec_tables(c, n_valid == 1, C_HEADS if kind == "c" else 1)
    st_spec = pl.BlockSpec((None, 2, LANES, LANES), lambda b, j: (b, 0, 0, 0))
    extra_specs = [_const_spec(x.shape) for x in extra]
    return pl.pallas_call(
        functools.partial(_rec_body, kind=kind, c=c, n_valid=n_valid),
        grid=(nb, t // c),
        in_specs=[pl.BlockSpec((None, c, width), lambda b, j: (b, j, 0))] + extra_specs
                 + [_const_spec(e.shape), _const_spec(m.shape), st_spec],
        out_specs=[pl.BlockSpec((None, c, 2 * LANES), lambda b, j: (b, j, 0)), st_spec],
        out_shape=[jax.ShapeDtypeStruct((nb, t, 2 * LANES), F32), jax.ShapeDtypeStruct((nb, 2, LANES, LANES), F32)],
        scratch_shapes=[pltpu.VMEM((2, LANES, LANES), F32)],
        compiler_params=_params(2),
        name=f"recurrence_{kind}",
    )(z, *extra, e, m, s0)


def _mix_body(*refs, tm, dils):
    h_ref = refs[0]
    n_att = 2 * len(dils) if dils else 1
    att_refs = refs[1:1 + n_att]
    ob_ref, bg_ref, oc_ref, cg_ref, bn_ref, cn_ref, ex_ref, bd_ref, w_ref, out_ref, mixed = refs[1 + n_att:12 + n_att]
    if dils:
        o_refs, l_refs = att_refs[:len(dils)], att_refs[len(dils):]
        o_nat, l_nat = refs[12 + n_att], refs[13 + n_att]
        n_cb = A_WIDTH // LANES
        lses = []
        for c, d in enumerate(dils):
            if d == 1:
                lses.append(l_refs[c][0])
                continue
            for r in range(d):
                l_nat[c, pl.ds(r, tm // d, stride=d), :] = l_refs[c][r]
                for cb in range(n_cb):
                    o_nat[c, cb, pl.ds(r, tm // d, stride=d), :] = o_refs[c][r, :, cb * LANES:(cb + 1) * LANES]
            lses.append(l_nat[c])
        mx = functools.reduce(jnp.maximum, lses)
        es = [jnp.exp(l - mx) for l in lses]
        den = functools.reduce(lambda a, b: a + b, es)
        ex = ex_ref[...]
        wts = [_dot_split2(e / den, ex) for e in es]
        for cb in range(n_cb):
            cs = slice(cb * LANES, (cb + 1) * LANES)
            terms = [wts[c][:, cs] * (o_refs[c][0, :, cs] if d == 1 else o_nat[c, cb]) for c, d in enumerate(dils)]
            mixed[:, cs] = functools.reduce(lambda a, b: a + b, terms).astype(BF16)
    else:
        mixed[:, 0:A_WIDTH] = att_refs[0][...].astype(BF16)
    bd = bd_ref[...]
    ob = ob_ref[...]
    obn = ob * lax.rsqrt(_dot_split2(ob * ob, bd) + RMS_EPS) * bn_ref[...]
    mixed[:, A_WIDTH:A_WIDTH + B_WIDTH] = (obn * _sigmoid(bg_ref[...])).astype(BF16)
    oc = oc_ref[...]
    ocn = oc * lax.rsqrt(_dot_split2(oc * oc, bd) + RMS_EPS) * cn_ref[...]
    cg = cg_ref[...]
    mixed[:, A_WIDTH + B_WIDTH:] = (ocn * (cg * _sigmoid(cg))).astype(BF16)
    out_ref[...] = h_ref[...] + _dot(mixed[...], w_ref[...])


def _mix(h, att, dils, ob, zb, oc, zc, b_norm, c_norm, w_out):
    m, d = h.shape
    tm = min(ROW_TILE, m)
    assert m % tm == 0
    row = lambda w, blk=0: pl.BlockSpec((tm, w), lambda i: (i, blk))
    if dils:
        att_specs = [pl.BlockSpec((dd, tm // dd, wd), lambda i: (0, i, 0))
                     for wd in (A_WIDTH, LANES) for dd in dils]
        att_scratch = [pltpu.VMEM((len(dils), A_WIDTH // LANES, tm, LANES), F32),
                       pltpu.VMEM((len(dils), tm, LANES), F32)]
    else:
        att_specs, att_scratch = [row(A_WIDTH)], []
    expand = np.zeros((LANES, A_WIDTH), np.float32)
    for hd in range(A_HEADS):
        expand[hd, hd * A_HEAD_DIM:(hd + 1) * A_HEAD_DIM] = 1.0
    grp = np.arange(B_WIDTH) // B_VAL_DIM
    blockmean = (grp[:, None] == grp[None, :]).astype(np.float32) / B_VAL_DIM
    d_mix = A_WIDTH + B_WIDTH + C_WIDTH
    return pl.pallas_call(
        functools.partial(_mix_body, tm=tm, dils=tuple(dils)),
        grid=(m // tm,),
        in_specs=[row(d)] + att_specs
                 + [row(B_WIDTH), row(B_WIDTH, (2 * B_KW + B_WIDTH) // B_WIDTH),
                    row(C_WIDTH), row(C_WIDTH, (2 * C_KW + C_WIDTH) // C_WIDTH)]
                 + [_const_spec((1, B_WIDTH)), _const_spec((1, C_WIDTH)), _const_spec((LANES, A_WIDTH)),
                    _const_spec((B_WIDTH, B_WIDTH)), _const_spec((d_mix, d))],
        out_specs=row(d),
        out_shape=jax.ShapeDtypeStruct((m, d), F32),
        scratch_shapes=[pltpu.VMEM((tm, d_mix), BF16)] + att_scratch,
        compiler_params=_params(1),
        name="mix_out",
    )(h, *att, ob, zb, oc, zc, b_norm, c_norm,
      jnp.asarray(expand, dtype=BF16), jnp.asarray(blockmean, dtype=BF16), w_out)


def _prep_w_in(w_in):
    sizes = (A_WIDTH, A_WIDTH, A_WIDTH, B_KW, B_KW, B_WIDTH, B_WIDTH, C_KW, C_KW, C_WIDTH, C_GATE_RANK, C_WIDTH)
    cuts = np.cumsum(sizes)[:-1].tolist()
    parts = jnp.split(w_in, cuts, axis=-1)
    clr = jnp.pad(parts[10], ((0, 0), (0, LANES - C_GATE_RANK)))
    return jnp.concatenate(parts[:10] + [parts[11], clr], axis=-1).astype(BF16)


def _state_to_pairs(s):
    nb, nh, dk, dv = s.shape
    st = jnp.swapaxes(s, 2, 3)
    if dk < LANES:
        st = jnp.stack([jnp.pad(st[:, h], ((0, 0), (0, 0), (h * dk, LANES - (h + 1) * dk))) for h in range(nh)], 1)
    return st.reshape(nb, 2, 2 * dv, LANES)


def _pairs_to_state(st, dk):
    nb = st.shape[0]
    st = st.reshape(nb, 4, B_VAL_DIM, LANES)
    if dk < LANES:
        st = jnp.stack([st[:, h, :, h * dk:(h + 1) * dk] for h in range(4)], 1)
    return jnp.swapaxes(st, 2, 3)


def _trunk(x, p, wts, lb_all, final_norm, caches):
    nb, t, d = x.shape
    decode = caches is not None
    assert (t == 1) if decode else (nb == 1)
    m = nb * t
    h = x.reshape(m, d)
    depth = len(wts)
    keep = m if decode else min(A_MAX_WINDOW, t)
    k_rows, v_rows, sb_out, sc_out = [], [], [], []
    y = None
    for i, w in enumerate(wts):
        h, = _ffn(h, w["ffn1_norm"], w["ffn1_w_gate"], w["ffn1_w_up"], w["ffn1_w_down"])
        if decode:
            qkv, kf, vf, zb, zc = _proj(h, w["mix_norm"], w["w_in"], keep, (), F32)
            pad = lambda a: jnp.pad(a.reshape(nb, 1, -1), ((0, 0), (0, DEC_PAD - 1), (0, 0)))
            cols = lambda a: jnp.swapaxes(a.reshape(nb, A_HEADS, A_HEAD_DIM), 1, 2)
            o_t = _attn_decode(cols(qkv[0, :, 0:A_WIDTH]), cols(kf), cols(vf), caches[0], caches[1], i)
            att, dils = [jnp.swapaxes(o_t, 1, 2).reshape(m, A_WIDTH)], ()
            zb3, zc3 = pad(zb), pad(zc)
            sb0, sc0 = _state_to_pairs(caches[2][i]), _state_to_pairs(caches[3][i])
            n_valid = 1
        else:
            dils = tuple(dil for _, dil in A_CONFIGS)
            assert dils[0] == 1
            *qkvs, kf, vf, zb, zc = _proj(h, w["mix_norm"], w["w_in"], keep, dils[1:], BF16)
            res = [_attn_prompt(a, dil) for a, dil in zip(qkvs, dils)]
            att = [r[0] for r in res] + [r[1] for r in res]
            zb3, zc3 = zb[None], zc[None]
            sb0 = sc0 = jnp.zeros((1, 2, LANES, LANES), F32)
            n_valid = None
        ob, sb = _recurrence("b", zb3, [lb_all[i]], sb0, n_valid)
        oc, sc = _recurrence("c", zc3, [w["c_w_gate"], w["c_gate_bias"]], sc0, n_valid)
        if decode:
            ob, oc = ob[:, 0], oc[:, 0]
        else:
            ob, oc = ob[0], oc[0]
        h = _mix(h, att, dils, ob, zb, oc, zc, w["b_out_norm"], w["c_out_norm"], w["w_out"])
        ple = (p[i].reshape(m, -1), w["ple_norm"], w["ple_w_gate"], w["ple_w_proj"])
        if i == depth - 1:
            h, y = _ffn(h, w["ffn2_norm"], w["ffn2_w_gate"], w["ffn2_w_up"], w["ffn2_w_down"], ple, final_norm)
        else:
            h, = _ffn(h, w["ffn2_norm"], w["ffn2_w_gate"], w["ffn2_w_up"], w["ffn2_w_down"], ple)
        k_rows.append(kf.reshape(nb, keep // nb, A_HEADS, A_HEAD_DIM))
        v_rows.append(vf.reshape(nb, keep // nb, A_HEADS, A_HEAD_DIM))
        sb_out.append(_pairs_to_state(sb, B_KEY_DIM))
        sc_out.append(_pairs_to_state(sc, C_KEY_DIM))
    return (y.reshape(nb, t, d), jnp.stack(k_rows), jnp.stack(v_rows), jnp.stack(sb_out), jnp.stack(sc_out))


def kernel(x_prompt, x_sample, cache_k_a, cache_v_a, state_b, state_c, p_prompt, p_sample, ffn1_norm, ffn1_w_gate, ffn1_w_up, ffn1_w_down, mix_norm, w_in, lb_logits, b_out_norm, c_w_gate, c_gate_bias, c_out_norm, w_out, ffn2_norm, ffn2_w_gate, ffn2_w_up, ffn2_w_down, ple_norm, ple_w_gate, ple_w_proj, final_norm):
    depth = w_in.shape[0]
    row = lambda a: a.reshape(1, -1)
    wts = []
    for i in range(depth):
        cwg = jnp.pad(c_w_gate[i], ((0, LANES - C_GATE_RANK), (0, 0))).astype(BF16)
        wts.append(dict(
            ffn1_norm=row(ffn1_norm[i]), ffn1_w_gate=ffn1_w_gate[i].astype(BF16), ffn1_w_up=ffn1_w_up[i].astype(BF16),
            ffn1_w_down=ffn1_w_down[i].astype(BF16), mix_norm=row(mix_norm[i]), w_in=_prep_w_in(w_in[i]),
            b_out_norm=row(b_out_norm[i]), c_w_gate=cwg, c_gate_bias=row(c_gate_bias[i]), c_out_norm=row(c_out_norm[i]),
            w_out=w_out[i].astype(BF16), ffn2_norm=row(ffn2_norm[i]), ffn2_w_gate=ffn2_w_gate[i].astype(BF16),
            ffn2_w_up=ffn2_w_up[i].astype(BF16), ffn2_w_down=ffn2_w_down[i].astype(BF16), ple_norm=row(ple_norm[i]),
            ple_w_gate=ple_w_gate[i].astype(BF16), ple_w_proj=ple_w_proj[i].astype(BF16)))
    sm = jax.nn.softmax(lb_logits.astype(F32), axis=0)
    lb_all = jnp.maximum(jnp.cumsum(sm, axis=0) - sm[0], 0.0).reshape(depth, 1, B_KW)
    fn = row(final_norm)
    y_p, k_p, v_p, sb_p, sc_p = _trunk(x_prompt, p_prompt, wts, lb_all, fn, None)
    k_cache_t = jnp.transpose(cache_k_a, (0, 1, 3, 4, 2))
    v_cache_t = jnp.transpose(cache_v_a, (0, 1, 3, 4, 2))
    y_s, k_s, v_s, sb_s, sc_s = _trunk(x_sample, p_sample, wts, lb_all, fn, (k_cache_t, v_cache_t, state_b, state_c))
    return (y_p, y_s, k_p, v_p, sb_p, sc_p, k_s, v_s, sb_s, sc_s)
```

```python
import functools
import math

import numpy as np
import jax
import jax.numpy as jnp
from jax import lax
from jax.experimental import pallas as pl
from jax.experimental.pallas import tpu as pltpu

F32 = jnp.float32
BF16 = jnp.bfloat16

RMS_EPS = 1e-6
NEG_BIG = -1e30
MIN_F = 1e-30
A_HEADS = 8
A_HEAD_DIM = 64
A_CONFIGS = ((128, 1), (512, 4), (2048, 16))
A_MAX_WINDOW = 2048
A_SUB = 128
B_HEADS = 4
B_KEY_DIM = 128
B_VAL_DIM = 64
C_HEADS = 4
C_KEY_DIM = 32
C_VAL_DIM = 64
C_GATE_RANK = 16
C_GATE_TEMP = 16.0
A_WIDTH = A_HEADS * A_HEAD_DIM
B_KW = B_HEADS * B_KEY_DIM
B_WIDTH = B_HEADS * B_VAL_DIM
C_KW = C_HEADS * C_KEY_DIM
C_WIDTH = C_HEADS * C_VAL_DIM
ZB_WIDTH = 2 * B_KW + 2 * B_WIDTH
ZC_WIDTH = 2 * C_KW + 2 * C_WIDTH + 128
PROJ_PAD = 3 * A_WIDTH + ZB_WIDTH + ZC_WIDTH

LANES = 128
VMEM_LIMIT_BYTES = 56 * 1024 * 1024

ROW_TILE = 512
FF_CHUNK = 1408
REC_CHUNK = 128
REC_CHUNKS_PER_STEP = 4
REC_SEQS_PER_STEP = 8
ATT_ROWS = 1024
DEC_PAD = 16


def _params(n_axes):
    return pltpu.CompilerParams(dimension_semantics=("arbitrary",) * n_axes,
                                vmem_limit_bytes=VMEM_LIMIT_BYTES)


def _const_spec(shape):
    zeros = (0,) * len(shape)
    return pl.BlockSpec(shape, lambda *_: zeros, pipeline_mode=pl.Buffered(1))


def _dot(a, b):
    return jnp.dot(a, b, preferred_element_type=F32)


def _dot_nt(a, b):
    return lax.dot_general(a, b, (((1,), (1,)), ((), ())), preferred_element_type=F32)


def _dot_tn(a, b):
    return lax.dot_general(a, b, (((0,), (0,)), ((), ())), preferred_element_type=F32)


def _dot_split2(x, w):
    hi = x.astype(BF16)
    lo = (x - hi.astype(F32)).astype(BF16)
    return _dot(hi, w) + _dot(lo, w)


def _rms(x, g):
    ms = jnp.mean(x * x, axis=-1, keepdims=True)
    return x * lax.rsqrt(ms + RMS_EPS) * g


def _sigmoid(x):
    return 1.0 / (1.0 + jnp.exp(-x))


def _ffn_body(*refs, n_chunks, chunk, with_ple, with_final):
    refs = list(refs)
    h_ref, nrm_ref, wg_ref, wu_ref, wd_ref = refs[:5]
    pos = 5
    if with_ple:
        p_ref, pn_ref, pwg_ref, pwp_ref = refs[pos:pos + 4]
        pos += 4
    if with_final:
        fn_ref = refs[pos]
        pos += 1
    o_ref = refs[pos]
    h = h_ref[...]
    xn = _rms(h, nrm_ref[...]).astype(BF16)
    acc = None
    for c in range(n_chunks):
        lo = c * chunk
        g = _dot(xn, wg_ref[:, lo:lo + chunk])
        u = _dot(xn, wu_ref[:, lo:lo + chunk])
        a = (g * _sigmoid(g) * u).astype(BF16)
        d = _dot(a, wd_ref[lo:lo + chunk, :])
        acc = d if acc is None else acc + d
    h = h + 0.5 * acc
    if with_ple:
        hn = _rms(h, pn_ref[...]).astype(BF16)
        gate = _sigmoid(_dot(hn, pwg_ref[...]))
        proj = _dot(p_ref[...].astype(BF16), pwp_ref[...])
        h = h + gate * proj
    o_ref[...] = h
    if with_final:
        refs[pos + 1][...] = _rms(h, fn_ref[...])


def _ffn(h, nrm, wg, wu, wd, ple=None, final_norm=None):
    m, d = h.shape
    dff = wg.shape[1]
    tm = min(ROW_TILE, m)
    chunk = min(FF_CHUNK, dff)
    assert m % tm == 0 and dff % chunk == 0
    row = lambda w: pl.BlockSpec((tm, w), lambda i: (i, 0))
    ins = [h, nrm, wg, wu, wd]
    specs = [row(d), _const_spec((1, d)), _const_spec((d, dff)), _const_spec((d, dff)), _const_spec((dff, d))]
    if ple is not None:
        p, pn, pwg, pwp = ple
        ins += [p, pn, pwg, pwp]
        specs += [row(p.shape[1]), _const_spec((1, d)), _const_spec((d, d)), _const_spec(pwp.shape)]
    if final_norm is not None:
        ins.append(final_norm)
        specs.append(_const_spec((1, d)))
    n_out = 2 if final_norm is not None else 1
    out = pl.pallas_call(
        functools.partial(_ffn_body, n_chunks=dff // chunk, chunk=chunk, with_ple=ple is not None,
                          with_final=final_norm is not None),
        grid=(m // tm,),
        in_specs=specs,
        out_specs=[row(d)] * n_out,
        out_shape=[jax.ShapeDtypeStruct((m, d), F32)] * n_out,
        compiler_params=_params(1),
        name="ffn",
    )(*ins)
    return out


def _proj_body(h_ref, nrm_ref, w_ref, *refs, tm, dils):
    n_a = 1 + len(dils)
    a_refs = refs[:n_a]
    kf_ref, vf_ref, zb_ref, zc_ref = refs[n_a:n_a + 4]
    xn = _rms(h_ref[...], nrm_ref[...]).astype(BF16)
    a = A_WIDTH
    for part in range(3):
        cols = slice(part * a, (part + 1) * a)
        z = _dot(xn, w_ref[:, cols])
        if part == 0:
            z = z * (A_HEAD_DIM ** -0.5)
        elif part == 1:
            kf_ref[...] = z
        else:
            vf_ref[...] = z
        a_refs[0][0, :, cols] = z.astype(a_refs[0].dtype)
        if dils:
            stage = refs[-1]
            for cb in range(a // LANES):
                stage[cb] = z[:, cb * LANES:(cb + 1) * LANES]
        for ref, d in zip(a_refs[1:], dils):
            for r in range(d):
                for cb in range(a // LANES):
                    lo = part * a + cb * LANES
                    ref[r, :, lo:lo + LANES] = stage[cb, pl.ds(r, tm // d, stride=d), :].astype(ref.dtype)
    b0 = 3 * a
    zb_ref[...] = _dot(xn, w_ref[:, b0:b0 + ZB_WIDTH])
    c0 = b0 + ZB_WIDTH
    zc_ref[:, 0:C_KW] = _dot(xn, w_ref[:, c0:c0 + C_KW]) * (C_KEY_DIM ** -0.5)
    zc_ref[:, C_KW:] = _dot(xn, w_ref[:, c0 + C_KW:c0 + ZC_WIDTH])


def _proj(h, nrm, w, keep, dils, a_dtype):
    m, d = h.shape
    tm = min(ROW_TILE, m)
    assert m % tm == 0 and keep % tm == 0
    nt, nk = m // tm, keep // tm
    row = lambda wd: pl.BlockSpec((tm, wd), lambda i: (i, 0))
    tail = pl.BlockSpec((tm, A_WIDTH), lambda i: (jnp.maximum(i - (nt - nk), 0), 0))
    sds = jax.ShapeDtypeStruct
    all_d = (1,) + tuple(dils)
    return pl.pallas_call(
        functools.partial(_proj_body, tm=tm, dils=tuple(dils)),
        grid=(nt,),
        in_specs=[row(d), _const_spec((1, d)), _const_spec((d, PROJ_PAD))],
        out_specs=[pl.BlockSpec((dd, tm // dd, 3 * A_WIDTH), lambda i: (0, i, 0)) for dd in all_d]
                  + [tail, tail, row(ZB_WIDTH), row(ZC_WIDTH)],
        out_shape=[sds((dd, m // dd, 3 * A_WIDTH), a_dtype) for dd in all_d]
                  + [sds((keep, A_WIDTH), F32), sds((keep, A_WIDTH), F32),
                     sds((m, ZB_WIDTH), F32), sds((m, ZC_WIDTH), F32)],
        scratch_shapes=[pltpu.VMEM((A_WIDTH // LANES, tm, LANES), F32)] if dils else [],
        compiler_params=_params(1),
        name="proj",
    )(h, nrm, w)


def _alibi_slopes():
    return (2.0 ** (-8.0 * np.arange(1, A_HEADS + 1, dtype=np.float32) / A_HEADS)).astype(np.float32)


def _attn_bias(dil):
    qi = np.arange(A_SUB)[:, None]
    ki = np.arange(2 * A_SUB)[None, :]
    j = qi + A_SUB - ki
    band = (j >= 0) & (j <= A_SUB)
    bias = -_alibi_slopes()[:, None, None] * (j * dil).astype(np.float32)[None]
    return jnp.asarray(np.where(band[None], bias, np.float32(NEG_BIG)).astype(np.float32))


def _attn_body(q_ref, kp_ref, kc_ref, vp_ref, vc_ref, bias_ref, o_ref, lse_ref, kk, vv, *, n_sub):
    qb = A_SUB
    kk[0:A_SUB, :] = kp_ref[...]
    vv[0:A_SUB, :] = vp_ref[...]
    kk[A_SUB:, :] = kc_ref[...]
    vv[A_SUB:, :] = vc_ref[...]
    lane = lax.broadcasted_iota(jnp.int32, (qb, LANES), 1)
    low = lane < A_HEAD_DIM
    low_f = low.astype(F32)
    head_lanes = (low_f.astype(BF16), (1.0 - low_f).astype(BF16))
    first = jnp.where(pl.program_id(1) > 0, 0, A_SUB)
    key_ok = lax.broadcasted_iota(jnp.int32, (qb, 2 * A_SUB), 1) >= first
    for i in range(n_sub):
        rows = slice(i * qb, (i + 1) * qb)
        krows = slice(i * qb, i * qb + 2 * A_SUB)
        lse_tile = jnp.zeros((qb, LANES), F32)
        for hp in range(A_HEADS // 2):
            cols = slice(hp * LANES, (hp + 1) * LANES)
            q2 = q_ref[rows, cols]
            k2 = kk[krows, cols]
            v2 = vv[krows, cols]
            outs = []
            for hh in range(2):
                h = 2 * hp + hh
                s = _dot_nt(q2 * head_lanes[hh], k2) + bias_ref[h]
                if i == 0:
                    s = jnp.where(key_ok, s, NEG_BIG)
                mx = jnp.max(s, axis=-1, keepdims=True)
                p = jnp.exp(s - mx)
                den = jnp.sum(p, axis=-1, keepdims=True)
                outs.append(_dot(p.astype(BF16), v2) / den)
                lse_tile = jnp.where(lane == h, mx + jnp.log(den), lse_tile)
            o_ref[rows, cols] = jnp.where(low, outs[0], outs[1]).astype(o_ref.dtype)
        lse_ref[rows, :] = lse_tile


def _attn_prompt(qkv, dil):
    _, tv, _ = qkv.shape
    assert qkv.shape[0] == dil and tv % A_SUB == 0
    bq = min(ATT_ROWS, tv)
    assert tv % bq == 0 and bq % A_SUB == 0
    cur = lambda part: pl.BlockSpec((None, bq, A_WIDTH), lambda r, n: (r, n, part))
    prev = lambda part: pl.BlockSpec((None, A_SUB, A_WIDTH),
                                     lambda r, n: (r, jnp.maximum(n * (bq // A_SUB) - 1, 0), part))
    return pl.pallas_call(
        functools.partial(_attn_body, n_sub=bq // A_SUB),
        grid=(dil, tv // bq),
        in_specs=[cur(0), prev(1), cur(1), prev(2), cur(2), _const_spec((A_HEADS, A_SUB, 2 * A_SUB))],
        out_specs=[cur(0), pl.BlockSpec((None, bq, LANES), lambda r, n: (r, n, 0))],
        out_shape=[jax.ShapeDtypeStruct((dil, tv, A_WIDTH), BF16), jax.ShapeDtypeStruct((dil, tv, LANES), F32)],
        scratch_shapes=[pltpu.VMEM((A_SUB + bq, A_WIDTH), BF16), pltpu.VMEM((A_SUB + bq, A_WIDTH), BF16)],
        compiler_params=_params(2),
        name=f"attn_prompt_d{dil}",
    )(qkv, qkv, qkv, qkv, qkv, _attn_bias(dil))


def _decode_tables(n_past):
    dist = n_past - np.arange(n_past)
    mult = np.zeros(n_past, np.float32)
    for window, dil in A_CONFIGS:
        mult += ((dist % dil == 0) & (dist // dil <= window // dil)).astype(np.float32)
    bias = -_alibi_slopes()[:, None] * dist.astype(np.float32)[None, :]
    return jnp.asarray(bias), jnp.asarray(mult[None, :])


def _attn_decode_body(q_ref, kn_ref, vn_ref, kc_ref, vc_ref, bias_ref, mult_ref, o_ref):
    q_t = q_ref[...]
    s_new = jnp.sum(q_t * kn_ref[...], axis=0, keepdims=True)
    mult = mult_ref[...]
    live = mult > 0.0
    n_cfg = float(len(A_CONFIGS))
    lane = lax.broadcasted_iota(jnp.int32, q_t.shape, 1)
    out = jnp.zeros(q_t.shape, F32)
    for h in range(A_HEADS):
        s = jnp.sum(kc_ref[h] * q_t[:, h:h + 1], axis=0, keepdims=True) + bias_ref[h:h + 1, :]
        s = jnp.where(live, s, NEG_BIG)
        s0 = s_new[:, h:h + 1]
        mx = jnp.maximum(jnp.max(s, axis=-1, keepdims=True), s0)
        p = jnp.exp(s - mx) * mult
        p0 = jnp.exp(s0 - mx) * n_cfg
        den = jnp.sum(p, axis=-1, keepdims=True) + p0
        acc = jnp.sum(vc_ref[h] * p, axis=-1, keepdims=True) + p0 * vn_ref[:, h:h + 1]
        out = jnp.where(lane == h, acc / den, out)
    o_ref[...] = out


def _attn_decode(q_t, kn_t, vn_t, k_cache_t, v_cache_t, layer):
    nb = q_t.shape[0]
    n_past = k_cache_t.shape[-1]
    assert n_past == A_MAX_WINDOW
    bias, mult = _decode_tables(n_past)
    new = pl.BlockSpec((None, A_HEAD_DIM, A_HEADS), lambda b: (b, 0, 0))
    past = pl.BlockSpec((None, None, A_HEADS, A_HEAD_DIM, n_past), lambda b: (layer, b, 0, 0, 0))
    return pl.pallas_call(
        _attn_decode_body,
        grid=(nb,),
        in_specs=[new, new, new, past, past, _const_spec(bias.shape), _const_spec(mult.shape)],
        out_specs=new,
        out_shape=jax.ShapeDtypeStruct((nb, A_HEAD_DIM, A_HEADS), F32),
        compiler_params=_params(1),
        name="attn_decode",
    )(q_t, kn_t, vn_t, k_cache_t, v_cache_t, bias, mult)


def _rec_tables(c, single, stacked_heads):
    nl = 0 if single else int(math.log2(c))
    assert single or 2 ** nl == c
    t = np.arange(c)[:, None]
    i = np.arange(c)[None, :]
    e_rows, masks = [], []
    for lvl in range(nl):
        half = c >> (lvl + 1)
        mid = (t // (2 * half)) * (2 * half) + half
        right = t >= mid
        e_rows.append(np.where(right, (i >= mid) & (i <= t), (i > t) & (i < mid)))
        same = (t // (2 * half)) == (i // (2 * half))
        masks.append(same & right & (i < mid))
    masks.append(t == i)
    e_rows.append(i <= t)
    e_rows.append(i > t)
    e = np.concatenate(e_rows, 0).astype(np.float32)
    m = np.stack(masks, 0).astype(np.float32)
    if stacked_heads > 1:
        m = np.tile(m, (1, stacked_heads, 1))
    return jnp.asarray(np.concatenate([e, e], 1), dtype=BF16), jnp.asarray(m)


def _rec_core(q, k, lf, v, e_ref, m_ref, st_ref, o_ref, *, c, kind, single):
    nl = 0 if single else int(math.log2(c))
    n_win = q.shape[1] // LANES
    win = lambda x, w: x[:, w * LANES:(w + 1) * LANES]
    if single:
        e_cum = jnp.exp(lf)
        ke_f = k
    else:
        l1 = lf.astype(BF16)
        l2 = (lf - l1.astype(F32)).astype(BF16)
        decay = jnp.exp(_dot(e_ref[...], jnp.concatenate([l1, l2], axis=0)))
        seg = lambda n: decay[n * c:(n + 1) * c]
        e_cum = seg(nl)
        ke_f = k * seg(nl + 1)
    row = lax.broadcasted_iota(jnp.int32, (c, LANES), 0)

    def level_operand(n, w):
        half = c >> (n + 1)
        qw, kw = win(q, w), win(k, w)
        if half >= 8:
            pieces = [(qw if j % 2 else kw)[j * half:(j + 1) * half] for j in range(c // half)]
            side = jnp.concatenate(pieces, axis=0)
        else:
            side = jnp.where((row & half) != 0, qw, kw)
        return (side * win(seg(n), w)).astype(BF16)

    u = [[level_operand(n, w) for w in range(n_win)] for n in range(nl)]
    uq = [win(q, w).astype(BF16) for w in range(n_win)]
    uk = [win(k, w).astype(BF16) for w in range(n_win)]
    qg = (q * e_cum).astype(BF16)
    ke = ke_f.astype(BF16)
    e_tot = e_cum[0:1, :] if single else e_cum[c - 1:c, :]
    vb = v.astype(BF16)
    lane = lax.broadcasted_iota(jnp.int32, (c, LANES), 1)
    low = lane < B_VAL_DIM
    if kind == "c":
        head = lax.broadcasted_iota(jnp.int32, (1, LANES), 1) // C_KEY_DIM
        hm = [(head == h).astype(F32).astype(BF16) for h in range(C_HEADS)]
        stack = lambda x: jnp.concatenate([x * hm[h] for h in range(C_HEADS)], axis=0)
        att_all = _dot_nt(stack(uq[0]), uk[0]) * m_ref[nl]
        for n in range(nl):
            att_all = att_all + _dot_nt(stack(u[n][0]), u[n][0]) * m_ref[n]
        att_all = att_all.astype(BF16)
        srow = lax.broadcasted_iota(jnp.int32, (LANES, LANES), 0) // B_VAL_DIM
        slane = lax.broadcasted_iota(jnp.int32, (LANES, LANES), 1) // C_KEY_DIM
    for pair in range(2):
        v2 = win(vb, pair)
        st = st_ref[pair]
        stb = st.astype(BF16)
        if kind == "c":
            outs = [_dot(att_all[h * c:(h + 1) * c], v2) for h in (2 * pair, 2 * pair + 1)]
            o_ref[:, pair * LANES:(pair + 1) * LANES] = jnp.where(low, outs[0], outs[1]) + _dot_nt(qg, stb)
            own = (slane == srow + 2 * pair).astype(F32)
            st_ref[pair] = st * e_tot + _dot_tn(v2, ke) * own
            continue
        outs = []
        for hh in range(2):
            h = 2 * pair + hh
            att = _dot_nt(uq[h], uk[h]) * m_ref[nl]
            for n in range(nl):
                att = att + _dot_nt(u[n][h], u[n][h]) * m_ref[n]
            outs.append(_dot(att.astype(BF16), v2) + _dot_nt(win(qg, h), stb))
            rows = slice(hh * B_VAL_DIM, (hh + 1) * B_VAL_DIM)
            upd = _dot_tn(v2, win(ke, h))
            st_ref[pair, rows, :] = st[rows, :] * win(e_tot, h) + upd[rows, :]
        o_ref[:, pair * LANES:(pair + 1) * LANES] = jnp.where(low, outs[0], outs[1])


def _rec_body(*refs, kind, c, n_valid, n_b, n_sub):
    if kind == "b":
        z_ref, lb_ref, e_ref, m_ref, s0_ref, o_ref, s_ref, st_ref = refs
    else:
        z_ref, cwg_ref, cb_ref, e_ref, m_ref, s0_ref, o_ref, s_ref, st_ref = refs

    @pl.when(pl.program_id(1) == 0)
    def _():
        st_ref[...] = s0_ref[...]

    for bb in range(n_b):
        for j in range(n_sub):
            z = z_ref[bb, j * c:(j + 1) * c, :]
            _rec_chunk(z, refs, st_ref.at[bb], o_ref.at[bb, pl.ds(j * c, c)], kind=kind, c=c, n_valid=n_valid)

    @pl.when(pl.program_id(1) == pl.num_programs(1) - 1)
    def _():
        s_ref[...] = st_ref[...]


def _rec_chunk(z, refs, st_ref, o_ref, *, kind, c, n_valid):
    if kind == "b":
        _, lb_ref, e_ref, m_ref = refs[:4]
    else:
        _, cwg_ref, cb_ref, e_ref, m_ref = refs[:5]
    if kind == "b":
        lb = lb_ref[...]
        f = lb + (1.0 - lb) * _sigmoid(z[:, B_KW:2 * B_KW])
        lf = jnp.log(jnp.maximum(f, MIN_F))
        k = 1.0 - f
        zq = z[:, 0:B_KW]
        q = zq * _sigmoid(zq)
        v = z[:, 2 * B_KW:2 * B_KW + B_WIDTH]
    else:
        q = z[:, 0:C_KW]
        k = z[:, C_KW:2 * C_KW]
        v = z[:, 2 * C_KW:2 * C_KW + C_WIDTH]
        x = _dot(z[:, ZC_WIDTH - LANES:].astype(BF16), cwg_ref[...]) + cb_ref[...]
        lf = (jnp.minimum(x, 0.0) - jnp.log1p(jnp.exp(-jnp.abs(x)))) / C_GATE_TEMP
    if n_valid < c:
        live = lax.broadcasted_iota(jnp.int32, (c, 1), 0) < n_valid
        q = jnp.where(live, q, 0.0)
        k = jnp.where(live, k, 0.0)
        lf = jnp.where(live, lf, 0.0)
    _rec_core(q, k, lf, v, e_ref, m_ref, st_ref, o_ref, c=c, kind=kind, single=n_valid == 1)


def _recurrence(kind, z, extra, s0, n_valid=None):
    nb, t, width = z.shape
    c = min(REC_CHUNK, t)
    assert t % c == 0
    n_valid = c if n_valid is None else n_valid
    assert n_valid in (1, c)
    n_sub = math.gcd(t // c, REC_CHUNKS_PER_STEP)
    n_b = math.gcd(nb, REC_SEQS_PER_STEP)
    e, m = _rec_tables(c, n_valid == 1, C_HEADS if kind == "c" else 1)
    st_spec = pl.BlockSpec((n_b, 2, LANES, LANES), lambda b, j: (b, 0, 0, 0))
    extra_specs = [_const_spec(x.shape) for x in extra]
    return pl.pallas_call(
        functools.partial(_rec_body, kind=kind, c=c, n_valid=n_valid, n_b=n_b, n_sub=n_sub),
        grid=(nb // n_b, t // (c * n_sub)),
        in_specs=[pl.BlockSpec((n_b, n_sub * c, width), lambda b, j: (b, j, 0))] + extra_specs
                 + [_const_spec(e.shape), _const_spec(m.shape), st_spec],
        out_specs=[pl.BlockSpec((n_b, n_sub * c, 2 * LANES), lambda b, j: (b, j, 0)), st_spec],
        out_shape=[jax.ShapeDtypeStruct((nb, t, 2 * LANES), F32), jax.ShapeDtypeStruct((nb, 2, LANES, LANES), F32)],
        scratch_shapes=[pltpu.VMEM((n_b, 2, LANES, LANES), F32)],
        compiler_params=_params(2),
        name=f"recurrence_{kind}",
    )(z, *extra, e, m, s0)


def _mix_body(*refs, tm, dils):
    h_ref = refs[0]
    n_att = 2 * len(dils) if dils else 1
    att_refs = refs[1:1 + n_att]
    ob_ref, bg_ref, oc_ref, cg_ref, bn_ref, cn_ref, ex_ref, bd_ref, w_ref, out_ref, mixed = refs[1 + n_att:12 + n_att]
    if dils:
        o_refs, l_refs = att_refs[:len(dils)], att_refs[len(dils):]
        o_nat, l_nat = refs[12 + n_att], refs[13 + n_att]
        n_cb = A_WIDTH // LANES
        lses = []
        for c, d in enumerate(dils):
            if d == 1:
                lses.append(l_refs[c][0])
                continue
            for r in range(d):
                l_nat[c, pl.ds(r, tm // d, stride=d), :] = l_refs[c][r]
                for cb in range(n_cb):
                    o_nat[c, cb, pl.ds(r, tm // d, stride=d), :] = (
                        o_refs[c][r, :, cb * LANES:(cb + 1) * LANES].astype(F32))
            lses.append(l_nat[c])
        mx = functools.reduce(jnp.maximum, lses)
        es = [jnp.exp(l - mx) for l in lses]
        den = functools.reduce(lambda a, b: a + b, es)
        ex = ex_ref[...]
        wts = [_dot_split2(e / den, ex) for e in es]
        for cb in range(n_cb):
            cs = slice(cb * LANES, (cb + 1) * LANES)
            terms = [wts[c][:, cs] * (o_refs[c][0, :, cs] if d == 1 else o_nat[c, cb]) for c, d in enumerate(dils)]
            mixed[:, cs] = functools.reduce(lambda a, b: a + b, terms).astype(BF16)
    else:
        mixed[:, 0:A_WIDTH] = att_refs[0][...].astype(BF16)
    bd = bd_ref[...]
    ob = ob_ref[...]
    obn = ob * lax.rsqrt(_dot_split2(ob * ob, bd) + RMS_EPS) * bn_ref[...]
    mixed[:, A_WIDTH:A_WIDTH + B_WIDTH] = (obn * _sigmoid(bg_ref[...])).astype(BF16)
    oc = oc_ref[...]
    ocn = oc * lax.rsqrt(_dot_split2(oc * oc, bd) + RMS_EPS) * cn_ref[...]
    cg = cg_ref[...]
    mixed[:, A_WIDTH + B_WIDTH:] = (ocn * (cg * _sigmoid(cg))).astype(BF16)
    out_ref[...] = h_ref[...] + _dot(mixed[...], w_ref[...])


def _mix(h, att, dils, ob, zb, oc, zc, b_norm, c_norm, w_out):
    m, d = h.shape
    tm = min(ROW_TILE, m)
    assert m % tm == 0
    row = lambda w, blk=0: pl.BlockSpec((tm, w), lambda i: (i, blk))
    if dils:
        att_specs = [pl.BlockSpec((dd, tm // dd, wd), lambda i: (0, i, 0))
                     for wd in (A_WIDTH, LANES) for dd in dils]
        att_scratch = [pltpu.VMEM((len(dils), A_WIDTH // LANES, tm, LANES), F32),
                       pltpu.VMEM((len(dils), tm, LANES), F32)]
    else:
        att_specs, att_scratch = [row(A_WIDTH)], []
    expand = np.zeros((LANES, A_WIDTH), np.float32)
    for hd in range(A_HEADS):
        expand[hd, hd * A_HEAD_DIM:(hd + 1) * A_HEAD_DIM] = 1.0
    grp = np.arange(B_WIDTH) // B_VAL_DIM
    blockmean = (grp[:, None] == grp[None, :]).astype(np.float32) / B_VAL_DIM
    d_mix = A_WIDTH + B_WIDTH + C_WIDTH
    return pl.pallas_call(
        functools.partial(_mix_body, tm=tm, dils=tuple(dils)),
        grid=(m // tm,),
        in_specs=[row(d)] + att_specs
                 + [row(B_WIDTH), row(B_WIDTH, (2 * B_KW + B_WIDTH) // B_WIDTH),
                    row(C_WIDTH), row(C_WIDTH, (2 * C_KW + C_WIDTH) // C_WIDTH)]
                 + [_const_spec((1, B_WIDTH)), _const_spec((1, C_WIDTH)), _const_spec((LANES, A_WIDTH)),
                    _const_spec((B_WIDTH, B_WIDTH)), _const_spec((d_mix, d))],
        out_specs=row(d),
        out_shape=jax.ShapeDtypeStruct((m, d), F32),
        scratch_shapes=[pltpu.VMEM((tm, d_mix), BF16)] + att_scratch,
        compiler_params=_params(1),
        name="mix_out",
    )(h, *att, ob, zb, oc, zc, b_norm, c_norm,
      jnp.asarray(expand, dtype=BF16), jnp.asarray(blockmean, dtype=BF16), w_out)


def _prep_w_in(w_in):
    sizes = (A_WIDTH, A_WIDTH, A_WIDTH, B_KW, B_KW, B_WIDTH, B_WIDTH, C_KW, C_KW, C_WIDTH, C_GATE_RANK, C_WIDTH)
    cuts = np.cumsum(sizes)[:-1].tolist()
    parts = jnp.split(w_in, cuts, axis=-1)
    clr = jnp.pad(parts[10], ((0, 0), (0, LANES - C_GATE_RANK)))
    return jnp.concatenate(parts[:10] + [parts[11], clr], axis=-1).astype(BF16)


def _state_to_pairs(s):
    nb, nh, dk, dv = s.shape
    st = jnp.swapaxes(s, 2, 3)
    if dk < LANES:
        st = jnp.stack([jnp.pad(st[:, h], ((0, 0), (0, 0), (h * dk, LANES - (h + 1) * dk))) for h in range(nh)], 1)
    return st.reshape(nb, 2, 2 * dv, LANES)


def _pairs_to_state(st, dk):
    nb = st.shape[0]
    st = st.reshape(nb, 4, B_VAL_DIM, LANES)
    if dk < LANES:
        st = jnp.stack([st[:, h, :, h * dk:(h + 1) * dk] for h in range(4)], 1)
    return jnp.swapaxes(st, 2, 3)


def _trunk(x, p, wts, lb_all, final_norm, caches):
    nb, t, d = x.shape
    decode = caches is not None
    assert (t == 1) if decode else (nb == 1)
    m = nb * t
    h = x.reshape(m, d)
    depth = len(wts)
    keep = m if decode else min(A_MAX_WINDOW, t)
    k_rows, v_rows, sb_out, sc_out = [], [], [], []
    y = None
    for i, w in enumerate(wts):
        h, = _ffn(h, w["ffn1_norm"], w["ffn1_w_gate"], w["ffn1_w_up"], w["ffn1_w_down"])
        if decode:
            qkv, kf, vf, zb, zc = _proj(h, w["mix_norm"], w["w_in"], keep, (), F32)
            pad = lambda a: jnp.pad(a.reshape(nb, 1, -1), ((0, 0), (0, DEC_PAD - 1), (0, 0)))
            cols = lambda a: jnp.swapaxes(a.reshape(nb, A_HEADS, A_HEAD_DIM), 1, 2)
            o_t = _attn_decode(cols(qkv[0, :, 0:A_WIDTH]), cols(kf), cols(vf), caches[0], caches[1], i)
            att, dils = [jnp.swapaxes(o_t, 1, 2).reshape(m, A_WIDTH)], ()
            zb3, zc3 = pad(zb), pad(zc)
            sb0, sc0 = _state_to_pairs(caches[2][i]), _state_to_pairs(caches[3][i])
            n_valid = 1
        else:
            dils = tuple(dil for _, dil in A_CONFIGS)
            assert dils[0] == 1
            *qkvs, kf, vf, zb, zc = _proj(h, w["mix_norm"], w["w_in"], keep, dils[1:], BF16)
            res = [_attn_prompt(a, dil) for a, dil in zip(qkvs, dils)]
            att = [r[0] for r in res] + [r[1] for r in res]
            zb3, zc3 = zb[None], zc[None]
            sb0 = sc0 = jnp.zeros((1, 2, LANES, LANES), F32)
            n_valid = None
        ob, sb = _recurrence("b", zb3, [lb_all[i]], sb0, n_valid)
        oc, sc = _recurrence("c", zc3, [w["c_w_gate"], w["c_gate_bias"]], sc0, n_valid)
        if decode:
            ob, oc = ob[:, 0], oc[:, 0]
        else:
            ob, oc = ob[0], oc[0]
        h = _mix(h, att, dils, ob, zb, oc, zc, w["b_out_norm"], w["c_out_norm"], w["w_out"])
        ple = (p[i].reshape(m, -1), w["ple_norm"], w["ple_w_gate"], w["ple_w_proj"])
        if i == depth - 1:
            h, y = _ffn(h, w["ffn2_norm"], w["ffn2_w_gate"], w["ffn2_w_up"], w["ffn2_w_down"], ple, final_norm)
        else:
            h, = _ffn(h, w["ffn2_norm"], w["ffn2_w_gate"], w["ffn2_w_up"], w["ffn2_w_down"], ple)
        k_rows.append(kf.reshape(nb, keep // nb, A_HEADS, A_HEAD_DIM))
        v_rows.append(vf.reshape(nb, keep // nb, A_HEADS, A_HEAD_DIM))
        sb_out.append(_pairs_to_state(sb, B_KEY_DIM))
        sc_out.append(_pairs_to_state(sc, C_KEY_DIM))
    return (y.reshape(nb, t, d), jnp.stack(k_rows), jnp.stack(v_rows), jnp.stack(sb_out), jnp.stack(sc_out))


def kernel(x_prompt, x_sample, cache_k_a, cache_v_a, state_b, state_c, p_prompt, p_sample, ffn1_norm, ffn1_w_gate, ffn1_w_up, ffn1_w_down, mix_norm, w_in, lb_logits, b_out_norm, c_w_gate, c_gate_bias, c_out_norm, w_out, ffn2_norm, ffn2_w_gate, ffn2_w_up, ffn2_w_down, ple_norm, ple_w_gate, ple_w_proj, final_norm):
    depth = w_in.shape[0]
    row = lambda a: a.reshape(1, -1)
    wts = []
    bf = {name: a.astype(BF16) for name, a in dict(
        ffn1_w_gate=ffn1_w_gate, ffn1_w_up=ffn1_w_up, ffn1_w_down=ffn1_w_down, w_out=w_out, ffn2_w_gate=ffn2_w_gate,
        ffn2_w_up=ffn2_w_up, ffn2_w_down=ffn2_w_down, ple_w_gate=ple_w_gate, ple_w_proj=ple_w_proj).items()}
    for i in range(depth):
        cwg = jnp.pad(c_w_gate[i], ((0, LANES - C_GATE_RANK), (0, 0))).astype(BF16)
        wts.append(dict(
            ffn1_norm=row(ffn1_norm[i]), mix_norm=row(mix_norm[i]), w_in=_prep_w_in(w_in[i]),
            b_out_norm=row(b_out_norm[i]), c_w_gate=cwg, c_gate_bias=row(c_gate_bias[i]), c_out_norm=row(c_out_norm[i]),
            ffn2_norm=row(ffn2_norm[i]), ple_norm=row(ple_norm[i]), **{name: a[i] for name, a in bf.items()}))
    sm = jax.nn.softmax(lb_logits.astype(F32), axis=0)
    lb_all = jnp.maximum(jnp.cumsum(sm, axis=0) - sm[0], 0.0).reshape(depth, 1, B_KW)
    fn = row(final_norm)
    y_p, k_p, v_p, sb_p, sc_p = _trunk(x_prompt, p_prompt, wts, lb_all, fn, None)
    k_cache_t = jnp.transpose(cache_k_a, (0, 1, 3, 4, 2))
    v_cache_t = jnp.transpose(cache_v_a, (0, 1, 3, 4, 2))
    y_s, k_s, v_s, sb_s, sc_s = _trunk(x_sample, p_sample, wts, lb_all, fn, (k_cache_t, v_cache_t, state_b, state_c))
    return (y_p, y_s, k_p, v_p, sb_p, sc_p, k_s, v_s, sb_s, sc_s)
```

```python
import functools
import math

import numpy as np
import jax
import jax.numpy as jnp
from jax import lax
from jax.experimental import pallas as pl
from jax.experimental.pallas import tpu as pltpu

F32 = jnp.float32
BF16 = jnp.bfloat16

RMS_EPS = 1e-6
NEG_BIG = -1e30
MIN_F = 1e-30
A_HEADS = 8
A_HEAD_DIM = 64
A_CONFIGS = ((128, 1), (512, 4), (2048, 16))
A_MAX_WINDOW = 2048
A_SUB = 128
B_HEADS = 4
B_KEY_DIM = 128
B_VAL_DIM = 64
C_HEADS = 4
C_KEY_DIM = 32
C_VAL_DIM = 64
C_GATE_RANK = 16
C_GATE_TEMP = 16.0
A_WIDTH = A_HEADS * A_HEAD_DIM
B_KW = B_HEADS * B_KEY_DIM
B_WIDTH = B_HEADS * B_VAL_DIM
C_KW = C_HEADS * C_KEY_DIM
C_WIDTH = C_HEADS * C_VAL_DIM
ZB_WIDTH = 2 * B_KW + 2 * B_WIDTH
ZC_WIDTH = 2 * C_KW + 2 * C_WIDTH + 128
PROJ_PAD = 3 * A_WIDTH + ZB_WIDTH + ZC_WIDTH

LANES = 128
MXU_WIDTH = 256
VMEM_LIMIT_BYTES = 56 * 1024 * 1024

ROW_TILE = 512
FF_CHUNK = 1536
REC_CHUNK = 128
REC_CHUNKS_PER_STEP = 4
REC_SEQS_PER_STEP = 8
ATT_ROWS = 1024
DEC_PAD = 16


def _params(n_axes):
    return pltpu.CompilerParams(dimension_semantics=("arbitrary",) * n_axes,
                                vmem_limit_bytes=VMEM_LIMIT_BYTES)


def _const_spec(shape):
    zeros = (0,) * len(shape)
    return pl.BlockSpec(shape, lambda *_: zeros, pipeline_mode=pl.Buffered(1))


def _layer_spec(stacked, layer):
    shape = tuple(stacked.shape[1:])
    zeros = (0,) * len(shape)
    return pl.BlockSpec((None,) + shape, lambda *_: (layer,) + zeros, pipeline_mode=pl.Buffered(1))


def _dot(a, b):
    return jnp.dot(a, b, preferred_element_type=F32)


def _dot_nt(a, b):
    return lax.dot_general(a, b, (((1,), (1,)), ((), ())), preferred_element_type=F32)


def _dot_tn(a, b):
    return lax.dot_general(a, b, (((0,), (0,)), ((), ())), preferred_element_type=F32)


def _dot_split2(x, w):
    hi = x.astype(BF16)
    lo = (x - hi.astype(F32)).astype(BF16)
    return _dot(hi, w) + _dot(lo, w)


def _rms(x, g):
    ms = jnp.mean(x * x, axis=-1, keepdims=True)
    return x * lax.rsqrt(ms + RMS_EPS) * g


def _sigmoid(x):
    return 1.0 / (1.0 + jnp.exp(-x))


def _ff_chunks(dff):
    unit = MXU_WIDTH if dff % MXU_WIDTH == 0 else LANES
    assert dff % unit == 0
    tiles = dff // unit
    n = -(-dff // FF_CHUNK)
    sizes = [(tiles // n + (1 if i < tiles % n else 0)) * unit for i in range(n)]
    starts = np.cumsum([0] + sizes[:-1]).tolist()
    return tuple(zip(starts, sizes))


def _ffn_body(*refs, chunks, with_ple, with_final):
    refs = list(refs)
    h_ref, nrm_ref, wg_ref, wu_ref, wd_ref = refs[:5]
    pos = 5
    if with_ple:
        p_ref, pn_ref, pwg_ref, pwp_ref = refs[pos:pos + 4]
        pos += 4
    if with_final:
        fn_ref = refs[pos]
        pos += 1
    o_ref = refs[pos]
    h = h_ref[...]
    xn = _rms(h, nrm_ref[...]).astype(BF16)
    acc = None
    for lo, chunk in chunks:
        g = _dot(xn, wg_ref[:, lo:lo + chunk])
        u = _dot(xn, wu_ref[:, lo:lo + chunk])
        a = (g * _sigmoid(g) * u).astype(BF16)
        d = _dot(a, wd_ref[lo:lo + chunk, :])
        acc = d if acc is None else acc + d
    h = h + 0.5 * acc
    if with_ple:
        hn = _rms(h, pn_ref[...]).astype(BF16)
        gate = _sigmoid(_dot(hn, pwg_ref[...]))
        proj = _dot(p_ref[...].astype(BF16), pwp_ref[...])
        h = h + gate * proj
    o_ref[...] = h
    if with_final:
        refs[pos + 1][...] = _rms(h, fn_ref[...])


def _ffn(h, layer, nrm, wg, wu, wd, ple=None, final_norm=None):
    m, d = h.shape
    dff = wg.shape[-1]
    tm = min(ROW_TILE, m)
    assert m % tm == 0
    row = lambda w: pl.BlockSpec((tm, w), lambda i: (i, 0))
    ins = [h, nrm, wg, wu, wd]
    specs = [row(d)] + [_layer_spec(a, layer) for a in ins[1:]]
    if ple is not None:
        ins += list(ple)
        specs += [pl.BlockSpec((None, tm, ple[0].shape[-1]), lambda i: (layer, i, 0))]
        specs += [_layer_spec(a, layer) for a in ple[1:]]
    if final_norm is not None:
        ins.append(final_norm)
        specs.append(_const_spec((1, d)))
    n_out = 2 if final_norm is not None else 1
    out = pl.pallas_call(
        functools.partial(_ffn_body, chunks=_ff_chunks(dff), with_ple=ple is not None,
                          with_final=final_norm is not None),
        grid=(m // tm,),
        in_specs=specs,
        out_specs=[row(d)] * n_out,
        out_shape=[jax.ShapeDtypeStruct((m, d), F32)] * n_out,
        compiler_params=_params(1),
        name="ffn",
    )(*ins)
    return out


def _proj_body(h_ref, nrm_ref, w_ref, *refs, tm, dils):
    n_a = 1 + len(dils)
    a_refs = refs[:n_a]
    kf_ref, vf_ref, zb_ref, zc_ref = refs[n_a:n_a + 4]
    xn = _rms(h_ref[...], nrm_ref[...]).astype(BF16)
    a = A_WIDTH
    for part in range(3):
        cols = slice(part * a, (part + 1) * a)
        z = _dot(xn, w_ref[:, cols])
        if part == 0:
            z = z * (A_HEAD_DIM ** -0.5)
        elif part == 1:
            kf_ref[...] = z
        else:
            vf_ref[...] = z
        a_refs[0][0, :, cols] = z.astype(a_refs[0].dtype)
        if dils:
            stage = refs[-1]
            for cb in range(a // LANES):
                stage[cb] = z[:, cb * LANES:(cb + 1) * LANES]
        for ref, d in zip(a_refs[1:], dils):
            for r in range(d):
                for cb in range(a // LANES):
                    lo = part * a + cb * LANES
                    ref[r, :, lo:lo + LANES] = stage[cb, pl.ds(r, tm // d, stride=d), :].astype(ref.dtype)
    b0 = 3 * a
    zb_ref[...] = _dot(xn, w_ref[:, b0:b0 + ZB_WIDTH])
    c0 = b0 + ZB_WIDTH
    zc_ref[:, 0:C_KW] = _dot(xn, w_ref[:, c0:c0 + C_KW]) * (C_KEY_DIM ** -0.5)
    zc_ref[:, C_KW:] = _dot(xn, w_ref[:, c0 + C_KW:c0 + ZC_WIDTH])


def _proj(h, layer, nrm, w, keep, dils, a_dtype):
    m, d = h.shape
    tm = min(ROW_TILE, m)
    assert m % tm == 0 and keep % tm == 0
    nt, nk = m // tm, keep // tm
    row = lambda wd: pl.BlockSpec((tm, wd), lambda i: (i, 0))
    tail = pl.BlockSpec((tm, A_WIDTH), lambda i: (jnp.maximum(i - (nt - nk), 0), 0))
    sds = jax.ShapeDtypeStruct
    all_d = (1,) + tuple(dils)
    return pl.pallas_call(
        functools.partial(_proj_body, tm=tm, dils=tuple(dils)),
        grid=(nt,),
        in_specs=[row(d), _layer_spec(nrm, layer), _layer_spec(w, layer)],
        out_specs=[pl.BlockSpec((dd, tm // dd, 3 * A_WIDTH), lambda i: (0, i, 0)) for dd in all_d]
                  + [tail, tail, row(ZB_WIDTH), row(ZC_WIDTH)],
        out_shape=[sds((dd, m // dd, 3 * A_WIDTH), a_dtype) for dd in all_d]
                  + [sds((keep, A_WIDTH), F32), sds((keep, A_WIDTH), F32),
                     sds((m, ZB_WIDTH), F32), sds((m, ZC_WIDTH), F32)],
        scratch_shapes=[pltpu.VMEM((A_WIDTH // LANES, tm, LANES), F32)] if dils else [],
        compiler_params=_params(1),
        name="proj",
    )(h, nrm, w)


def _alibi_slopes():
    return (2.0 ** (-8.0 * np.arange(1, A_HEADS + 1, dtype=np.float32) / A_HEADS)).astype(np.float32)


def _attn_bias(dil):
    qi = np.arange(A_SUB)[:, None]
    ki = np.arange(2 * A_SUB)[None, :]
    j = qi + A_SUB - ki
    band = (j >= 0) & (j <= A_SUB)
    bias = -_alibi_slopes()[:, None, None] * (j * dil).astype(np.float32)[None]
    return jnp.asarray(np.where(band[None], bias, np.float32(NEG_BIG)).astype(np.float32))


def _attn_body(q_ref, kp_ref, kc_ref, vp_ref, vc_ref, bias_ref, o_ref, lse_ref, kk, vv, *, n_sub):
    qb = A_SUB
    kk[0:A_SUB, :] = kp_ref[...]
    vv[0:A_SUB, :] = vp_ref[...]
    kk[A_SUB:, :] = kc_ref[...]
    vv[A_SUB:, :] = vc_ref[...]
    lane = lax.broadcasted_iota(jnp.int32, (qb, LANES), 1)
    low = lane < A_HEAD_DIM
    low_f = low.astype(F32)
    head_lanes = (low_f.astype(BF16), (1.0 - low_f).astype(BF16))
    first = jnp.where(pl.program_id(1) > 0, 0, A_SUB)
    key_ok = lax.broadcasted_iota(jnp.int32, (qb, 2 * A_SUB), 1) >= first
    for i in range(n_sub):
        rows = slice(i * qb, (i + 1) * qb)
        krows = slice(i * qb, i * qb + 2 * A_SUB)
        lse_tile = jnp.zeros((qb, LANES), F32)
        for hp in range(A_HEADS // 2):
            cols = slice(hp * LANES, (hp + 1) * LANES)
            q2 = q_ref[rows, cols]
            k2 = kk[krows, cols]
            v2 = vv[krows, cols]
            outs = []
            for hh in range(2):
                h = 2 * hp + hh
                s = _dot_nt(q2 * head_lanes[hh], k2) + bias_ref[h]
                if i == 0:
                    s = jnp.where(key_ok, s, NEG_BIG)
                mx = jnp.max(s, axis=-1, keepdims=True)
                p = jnp.exp(s - mx)
                den = jnp.sum(p, axis=-1, keepdims=True)
                outs.append(_dot(p.astype(BF16), v2) / den)
                lse_tile = jnp.where(lane == h, mx + jnp.log(den), lse_tile)
            o_ref[rows, cols] = jnp.where(low, outs[0], outs[1]).astype(o_ref.dtype)
        lse_ref[rows, :] = lse_tile


def _attn_prompt(qkv, dil):
    _, tv, _ = qkv.shape
    assert qkv.shape[0] == dil and tv % A_SUB == 0
    bq = min(ATT_ROWS, tv)
    assert tv % bq == 0 and bq % A_SUB == 0
    cur = lambda part: pl.BlockSpec((None, bq, A_WIDTH), lambda r, n: (r, n, part))
    prev = lambda part: pl.BlockSpec((None, A_SUB, A_WIDTH),
                                     lambda r, n: (r, jnp.maximum(n * (bq // A_SUB) - 1, 0), part))
    return pl.pallas_call(
        functools.partial(_attn_body, n_sub=bq // A_SUB),
        grid=(dil, tv // bq),
        in_specs=[cur(0), prev(1), cur(1), prev(2), cur(2), _const_spec((A_HEADS, A_SUB, 2 * A_SUB))],
        out_specs=[cur(0), pl.BlockSpec((None, bq, LANES), lambda r, n: (r, n, 0))],
        out_shape=[jax.ShapeDtypeStruct((dil, tv, A_WIDTH), BF16), jax.ShapeDtypeStruct((dil, tv, LANES), F32)],
        scratch_shapes=[pltpu.VMEM((A_SUB + bq, A_WIDTH), BF16), pltpu.VMEM((A_SUB + bq, A_WIDTH), BF16)],
        compiler_params=_params(2),
        name=f"attn_prompt_d{dil}",
    )(qkv, qkv, qkv, qkv, qkv, _attn_bias(dil))


def _decode_tables(n_past):
    dist = n_past - np.arange(n_past)
    mult = np.zeros(n_past, np.float32)
    for window, dil in A_CONFIGS:
        mult += ((dist % dil == 0) & (dist // dil <= window // dil)).astype(np.float32)
    bias = -_alibi_slopes()[:, None] * dist.astype(np.float32)[None, :]
    return jnp.asarray(bias), jnp.asarray(mult[None, :])


def _attn_decode_body(q_ref, kn_ref, vn_ref, kc_ref, vc_ref, bias_ref, mult_ref, o_ref):
    q_t = q_ref[...]
    s_new = jnp.sum(q_t * kn_ref[...], axis=0, keepdims=True)
    mult = mult_ref[...]
    live = mult > 0.0
    n_cfg = float(len(A_CONFIGS))
    lane = lax.broadcasted_iota(jnp.int32, q_t.shape, 1)
    out = jnp.zeros(q_t.shape, F32)
    for h in range(A_HEADS):
        s = jnp.sum(kc_ref[h] * q_t[:, h:h + 1], axis=0, keepdims=True) + bias_ref[h:h + 1, :]
        s = jnp.where(live, s, NEG_BIG)
        s0 = s_new[:, h:h + 1]
        mx = jnp.maximum(jnp.max(s, axis=-1, keepdims=True), s0)
        p = jnp.exp(s - mx) * mult
        p0 = jnp.exp(s0 - mx) * n_cfg
        den = jnp.sum(p, axis=-1, keepdims=True) + p0
        acc = jnp.sum(vc_ref[h] * p, axis=-1, keepdims=True) + p0 * vn_ref[:, h:h + 1]
        out = jnp.where(lane == h, acc / den, out)
    o_ref[...] = out


def _attn_decode(q_t, kn_t, vn_t, k_cache_t, v_cache_t, layer):
    nb = q_t.shape[0]
    n_past = k_cache_t.shape[-1]
    assert n_past == A_MAX_WINDOW
    bias, mult = _decode_tables(n_past)
    new = pl.BlockSpec((None, A_HEAD_DIM, A_HEADS), lambda b: (b, 0, 0))
    past = pl.BlockSpec((None, None, A_HEADS, A_HEAD_DIM, n_past), lambda b: (layer, b, 0, 0, 0))
    return pl.pallas_call(
        _attn_decode_body,
        grid=(nb,),
        in_specs=[new, new, new, past, past, _const_spec(bias.shape), _const_spec(mult.shape)],
        out_specs=new,
        out_shape=jax.ShapeDtypeStruct((nb, A_HEAD_DIM, A_HEADS), F32),
        compiler_params=_params(1),
        name="attn_decode",
    )(q_t, kn_t, vn_t, k_cache_t, v_cache_t, bias, mult)


def _rec_tables(c, single, stacked_heads):
    nl = 0 if single else int(math.log2(c))
    assert single or 2 ** nl == c
    t = np.arange(c)[:, None]
    i = np.arange(c)[None, :]
    e_rows, masks = [], []
    for lvl in range(nl):
        half = c >> (lvl + 1)
        mid = (t // (2 * half)) * (2 * half) + half
        right = t >= mid
        e_rows.append(np.where(right, (i >= mid) & (i <= t), (i > t) & (i < mid)))
        same = (t // (2 * half)) == (i // (2 * half))
        masks.append(same & right & (i < mid))
    masks.append(t == i)
    e_rows.append(i <= t)
    e_rows.append(i > t)
    e = np.concatenate(e_rows, 0).astype(np.float32)
    m = np.stack(masks, 0).astype(np.float32)
    if stacked_heads > 1:
        m = np.tile(m, (1, stacked_heads, 1))
    return jnp.asarray(np.concatenate([e, e], 1), dtype=BF16), jnp.asarray(m)


def _rec_core(q, k, lf, v, e_ref, m_ref, st_ref, o_ref, *, c, kind, single):
    nl = 0 if single else int(math.log2(c))
    n_win = q.shape[1] // LANES
    win = lambda x, w: x[:, w * LANES:(w + 1) * LANES]
    if single:
        e_cum = jnp.exp(lf)
        ke_f = k
    else:
        l1 = lf.astype(BF16)
        l2 = (lf - l1.astype(F32)).astype(BF16)
        decay = jnp.exp(_dot(e_ref[...], jnp.concatenate([l1, l2], axis=0)))
        seg = lambda n: decay[n * c:(n + 1) * c]
        e_cum = seg(nl)
        ke_f = k * seg(nl + 1)
    row = lax.broadcasted_iota(jnp.int32, (c, LANES), 0)

    def level_operand(n, w):
        half = c >> (n + 1)
        qw, kw = win(q, w), win(k, w)
        if half >= 8:
            pieces = [(qw if j % 2 else kw)[j * half:(j + 1) * half] for j in range(c // half)]
            side = jnp.concatenate(pieces, axis=0)
        else:
            side = jnp.where((row & half) != 0, qw, kw)
        return (side * win(seg(n), w)).astype(BF16)

    u = [[level_operand(n, w) for w in range(n_win)] for n in range(nl)]
    uq = [win(q, w).astype(BF16) for w in range(n_win)]
    uk = [win(k, w).astype(BF16) for w in range(n_win)]
    qg = (q * e_cum).astype(BF16)
    ke = ke_f.astype(BF16)
    e_tot = e_cum[0:1, :] if single else e_cum[c - 1:c, :]
    vb = v.astype(BF16)
    lane = lax.broadcasted_iota(jnp.int32, (c, LANES), 1)
    low = lane < B_VAL_DIM
    if kind == "c":
        head = lax.broadcasted_iota(jnp.int32, (1, LANES), 1) // C_KEY_DIM
        hm = [(head == h).astype(F32).astype(BF16) for h in range(C_HEADS)]
        stack = lambda x: jnp.concatenate([x * hm[h] for h in range(C_HEADS)], axis=0)
        att_all = _dot_nt(stack(uq[0]), uk[0]) * m_ref[nl]
        for n in range(nl):
            att_all = att_all + _dot_nt(stack(u[n][0]), u[n][0]) * m_ref[n]
        att_all = att_all.astype(BF16)
        srow = lax.broadcasted_iota(jnp.int32, (LANES, LANES), 0) // B_VAL_DIM
        slane = lax.broadcasted_iota(jnp.int32, (LANES, LANES), 1) // C_KEY_DIM
    for pair in range(2):
        v2 = win(vb, pair)
        st = st_ref[pair]
        stb = st.astype(BF16)
        if kind == "c":
            outs = [_dot(att_all[h * c:(h + 1) * c], v2) for h in (2 * pair, 2 * pair + 1)]
            o_ref[:, pair * LANES:(pair + 1) * LANES] = jnp.where(low, outs[0], outs[1]) + _dot_nt(qg, stb)
            own = (slane == srow + 2 * pair).astype(F32)
            st_ref[pair] = st * e_tot + _dot_tn(v2, ke) * own
            continue
        outs = []
        for hh in range(2):
            h = 2 * pair + hh
            att = _dot_nt(uq[h], uk[h]) * m_ref[nl]
            for n in range(nl):
                att = att + _dot_nt(u[n][h], u[n][h]) * m_ref[n]
            outs.append(_dot(att.astype(BF16), v2) + _dot_nt(win(qg, h), stb))
            rows = slice(hh * B_VAL_DIM, (hh + 1) * B_VAL_DIM)
            upd = _dot_tn(v2, win(ke, h))
            st_ref[pair, rows, :] = st[rows, :] * win(e_tot, h) + upd[rows, :]
        o_ref[:, pair * LANES:(pair + 1) * LANES] = jnp.where(low, outs[0], outs[1])


def _rec_body(*refs, kind, c, n_valid, n_b, n_sub):
    if kind == "b":
        z_ref, lb_ref, e_ref, m_ref, s0_ref, o_ref, s_ref, st_ref = refs
    else:
        z_ref, cwg_ref, cb_ref, e_ref, m_ref, s0_ref, o_ref, s_ref, st_ref = refs

    @pl.when(pl.program_id(1) == 0)
    def _():
        st_ref[...] = s0_ref[...]

    for bb in range(n_b):
        for j in range(n_sub):
            z = z_ref[bb, j * c:(j + 1) * c, :]
            _rec_chunk(z, refs, st_ref.at[bb], o_ref.at[bb, pl.ds(j * c, c)], kind=kind, c=c, n_valid=n_valid)

    @pl.when(pl.program_id(1) == pl.num_programs(1) - 1)
    def _():
        s_ref[...] = st_ref[...]


def _rec_chunk(z, refs, st_ref, o_ref, *, kind, c, n_valid):
    if kind == "b":
        _, lb_ref, e_ref, m_ref = refs[:4]
    else:
        _, cwg_ref, cb_ref, e_ref, m_ref = refs[:5]
    if kind == "b":
        lb = lb_ref[...]
        f = lb + (1.0 - lb) * _sigmoid(z[:, B_KW:2 * B_KW])
        lf = jnp.log(jnp.maximum(f, MIN_F))
        k = 1.0 - f
        zq = z[:, 0:B_KW]
        q = zq * _sigmoid(zq)
        v = z[:, 2 * B_KW:2 * B_KW + B_WIDTH]
    else:
        q = z[:, 0:C_KW]
        k = z[:, C_KW:2 * C_KW]
        v = z[:, 2 * C_KW:2 * C_KW + C_WIDTH]
        x = _dot(z[:, ZC_WIDTH - LANES:].astype(BF16), cwg_ref[...]) + cb_ref[...]
        lf = (jnp.minimum(x, 0.0) - jnp.log1p(jnp.exp(-jnp.abs(x)))) / C_GATE_TEMP
    if n_valid < c:
        live = lax.broadcasted_iota(jnp.int32, (c, 1), 0) < n_valid
        q = jnp.where(live, q, 0.0)
        k = jnp.where(live, k, 0.0)
        lf = jnp.where(live, lf, 0.0)
    _rec_core(q, k, lf, v, e_ref, m_ref, st_ref, o_ref, c=c, kind=kind, single=n_valid == 1)


def _recurrence(kind, z, layer, extra, s0, s0_layer, n_valid=None):
    nb, t, width = z.shape
    c = min(REC_CHUNK, t)
    assert t % c == 0
    n_valid = c if n_valid is None else n_valid
    assert n_valid in (1, c)
    n_sub = math.gcd(t // c, REC_CHUNKS_PER_STEP)
    n_b = math.gcd(nb, REC_SEQS_PER_STEP)
    e, m = _rec_tables(c, n_valid == 1, C_HEADS if kind == "c" else 1)
    st_spec = pl.BlockSpec((n_b, 2, LANES, LANES), lambda b, j: (b, 0, 0, 0))
    s0_spec = pl.BlockSpec((None, n_b, 2, LANES, LANES), lambda b, j: (s0_layer, b, 0, 0, 0))
    extra_specs = [_layer_spec(x, layer) for x in extra]
    return pl.pallas_call(
        functools.partial(_rec_body, kind=kind, c=c, n_valid=n_valid, n_b=n_b, n_sub=n_sub),
        grid=(nb // n_b, t // (c * n_sub)),
        in_specs=[pl.BlockSpec((n_b, n_sub * c, width), lambda b, j: (b, j, 0))] + extra_specs
                 + [_const_spec(e.shape), _const_spec(m.shape), s0_spec],
        out_specs=[pl.BlockSpec((n_b, n_sub * c, 2 * LANES), lambda b, j: (b, j, 0)), st_spec],
        out_shape=[jax.ShapeDtypeStruct((nb, t, 2 * LANES), F32), jax.ShapeDtypeStruct((nb, 2, LANES, LANES), F32)],
        scratch_shapes=[pltpu.VMEM((n_b, 2, LANES, LANES), F32)],
        compiler_params=_params(2),
        name=f"recurrence_{kind}",
    )(z, *extra, e, m, s0)


def _mix_body(*refs, tm, dils):
    h_ref = refs[0]
    n_att = 2 * len(dils) if dils else 1
    att_refs = refs[1:1 + n_att]
    ob_ref, bg_ref, oc_ref, cg_ref, bn_ref, cn_ref, ex_ref, bd_ref, w_ref, out_ref, mixed = refs[1 + n_att:12 + n_att]
    if dils:
        o_refs, l_refs = att_refs[:len(dils)], att_refs[len(dils):]
        o_nat, l_nat = refs[12 + n_att], refs[13 + n_att]
        n_cb = A_WIDTH // LANES
        lses = []
        for c, d in enumerate(dils):
            if d == 1:
                lses.append(l_refs[c][0])
                continue
            for r in range(d):
                l_nat[c, pl.ds(r, tm // d, stride=d), :] = l_refs[c][r]
                for cb in range(n_cb):
                    o_nat[c, cb, pl.ds(r, tm // d, stride=d), :] = (
                        o_refs[c][r, :, cb * LANES:(cb + 1) * LANES].astype(F32))
            lses.append(l_nat[c])
        mx = functools.reduce(jnp.maximum, lses)
        es = [jnp.exp(l - mx) for l in lses]
        den = functools.reduce(lambda a, b: a + b, es)
        ex = ex_ref[...]
        wts = [_dot_split2(e / den, ex) for e in es]
        for cb in range(n_cb):
            cs = slice(cb * LANES, (cb + 1) * LANES)
            terms = [wts[c][:, cs] * (o_refs[c][0, :, cs] if d == 1 else o_nat[c, cb]) for c, d in enumerate(dils)]
            mixed[:, cs] = functools.reduce(lambda a, b: a + b, terms).astype(BF16)
    else:
        mixed[:, 0:A_WIDTH] = att_refs[0][...].astype(BF16)
    bd = bd_ref[...]
    ob = ob_ref[...]
    obn = ob * lax.rsqrt(_dot_split2(ob * ob, bd) + RMS_EPS) * bn_ref[...]
    mixed[:, A_WIDTH:A_WIDTH + B_WIDTH] = (obn * _sigmoid(bg_ref[...])).astype(BF16)
    oc = oc_ref[...]
    ocn = oc * lax.rsqrt(_dot_split2(oc * oc, bd) + RMS_EPS) * cn_ref[...]
    cg = cg_ref[...]
    mixed[:, A_WIDTH + B_WIDTH:] = (ocn * (cg * _sigmoid(cg))).astype(BF16)
    out_ref[...] = h_ref[...] + _dot(mixed[...], w_ref[...])


def _mix(h, layer, att, dils, ob, zb, oc, zc, b_norm, c_norm, w_out):
    m, d = h.shape
    tm = min(ROW_TILE, m)
    assert m % tm == 0
    row = lambda w, blk=0: pl.BlockSpec((tm, w), lambda i: (i, blk))
    if dils:
        att_specs = [pl.BlockSpec((dd, tm // dd, wd), lambda i: (0, i, 0))
                     for wd in (A_WIDTH, LANES) for dd in dils]
        att_scratch = [pltpu.VMEM((len(dils), A_WIDTH // LANES, tm, LANES), F32),
                       pltpu.VMEM((len(dils), tm, LANES), F32)]
    else:
        att_specs, att_scratch = [row(A_WIDTH)], []
    expand = np.zeros((LANES, A_WIDTH), np.float32)
    for hd in range(A_HEADS):
        expand[hd, hd * A_HEAD_DIM:(hd + 1) * A_HEAD_DIM] = 1.0
    grp = np.arange(B_WIDTH) // B_VAL_DIM
    blockmean = (grp[:, None] == grp[None, :]).astype(np.float32) / B_VAL_DIM
    d_mix = A_WIDTH + B_WIDTH + C_WIDTH
    return pl.pallas_call(
        functools.partial(_mix_body, tm=tm, dils=tuple(dils)),
        grid=(m // tm,),
        in_specs=[row(d)] + att_specs
                 + [row(B_WIDTH), row(B_WIDTH, (2 * B_KW + B_WIDTH) // B_WIDTH),
                    row(C_WIDTH), row(C_WIDTH, (2 * C_KW + C_WIDTH) // C_WIDTH)]
                 + [_layer_spec(b_norm, layer), _layer_spec(c_norm, layer), _const_spec((LANES, A_WIDTH)),
                    _const_spec((B_WIDTH, B_WIDTH)), _layer_spec(w_out, layer)],
        out_specs=row(d),
        out_shape=jax.ShapeDtypeStruct((m, d), F32),
        scratch_shapes=[pltpu.VMEM((tm, d_mix), BF16)] + att_scratch,
        compiler_params=_params(1),
        name="mix_out",
    )(h, *att, ob, zb, oc, zc, b_norm, c_norm,
      jnp.asarray(expand, dtype=BF16), jnp.asarray(blockmean, dtype=BF16), w_out)


def _prep_w_in(w_in):
    sizes = (A_WIDTH, A_WIDTH, A_WIDTH, B_KW, B_KW, B_WIDTH, B_WIDTH, C_KW, C_KW, C_WIDTH, C_GATE_RANK, C_WIDTH)
    cuts = np.cumsum(sizes)[:-1].tolist()
    parts = jnp.split(w_in, cuts, axis=-1)
    clr = jnp.pad(parts[10], ((0, 0),) * (w_in.ndim - 1) + ((0, LANES - C_GATE_RANK),))
    return jnp.concatenate(parts[:10] + [parts[11], clr], axis=-1).astype(BF16)


def _state_to_pairs(s):
    nl, nb, nh, dk, dv = s.shape
    st = jnp.swapaxes(s, 3, 4)
    if dk < LANES:
        st = jnp.stack([jnp.pad(st[:, :, h], ((0, 0), (0, 0), (0, 0), (h * dk, LANES - (h + 1) * dk)))
                        for h in range(nh)], 2)
    return st.reshape(nl, nb, 2, 2 * dv, LANES)


def _pairs_to_state(st, dk):
    nb = st.shape[0]
    st = st.reshape(nb, 4, B_VAL_DIM, LANES)
    if dk < LANES:
        st = jnp.stack([st[:, h, :, h * dk:(h + 1) * dk] for h in range(4)], 1)
    return jnp.swapaxes(st, 2, 3)


def _trunk(x, p, wts, lb_all, final_norm, caches):
    nb, t, d = x.shape
    decode = caches is not None
    assert (t == 1) if decode else (nb == 1)
    m = nb * t
    h = x.reshape(m, d)
    w = wts
    depth = w["w_in"].shape[0]
    keep = m if decode else min(A_MAX_WINDOW, t)
    k_rows, v_rows, sb_out, sc_out = [], [], [], []
    y = None
    p = p.reshape(depth, m, -1)
    if decode:
        sb0, sc0 = _state_to_pairs(caches[2]), _state_to_pairs(caches[3])
    else:
        sb0 = sc0 = jnp.zeros((1, 1, 2, LANES, LANES), F32)
    for i in range(depth):
        h, = _ffn(h, i, w["ffn1_norm"], w["ffn1_w_gate"], w["ffn1_w_up"], w["ffn1_w_down"])
        if decode:
            qkv, kf, vf, zb, zc = _proj(h, i, w["mix_norm"], w["w_in"], keep, (), F32)
            pad = lambda a: jnp.pad(a.reshape(nb, 1, -1), ((0, 0), (0, DEC_PAD - 1), (0, 0)))
            cols = lambda a: jnp.swapaxes(a.reshape(nb, A_HEADS, A_HEAD_DIM), 1, 2)
            o_t = _attn_decode(cols(qkv[0, :, 0:A_WIDTH]), cols(kf), cols(vf), caches[0], caches[1], i)
            att, dils = [jnp.swapaxes(o_t, 1, 2).reshape(m, A_WIDTH)], ()
            zb3, zc3 = pad(zb), pad(zc)
            n_valid, s0_layer = 1, i
        else:
            dils = tuple(dil for _, dil in A_CONFIGS)
            assert dils[0] == 1
            *qkvs, kf, vf, zb, zc = _proj(h, i, w["mix_norm"], w["w_in"], keep, dils[1:], BF16)
            res = [_attn_prompt(a, dil) for a, dil in zip(qkvs, dils)]
            att = [r[0] for r in res] + [r[1] for r in res]
            zb3, zc3 = zb[None], zc[None]
            n_valid, s0_layer = None, 0
        ob, sb = _recurrence("b", zb3, i, [lb_all], sb0, s0_layer, n_valid)
        oc, sc = _recurrence("c", zc3, i, [w["c_w_gate"], w["c_gate_bias"]], sc0, s0_layer, n_valid)
        if decode:
            ob, oc = ob[:, 0], oc[:, 0]
        else:
            ob, oc = ob[0], oc[0]
        h = _mix(h, i, att, dils, ob, zb, oc, zc, w["b_out_norm"], w["c_out_norm"], w["w_out"])
        ple = (p, w["ple_norm"], w["ple_w_gate"], w["ple_w_proj"])
        if i == depth - 1:
            h, y = _ffn(h, i, w["ffn2_norm"], w["ffn2_w_gate"], w["ffn2_w_up"], w["ffn2_w_down"], ple, final_norm)
        else:
            h, = _ffn(h, i, w["ffn2_norm"], w["ffn2_w_gate"], w["ffn2_w_up"], w["ffn2_w_down"], ple)
        k_rows.append(kf.reshape(nb, keep // nb, A_HEADS, A_HEAD_DIM))
        v_rows.append(vf.reshape(nb, keep // nb, A_HEADS, A_HEAD_DIM))
        sb_out.append(_pairs_to_state(sb, B_KEY_DIM))
        sc_out.append(_pairs_to_state(sc, C_KEY_DIM))
    return (y.reshape(nb, t, d), jnp.stack(k_rows), jnp.stack(v_rows), jnp.stack(sb_out), jnp.stack(sc_out))


def kernel(x_prompt, x_sample, cache_k_a, cache_v_a, state_b, state_c, p_prompt, p_sample, ffn1_norm, ffn1_w_gate, ffn1_w_up, ffn1_w_down, mix_norm, w_in, lb_logits, b_out_norm, c_w_gate, c_gate_bias, c_out_norm, w_out, ffn2_norm, ffn2_w_gate, ffn2_w_up, ffn2_w_down, ple_norm, ple_w_gate, ple_w_proj, final_norm):
    depth = w_in.shape[0]
    rows = lambda a: a.reshape(depth, 1, -1)
    wts = {name: a.astype(BF16) for name, a in dict(
        ffn1_w_gate=ffn1_w_gate, ffn1_w_up=ffn1_w_up, ffn1_w_down=ffn1_w_down, w_out=w_out, ffn2_w_gate=ffn2_w_gate,
        ffn2_w_up=ffn2_w_up, ffn2_w_down=ffn2_w_down, ple_w_gate=ple_w_gate, ple_w_proj=ple_w_proj).items()}
    wts.update(
        ffn1_norm=rows(ffn1_norm), mix_norm=rows(mix_norm), w_in=_prep_w_in(w_in), b_out_norm=rows(b_out_norm),
        c_w_gate=jnp.pad(c_w_gate, ((0, 0), (0, LANES - C_GATE_RANK), (0, 0))).astype(BF16),
        c_gate_bias=rows(c_gate_bias), c_out_norm=rows(c_out_norm), ffn2_norm=rows(ffn2_norm), ple_norm=rows(ple_norm))
    sm = jax.nn.softmax(lb_logits.astype(F32), axis=0)
    lb_all = jnp.maximum(jnp.cumsum(sm, axis=0) - sm[0], 0.0).reshape(depth, 1, B_KW)
    fn = final_norm.reshape(1, -1)
    y_p, k_p, v_p, sb_p, sc_p = _trunk(x_prompt, p_prompt, wts, lb_all, fn, None)
    k_cache_t = jnp.transpose(cache_k_a, (0, 1, 3, 4, 2))
    v_cache_t = jnp.transpose(cache_v_a, (0, 1, 3, 4, 2))
    y_s, k_s, v_s, sb_s, sc_s = _trunk(x_sample, p_sample, wts, lb_all, fn, (k_cache_t, v_cache_t, state_b, state_c))
    return (y_p, y_s, k_p, v_p, sb_p, sc_p, k_s, v_s, sb_s, sc_s)
```

```python
import functools
import math

import numpy as np
import jax
import jax.numpy as jnp
from jax import lax
from jax.experimental import pallas as pl
from jax.experimental.pallas import tpu as pltpu

F32 = jnp.float32
BF16 = jnp.bfloat16

RMS_EPS = 1e-6
NEG_BIG = -1e30
MIN_F = 1e-30
A_HEADS = 8
A_HEAD_DIM = 64
A_CONFIGS = ((128, 1), (512, 4), (2048, 16))
A_MAX_WINDOW = 2048
A_SUB = 128
B_HEADS = 4
B_KEY_DIM = 128
B_VAL_DIM = 64
C_HEADS = 4
C_KEY_DIM = 32
C_VAL_DIM = 64
C_GATE_RANK = 16
C_GATE_TEMP = 16.0
A_WIDTH = A_HEADS * A_HEAD_DIM
B_KW = B_HEADS * B_KEY_DIM
B_WIDTH = B_HEADS * B_VAL_DIM
C_KW = C_HEADS * C_KEY_DIM
C_WIDTH = C_HEADS * C_VAL_DIM
ZB_WIDTH = 2 * B_KW + 2 * B_WIDTH
ZC_WIDTH = 2 * C_KW + 2 * C_WIDTH + 128
PROJ_PAD = 3 * A_WIDTH + ZB_WIDTH + ZC_WIDTH

LANES = 128
MXU_WIDTH = 256
VMEM_LIMIT_BYTES = 56 * 1024 * 1024

ROW_TILE = 512
FF_CHUNK = 1536
REC_CHUNK = 128
REC_CHUNKS_PER_STEP = 4
REC_SEQS_PER_STEP = 8
ATT_ROWS = 1024
DEC_PAD = 16


def _params(n_axes):
    return pltpu.CompilerParams(dimension_semantics=("arbitrary",) * n_axes,
                                vmem_limit_bytes=VMEM_LIMIT_BYTES)


def _const_spec(shape):
    zeros = (0,) * len(shape)
    return pl.BlockSpec(shape, lambda *_: zeros, pipeline_mode=pl.Buffered(1))


def _layer_spec(stacked, layer):
    shape = tuple(stacked.shape[1:])
    zeros = (0,) * len(shape)
    return pl.BlockSpec((None,) + shape, lambda *_: (layer,) + zeros, pipeline_mode=pl.Buffered(1))


def _dot(a, b):
    return jnp.dot(a, b, preferred_element_type=F32)


def _dot_nt(a, b):
    return lax.dot_general(a, b, (((1,), (1,)), ((), ())), preferred_element_type=F32)


def _dot_tn(a, b):
    return lax.dot_general(a, b, (((0,), (0,)), ((), ())), preferred_element_type=F32)


def _dot_split2(x, w):
    hi = x.astype(BF16)
    lo = (x - hi.astype(F32)).astype(BF16)
    return _dot(hi, w) + _dot(lo, w)


def _rms(x, g):
    ms = jnp.mean(x * x, axis=-1, keepdims=True)
    return x * lax.rsqrt(ms + RMS_EPS) * g


def _sigmoid(x):
    return 1.0 / (1.0 + jnp.exp(-x))


def _ff_chunks(dff):
    unit = MXU_WIDTH if dff % MXU_WIDTH == 0 else LANES
    assert dff % unit == 0
    tiles = dff // unit
    n = -(-dff // FF_CHUNK)
    sizes = [(tiles // n + (1 if i < tiles % n else 0)) * unit for i in range(n)]
    starts = np.cumsum([0] + sizes[:-1]).tolist()
    return tuple(zip(starts, sizes))


def _ffn_body(*refs, chunks, with_ple, with_final):
    refs = list(refs)
    h_ref, nrm_ref, wg_ref, wu_ref, wd_ref = refs[:5]
    pos = 5
    if with_ple:
        p_ref, pn_ref, pwg_ref, pwp_ref = refs[pos:pos + 4]
        pos += 4
    if with_final:
        fn_ref = refs[pos]
        pos += 1
    o_ref = refs[pos]
    h = h_ref[...]
    xn = _rms(h, nrm_ref[...]).astype(BF16)
    acc = None
    for lo, chunk in chunks:
        g = _dot(xn, wg_ref[:, lo:lo + chunk])
        u = _dot(xn, wu_ref[:, lo:lo + chunk])
        a = (g * _sigmoid(g) * u).astype(BF16)
        d = _dot(a, wd_ref[lo:lo + chunk, :])
        acc = d if acc is None else acc + d
    h = h + 0.5 * acc
    if with_ple:
        hn = _rms(h, pn_ref[...]).astype(BF16)
        gate = _sigmoid(_dot(hn, pwg_ref[...]))
        proj = _dot(p_ref[...].astype(BF16), pwp_ref[...])
        h = h + gate * proj
    o_ref[...] = h
    if with_final:
        refs[pos + 1][...] = _rms(h, fn_ref[...])


def _ffn(h, layer, nrm, wg, wu, wd, ple=None, final_norm=None):
    m, d = h.shape
    dff = wg.shape[-1]
    tm = min(ROW_TILE, m)
    assert m % tm == 0
    row = lambda w: pl.BlockSpec((tm, w), lambda i: (i, 0))
    ins = [h, nrm, wg, wu, wd]
    specs = [row(d)] + [_layer_spec(a, layer) for a in ins[1:]]
    if ple is not None:
        ins += list(ple)
        specs += [pl.BlockSpec((None, tm, ple[0].shape[-1]), lambda i: (layer, i, 0))]
        specs += [_layer_spec(a, layer) for a in ple[1:]]
    if final_norm is not None:
        ins.append(final_norm)
        specs.append(_const_spec((1, d)))
    n_out = 2 if final_norm is not None else 1
    out = pl.pallas_call(
        functools.partial(_ffn_body, chunks=_ff_chunks(dff), with_ple=ple is not None,
                          with_final=final_norm is not None),
        grid=(m // tm,),
        in_specs=specs,
        out_specs=[row(d)] * n_out,
        out_shape=[jax.ShapeDtypeStruct((m, d), F32)] * n_out,
        compiler_params=_params(1),
        name="ffn",
    )(*ins)
    return out


def _proj_body(h_ref, nrm_ref, w_ref, *refs, tm, dils):
    n_a = 1 + len(dils)
    a_refs = refs[:n_a]
    kf_ref, vf_ref, zb_ref, zc_ref = refs[n_a:n_a + 4]
    xn = _rms(h_ref[...], nrm_ref[...]).astype(BF16)
    a = A_WIDTH
    b0 = 3 * a
    for part in range(3):
        cols = slice(part * a, (part + 1) * a)
        z = _dot(xn, w_ref[:, cols])
        if part == 0:
            z = z * (A_HEAD_DIM ** -0.5)
        elif part == 1:
            kf_ref[...] = z
        else:
            vf_ref[...] = z
        a_refs[0][0, :, cols] = z.astype(a_refs[0].dtype)
        if dils:
            stage = refs[-1]
            for cb in range(a // LANES):
                stage[cb] = z[:, cb * LANES:(cb + 1) * LANES]
        for ref, d in zip(a_refs[1:], dils):
            for r in range(d):
                for cb in range(a // LANES):
                    lo = part * a + cb * LANES
                    ref[r, :, lo:lo + LANES] = stage[cb, pl.ds(r, tm // d, stride=d), :].astype(ref.dtype)
    zb_ref[...] = _dot(xn, w_ref[:, b0:b0 + ZB_WIDTH])
    c0 = b0 + ZB_WIDTH
    zc_ref[:, 0:C_KW] = _dot(xn, w_ref[:, c0:c0 + C_KW]) * (C_KEY_DIM ** -0.5)
    zc_ref[:, C_KW:] = _dot(xn, w_ref[:, c0 + C_KW:c0 + ZC_WIDTH])


def _proj(h, layer, nrm, w, keep, dils, a_dtype):
    m, d = h.shape
    tm = min(ROW_TILE, m)
    assert m % tm == 0 and keep % tm == 0
    nt, nk = m // tm, keep // tm
    row = lambda wd: pl.BlockSpec((tm, wd), lambda i: (i, 0))
    tail = pl.BlockSpec((tm, A_WIDTH), lambda i: (jnp.maximum(i - (nt - nk), 0), 0))
    sds = jax.ShapeDtypeStruct
    all_d = (1,) + tuple(dils)
    return pl.pallas_call(
        functools.partial(_proj_body, tm=tm, dils=tuple(dils)),
        grid=(nt,),
        in_specs=[row(d), _layer_spec(nrm, layer), _layer_spec(w, layer)],
        out_specs=[pl.BlockSpec((dd, tm // dd, 3 * A_WIDTH), lambda i: (0, i, 0)) for dd in all_d]
                  + [tail, tail, row(ZB_WIDTH), row(ZC_WIDTH)],
        out_shape=[sds((dd, m // dd, 3 * A_WIDTH), a_dtype) for dd in all_d]
                  + [sds((keep, A_WIDTH), F32), sds((keep, A_WIDTH), F32),
                     sds((m, ZB_WIDTH), F32), sds((m, ZC_WIDTH), F32)],
        scratch_shapes=[pltpu.VMEM((A_WIDTH // LANES, tm, LANES), F32)] if dils else [],
        compiler_params=_params(1),
        name="proj",
    )(h, nrm, w)


def _alibi_slopes():
    return (2.0 ** (-8.0 * np.arange(1, A_HEADS + 1, dtype=np.float32) / A_HEADS)).astype(np.float32)


def _attn_bias(dil):
    qi = np.arange(A_SUB)[:, None]
    ki = np.arange(2 * A_SUB)[None, :]
    j = qi + A_SUB - ki
    band = (j >= 0) & (j <= A_SUB)
    bias = -_alibi_slopes()[:, None, None] * (j * dil).astype(np.float32)[None]
    return jnp.asarray(np.where(band[None], bias, np.float32(NEG_BIG)).astype(np.float32))


def _attn_body(q_ref, kp_ref, kc_ref, vp_ref, vc_ref, bias_ref, o_ref, lse_ref, kk, vv, *, n_sub):
    qb = A_SUB
    kk[0:A_SUB, :] = kp_ref[...]
    vv[0:A_SUB, :] = vp_ref[...]
    kk[A_SUB:, :] = kc_ref[...]
    vv[A_SUB:, :] = vc_ref[...]
    lane = lax.broadcasted_iota(jnp.int32, (qb, LANES), 1)
    low = lane < A_HEAD_DIM
    low_f = low.astype(F32)
    head_lanes = (low_f.astype(BF16), (1.0 - low_f).astype(BF16))
    first = jnp.where(pl.program_id(1) > 0, 0, A_SUB)
    key_ok = lax.broadcasted_iota(jnp.int32, (qb, 2 * A_SUB), 1) >= first
    for i in range(n_sub):
        rows = slice(i * qb, (i + 1) * qb)
        krows = slice(i * qb, i * qb + 2 * A_SUB)
        lse_tile = jnp.zeros((qb, LANES), F32)
        for hp in range(A_HEADS // 2):
            cols = slice(hp * LANES, (hp + 1) * LANES)
            q2 = q_ref[rows, cols]
            k2 = kk[krows, cols]
            v2 = vv[krows, cols]
            outs = []
            for hh in range(2):
                h = 2 * hp + hh
                s = _dot_nt(q2 * head_lanes[hh], k2) + bias_ref[h]
                if i == 0:
                    s = jnp.where(key_ok, s, NEG_BIG)
                mx = jnp.max(s, axis=-1, keepdims=True)
                p = jnp.exp(s - mx)
                den = jnp.sum(p, axis=-1, keepdims=True)
                outs.append(_dot(p.astype(BF16), v2) / den)
                lse_tile = jnp.where(lane == h, mx + jnp.log(den), lse_tile)
            o_ref[rows, cols] = jnp.where(low, outs[0], outs[1]).astype(o_ref.dtype)
        lse_ref[rows, :] = lse_tile


def _attn_prompt(qkv, dil):
    _, tv, _ = qkv.shape
    assert qkv.shape[0] == dil and tv % A_SUB == 0
    bq = min(ATT_ROWS, tv)
    assert tv % bq == 0 and bq % A_SUB == 0
    cur = lambda part: pl.BlockSpec((None, bq, A_WIDTH), lambda r, n: (r, n, part))
    prev = lambda part: pl.BlockSpec((None, A_SUB, A_WIDTH),
                                     lambda r, n: (r, jnp.maximum(n * (bq // A_SUB) - 1, 0), part))
    return pl.pallas_call(
        functools.partial(_attn_body, n_sub=bq // A_SUB),
        grid=(dil, tv // bq),
        in_specs=[cur(0), prev(1), cur(1), prev(2), cur(2), _const_spec((A_HEADS, A_SUB, 2 * A_SUB))],
        out_specs=[cur(0), pl.BlockSpec((None, bq, LANES), lambda r, n: (r, n, 0))],
        out_shape=[jax.ShapeDtypeStruct((dil, tv, A_WIDTH), BF16), jax.ShapeDtypeStruct((dil, tv, LANES), F32)],
        scratch_shapes=[pltpu.VMEM((A_SUB + bq, A_WIDTH), BF16), pltpu.VMEM((A_SUB + bq, A_WIDTH), BF16)],
        compiler_params=_params(2),
        name=f"attn_prompt_d{dil}",
    )(qkv, qkv, qkv, qkv, qkv, _attn_bias(dil))


def _decode_tables(n_past):
    dist = n_past - np.arange(n_past)
    mult = np.zeros(n_past, np.float32)
    for window, dil in A_CONFIGS:
        mult += ((dist % dil == 0) & (dist // dil <= window // dil)).astype(np.float32)
    bias = -_alibi_slopes()[:, None] * dist.astype(np.float32)[None, :]
    return jnp.asarray(bias), jnp.asarray(mult[None, :])


def _attn_decode_body(q_ref, kn_ref, vn_ref, kc_ref, vc_ref, bias_ref, mult_ref, o_ref):
    q_t = q_ref[...]
    s_new = jnp.sum(q_t * kn_ref[...], axis=0, keepdims=True)
    mult = mult_ref[...]
    live = mult > 0.0
    n_cfg = float(len(A_CONFIGS))
    lane = lax.broadcasted_iota(jnp.int32, q_t.shape, 1)
    out = jnp.zeros(q_t.shape, F32)
    for h in range(A_HEADS):
        s = jnp.sum(kc_ref[h] * q_t[:, h:h + 1], axis=0, keepdims=True) + bias_ref[h:h + 1, :]
        s = jnp.where(live, s, NEG_BIG)
        s0 = s_new[:, h:h + 1]
        mx = jnp.maximum(jnp.max(s, axis=-1, keepdims=True), s0)
        p = jnp.exp(s - mx) * mult
        p0 = jnp.exp(s0 - mx) * n_cfg
        den = jnp.sum(p, axis=-1, keepdims=True) + p0
        acc = jnp.sum(vc_ref[h] * p, axis=-1, keepdims=True) + p0 * vn_ref[:, h:h + 1]
        out = jnp.where(lane == h, acc / den, out)
    o_ref[...] = out


def _attn_decode(q_t, kn_t, vn_t, k_cache_t, v_cache_t, layer):
    nb = q_t.shape[0]
    n_past = k_cache_t.shape[-1]
    assert n_past == A_MAX_WINDOW
    bias, mult = _decode_tables(n_past)
    new = pl.BlockSpec((None, A_HEAD_DIM, A_HEADS), lambda b: (b, 0, 0))
    past = pl.BlockSpec((None, None, A_HEADS, A_HEAD_DIM, n_past), lambda b: (layer, b, 0, 0, 0))
    return pl.pallas_call(
        _attn_decode_body,
        grid=(nb,),
        in_specs=[new, new, new, past, past, _const_spec(bias.shape), _const_spec(mult.shape)],
        out_specs=new,
        out_shape=jax.ShapeDtypeStruct((nb, A_HEAD_DIM, A_HEADS), F32),
        compiler_params=_params(1),
        name="attn_decode",
    )(q_t, kn_t, vn_t, k_cache_t, v_cache_t, bias, mult)


def _rec_tables(c, single, stacked_heads):
    nl = 0 if single else int(math.log2(c))
    assert single or 2 ** nl == c
    t = np.arange(c)[:, None]
    i = np.arange(c)[None, :]
    e_rows, masks = [], []
    for lvl in range(nl):
        half = c >> (lvl + 1)
        mid = (t // (2 * half)) * (2 * half) + half
        right = t >= mid
        e_rows.append(np.where(right, (i >= mid) & (i <= t), (i > t) & (i < mid)))
        same = (t // (2 * half)) == (i // (2 * half))
        masks.append(same & right & (i < mid))
    masks.append(t == i)
    e_rows.append(i <= t)
    e_rows.append(i > t)
    e = np.concatenate(e_rows, 0).astype(np.float32)
    m = np.stack(masks, 0).astype(np.float32)
    if stacked_heads > 1:
        m = np.tile(m, (1, stacked_heads, 1))
    return jnp.asarray(np.concatenate([e, e], 1), dtype=BF16), jnp.asarray(m)


def _rec_core(q, k, lf, v, e_ref, m_ref, st_ref, o_ref, *, c, kind, single):
    nl = 0 if single else int(math.log2(c))
    n_win = q.shape[1] // LANES
    win = lambda x, w: x[:, w * LANES:(w + 1) * LANES]
    if single:
        e_cum = jnp.exp(lf)
        ke_f = k
    else:
        l1 = lf.astype(BF16)
        l2 = (lf - l1.astype(F32)).astype(BF16)
        decay = jnp.exp(_dot(e_ref[...], jnp.concatenate([l1, l2], axis=0)))
        seg = lambda n: decay[n * c:(n + 1) * c]
        e_cum = seg(nl)
        ke_f = k * seg(nl + 1)
    row = lax.broadcasted_iota(jnp.int32, (c, LANES), 0)

    def level_operand(n, w):
        half = c >> (n + 1)
        qw, kw = win(q, w), win(k, w)
        if half >= 8:
            pieces = [(qw if j % 2 else kw)[j * half:(j + 1) * half] for j in range(c // half)]
            side = jnp.concatenate(pieces, axis=0)
        else:
            side = jnp.where((row & half) != 0, qw, kw)
        return (side * win(seg(n), w)).astype(BF16)

    u = [[level_operand(n, w) for w in range(n_win)] for n in range(nl)]
    uq = [win(q, w).astype(BF16) for w in range(n_win)]
    uk = [win(k, w).astype(BF16) for w in range(n_win)]
    qg = (q * e_cum).astype(BF16)
    ke = ke_f.astype(BF16)
    e_tot = e_cum[0:1, :] if single else e_cum[c - 1:c, :]
    vb = v.astype(BF16)
    lane = lax.broadcasted_iota(jnp.int32, (c, LANES), 1)
    low = lane < B_VAL_DIM
    if kind == "c":
        head = lax.broadcasted_iota(jnp.int32, (1, LANES), 1) // C_KEY_DIM
        hm = [(head == h).astype(F32).astype(BF16) for h in range(C_HEADS)]
        stack = lambda x: jnp.concatenate([x * hm[h] for h in range(C_HEADS)], axis=0)
        att_all = _dot_nt(stack(uq[0]), uk[0]) * m_ref[nl]
        for n in range(nl):
            att_all = att_all + _dot_nt(stack(u[n][0]), u[n][0]) * m_ref[n]
        att_all = att_all.astype(BF16)
        srow = lax.broadcasted_iota(jnp.int32, (LANES, LANES), 0) // B_VAL_DIM
        slane = lax.broadcasted_iota(jnp.int32, (LANES, LANES), 1) // C_KEY_DIM
    for pair in range(2):
        v2 = win(vb, pair)
        st = st_ref[pair]
        stb = st.astype(BF16)
        if kind == "c":
            outs = [_dot(att_all[h * c:(h + 1) * c], v2) for h in (2 * pair, 2 * pair + 1)]
            o_ref[:, pair * LANES:(pair + 1) * LANES] = jnp.where(low, outs[0], outs[1]) + _dot_nt(qg, stb)
            own = (slane == srow + 2 * pair).astype(F32)
            st_ref[pair] = st * e_tot + _dot_tn(v2, ke) * own
            continue
        outs = []
        for hh in range(2):
            h = 2 * pair + hh
            att = _dot_nt(uq[h], uk[h]) * m_ref[nl]
            for n in range(nl):
                att = att + _dot_nt(u[n][h], u[n][h]) * m_ref[n]
            outs.append(_dot(att.astype(BF16), v2) + _dot_nt(win(qg, h), stb))
            rows = slice(hh * B_VAL_DIM, (hh + 1) * B_VAL_DIM)
            upd = _dot_tn(v2, win(ke, h))
            st_ref[pair, rows, :] = st[rows, :] * win(e_tot, h) + upd[rows, :]
        o_ref[:, pair * LANES:(pair + 1) * LANES] = jnp.where(low, outs[0], outs[1])


_REC_KINDS = ("b", "c")
_REC_N_PARAMS = {"b": 1, "c": 2}


def _rec_body(*refs, c, n_valid, n_b, n_sub):
    ins, pos = {}, 0
    for kind in _REC_KINDS:
        n = 4 + _REC_N_PARAMS[kind]
        ins[kind] = refs[pos:pos + n]
        pos += n
    outs = {kind: refs[pos + 2 * i:pos + 2 * i + 2] for i, kind in enumerate(_REC_KINDS)}
    pos += 2 * len(_REC_KINDS)
    states = {kind: refs[pos + i] for i, kind in enumerate(_REC_KINDS)}

    @pl.when(pl.program_id(1) == 0)
    def _():
        for kind in _REC_KINDS:
            states[kind][...] = ins[kind][-1][...]

    for bb in range(n_b):
        for j in range(n_sub):
            for kind in _REC_KINDS:
                z_ref, *params, e_ref, m_ref, _ = ins[kind]
                z = z_ref[bb, j * c:(j + 1) * c, :]
                _rec_chunk(z, params, e_ref, m_ref, states[kind].at[bb], outs[kind][0].at[bb, pl.ds(j * c, c)],
                           kind=kind, c=c, n_valid=n_valid)

    @pl.when(pl.program_id(1) == pl.num_programs(1) - 1)
    def _():
        for kind in _REC_KINDS:
            outs[kind][1][...] = states[kind][...]


def _rec_chunk(z, params, e_ref, m_ref, st_ref, o_ref, *, kind, c, n_valid):
    if kind == "b":
        lb = params[0][...]
        f = lb + (1.0 - lb) * _sigmoid(z[:, B_KW:2 * B_KW])
        lf = jnp.log(jnp.maximum(f, MIN_F))
        k = 1.0 - f
        zq = z[:, 0:B_KW]
        q = zq * _sigmoid(zq)
        v = z[:, 2 * B_KW:2 * B_KW + B_WIDTH]
    else:
        cwg_ref, cb_ref = params
        q = z[:, 0:C_KW]
        k = z[:, C_KW:2 * C_KW]
        v = z[:, 2 * C_KW:2 * C_KW + C_WIDTH]
        x = _dot(z[:, ZC_WIDTH - LANES:].astype(BF16), cwg_ref[...]) + cb_ref[...]
        lf = (jnp.minimum(x, 0.0) - jnp.log1p(jnp.exp(-jnp.abs(x)))) / C_GATE_TEMP
    if n_valid < c:
        live = lax.broadcasted_iota(jnp.int32, (c, 1), 0) < n_valid
        q = jnp.where(live, q, 0.0)
        k = jnp.where(live, k, 0.0)
        lf = jnp.where(live, lf, 0.0)
    _rec_core(q, k, lf, v, e_ref, m_ref, st_ref, o_ref, c=c, kind=kind, single=n_valid == 1)


def _recurrences(zs, layer, params, s0s, s0_layer, n_valid=None):
    nb, t, _ = zs["b"].shape
    c = min(REC_CHUNK, t)
    assert t % c == 0
    n_valid = c if n_valid is None else n_valid
    assert n_valid in (1, c)
    n_sub = math.gcd(t // c, REC_CHUNKS_PER_STEP)
    n_b = math.gcd(nb, REC_SEQS_PER_STEP)
    st_spec = pl.BlockSpec((n_b, 2, LANES, LANES), lambda b, j: (b, 0, 0, 0))
    s0_spec = pl.BlockSpec((None, n_b, 2, LANES, LANES), lambda b, j: (s0_layer, b, 0, 0, 0))
    args, in_specs = [], []
    for kind in _REC_KINDS:
        assert len(params[kind]) == _REC_N_PARAMS[kind] and zs[kind].shape[:2] == (nb, t)
        e, m = _rec_tables(c, n_valid == 1, C_HEADS if kind == "c" else 1)
        args += [zs[kind], *params[kind], e, m, s0s[kind]]
        in_specs += [pl.BlockSpec((n_b, n_sub * c, zs[kind].shape[2]), lambda b, j: (b, j, 0))]
        in_specs += [_layer_spec(x, layer) for x in params[kind]]
        in_specs += [_const_spec(e.shape), _const_spec(m.shape), s0_spec]
    n_kinds = len(_REC_KINDS)
    res = pl.pallas_call(
        functools.partial(_rec_body, c=c, n_valid=n_valid, n_b=n_b, n_sub=n_sub),
        grid=(nb // n_b, t // (c * n_sub)),
        in_specs=in_specs,
        out_specs=[pl.BlockSpec((n_b, n_sub * c, 2 * LANES), lambda b, j: (b, j, 0)), st_spec] * n_kinds,
        out_shape=[jax.ShapeDtypeStruct((nb, t, 2 * LANES), F32),
                   jax.ShapeDtypeStruct((nb, 2, LANES, LANES), F32)] * n_kinds,
        scratch_shapes=[pltpu.VMEM((n_b, 2, LANES, LANES), F32)] * n_kinds,
        compiler_params=_params(2),
        name="recurrences",
    )(*args)
    return {kind: (res[2 * i], res[2 * i + 1]) for i, kind in enumerate(_REC_KINDS)}


def _mix_body(*refs, tm, dils):
    h_ref = refs[0]
    n_att = 2 * len(dils) if dils else 1
    att_refs = refs[1:1 + n_att]
    ob_ref, bg_ref, oc_ref, cg_ref, bn_ref, cn_ref, ex_ref, bd_ref, w_ref, out_ref, mixed = refs[1 + n_att:12 + n_att]
    if dils:
        o_refs, l_refs = att_refs[:len(dils)], att_refs[len(dils):]
        o_nat, l_nat = refs[12 + n_att], refs[13 + n_att]
        n_cb = A_WIDTH // LANES
        lses = []
        for c, d in enumerate(dils):
            if d == 1:
                lses.append(l_refs[c][0])
                continue
            for r in range(d):
                l_nat[c, pl.ds(r, tm // d, stride=d), :] = l_refs[c][r]
                for cb in range(n_cb):
                    o_nat[c, cb, pl.ds(r, tm // d, stride=d), :] = (
                        o_refs[c][r, :, cb * LANES:(cb + 1) * LANES].astype(F32))
            lses.append(l_nat[c])
        mx = functools.reduce(jnp.maximum, lses)
        es = [jnp.exp(l - mx) for l in lses]
        den = functools.reduce(lambda a, b: a + b, es)
        ex = ex_ref[...]
        wts = [_dot_split2(e / den, ex) for e in es]
        for cb in range(n_cb):
            cs = slice(cb * LANES, (cb + 1) * LANES)
            terms = [wts[c][:, cs] * (o_refs[c][0, :, cs] if d == 1 else o_nat[c, cb]) for c, d in enumerate(dils)]
            mixed[:, cs] = functools.reduce(lambda a, b: a + b, terms).astype(BF16)
    else:
        mixed[:, 0:A_WIDTH] = att_refs[0][...].astype(BF16)
    bd = bd_ref[...]
    ob = ob_ref[...]
    obn = ob * lax.rsqrt(_dot_split2(ob * ob, bd) + RMS_EPS) * bn_ref[...]
    mixed[:, A_WIDTH:A_WIDTH + B_WIDTH] = (obn * _sigmoid(bg_ref[...])).astype(BF16)
    oc = oc_ref[...]
    ocn = oc * lax.rsqrt(_dot_split2(oc * oc, bd) + RMS_EPS) * cn_ref[...]
    cg = cg_ref[...]
    mixed[:, A_WIDTH + B_WIDTH:] = (ocn * (cg * _sigmoid(cg))).astype(BF16)
    out_ref[...] = h_ref[...] + _dot(mixed[...], w_ref[...])


def _mix(h, layer, att, dils, ob, zb, oc, zc, b_norm, c_norm, w_out):
    m, d = h.shape
    tm = min(ROW_TILE, m)
    assert m % tm == 0
    row = lambda w, blk=0: pl.BlockSpec((tm, w), lambda i: (i, blk))
    if dils:
        att_specs = [pl.BlockSpec((dd, tm // dd, wd), lambda i: (0, i, 0))
                     for wd in (A_WIDTH, LANES) for dd in dils]
        att_scratch = [pltpu.VMEM((len(dils), A_WIDTH // LANES, tm, LANES), F32),
                       pltpu.VMEM((len(dils), tm, LANES), F32)]
    else:
        att_specs, att_scratch = [row(A_WIDTH)], []
    expand = np.zeros((LANES, A_WIDTH), np.float32)
    for hd in range(A_HEADS):
        expand[hd, hd * A_HEAD_DIM:(hd + 1) * A_HEAD_DIM] = 1.0
    grp = np.arange(B_WIDTH) // B_VAL_DIM
    blockmean = (grp[:, None] == grp[None, :]).astype(np.float32) / B_VAL_DIM
    d_mix = A_WIDTH + B_WIDTH + C_WIDTH
    return pl.pallas_call(
        functools.partial(_mix_body, tm=tm, dils=tuple(dils)),
        grid=(m // tm,),
        in_specs=[row(d)] + att_specs
                 + [row(B_WIDTH), row(B_WIDTH, (2 * B_KW + B_WIDTH) // B_WIDTH),
                    row(C_WIDTH), row(C_WIDTH, (2 * C_KW + C_WIDTH) // C_WIDTH)]
                 + [_layer_spec(b_norm, layer), _layer_spec(c_norm, layer), _const_spec((LANES, A_WIDTH)),
                    _const_spec((B_WIDTH, B_WIDTH)), _layer_spec(w_out, layer)],
        out_specs=row(d),
        out_shape=jax.ShapeDtypeStruct((m, d), F32),
        scratch_shapes=[pltpu.VMEM((tm, d_mix), BF16)] + att_scratch,
        compiler_params=_params(1),
        name="mix_out",
    )(h, *att, ob, zb, oc, zc, b_norm, c_norm,
      jnp.asarray(expand, dtype=BF16), jnp.asarray(blockmean, dtype=BF16), w_out)


def _prep_w_in(w_in):
    sizes = (A_WIDTH, A_WIDTH, A_WIDTH, B_KW, B_KW, B_WIDTH, B_WIDTH, C_KW, C_KW, C_WIDTH, C_GATE_RANK, C_WIDTH)
    cuts = np.cumsum(sizes)[:-1].tolist()
    parts = jnp.split(w_in, cuts, axis=-1)
    clr = jnp.pad(parts[10], ((0, 0),) * (w_in.ndim - 1) + ((0, LANES - C_GATE_RANK),))
    return jnp.concatenate(parts[:10] + [parts[11], clr], axis=-1).astype(BF16)


def _state_to_pairs(s):
    nl, nb, nh, dk, dv = s.shape
    st = jnp.swapaxes(s, 3, 4)
    if dk < LANES:
        st = jnp.stack([jnp.pad(st[:, :, h], ((0, 0), (0, 0), (0, 0), (h * dk, LANES - (h + 1) * dk)))
                        for h in range(nh)], 2)
    return st.reshape(nl, nb, 2, 2 * dv, LANES)


def _pairs_to_state(st, dk):
    nb = st.shape[0]
    st = st.reshape(nb, 4, B_VAL_DIM, LANES)
    if dk < LANES:
        st = jnp.stack([st[:, h, :, h * dk:(h + 1) * dk] for h in range(4)], 1)
    return jnp.swapaxes(st, 2, 3)


def _trunk(x, p, wts, lb_all, final_norm, caches):
    nb, t, d = x.shape
    decode = caches is not None
    assert (t == 1) if decode else (nb == 1)
    m = nb * t
    h = x.reshape(m, d)
    w = wts
    depth = w["w_in"].shape[0]
    keep = m if decode else min(A_MAX_WINDOW, t)
    k_rows, v_rows, sb_out, sc_out = [], [], [], []
    y = None
    p = p.reshape(depth, m, -1)
    if decode:
        sb0, sc0 = _state_to_pairs(caches[2]), _state_to_pairs(caches[3])
    else:
        sb0 = sc0 = jnp.zeros((1, 1, 2, LANES, LANES), F32)
    for i in range(depth):
        h, = _ffn(h, i, w["ffn1_norm"], w["ffn1_w_gate"], w["ffn1_w_up"], w["ffn1_w_down"])
        if decode:
            qkv, kf, vf, zb, zc = _proj(h, i, w["mix_norm"], w["w_in"], keep, (), F32)
            pad = lambda a: jnp.pad(a.reshape(nb, 1, -1), ((0, 0), (0, DEC_PAD - 1), (0, 0)))
            cols = lambda a: jnp.swapaxes(a.reshape(nb, A_HEADS, A_HEAD_DIM), 1, 2)
            o_t = _attn_decode(cols(qkv[0, :, 0:A_WIDTH]), cols(kf), cols(vf), caches[0], caches[1], i)
            att, dils = [jnp.swapaxes(o_t, 1, 2).reshape(m, A_WIDTH)], ()
            zb3, zc3 = pad(zb), pad(zc)
            n_valid, s0_layer = 1, i
        else:
            dils = tuple(dil for _, dil in A_CONFIGS)
            assert dils[0] == 1
            *qkvs, kf, vf, zb, zc = _proj(h, i, w["mix_norm"], w["w_in"], keep, dils[1:], BF16)
            res = [_attn_prompt(a, dil) for a, dil in zip(qkvs, dils)]
            att = [r[0] for r in res] + [r[1] for r in res]
            zb3, zc3 = zb[None], zc[None]
            n_valid, s0_layer = None, 0
        rec = _recurrences(dict(b=zb3, c=zc3), i, dict(b=[lb_all], c=[w["c_w_gate"], w["c_gate_bias"]]),
                           dict(b=sb0, c=sc0), s0_layer, n_valid)
        (ob, sb), (oc, sc) = rec["b"], rec["c"]
        if decode:
            ob, oc = ob[:, 0], oc[:, 0]
        else:
            ob, oc = ob[0], oc[0]
        h = _mix(h, i, att, dils, ob, zb, oc, zc, w["b_out_norm"], w["c_out_norm"], w["w_out"])
        ple = (p, w["ple_norm"], w["ple_w_gate"], w["ple_w_proj"])
        if i == depth - 1:
            h, y = _ffn(h, i, w["ffn2_norm"], w["ffn2_w_gate"], w["ffn2_w_up"], w["ffn2_w_down"], ple, final_norm)
        else:
            h, = _ffn(h, i, w["ffn2_norm"], w["ffn2_w_gate"], w["ffn2_w_up"], w["ffn2_w_down"], ple)
        k_rows.append(kf.reshape(nb, keep // nb, A_HEADS, A_HEAD_DIM))
        v_rows.append(vf.reshape(nb, keep // nb, A_HEADS, A_HEAD_DIM))
        sb_out.append(_pairs_to_state(sb, B_KEY_DIM))
        sc_out.append(_pairs_to_state(sc, C_KEY_DIM))
    return (y.reshape(nb, t, d), jnp.stack(k_rows), jnp.stack(v_rows), jnp.stack(sb_out), jnp.stack(sc_out))


def kernel(x_prompt, x_sample, cache_k_a, cache_v_a, state_b, state_c, p_prompt, p_sample, ffn1_norm, ffn1_w_gate, ffn1_w_up, ffn1_w_down, mix_norm, w_in, lb_logits, b_out_norm, c_w_gate, c_gate_bias, c_out_norm, w_out, ffn2_norm, ffn2_w_gate, ffn2_w_up, ffn2_w_down, ple_norm, ple_w_gate, ple_w_proj, final_norm):
    depth = w_in.shape[0]
    rows = lambda a: a.reshape(depth, 1, -1)
    wts = {name: a.astype(BF16) for name, a in dict(
        ffn1_w_gate=ffn1_w_gate, ffn1_w_up=ffn1_w_up, ffn1_w_down=ffn1_w_down, w_out=w_out, ffn2_w_gate=ffn2_w_gate,
        ffn2_w_up=ffn2_w_up, ffn2_w_down=ffn2_w_down, ple_w_gate=ple_w_gate, ple_w_proj=ple_w_proj).items()}
    wts.update(
        ffn1_norm=rows(ffn1_norm), mix_norm=rows(mix_norm), w_in=_prep_w_in(w_in), b_out_norm=rows(b_out_norm),
        c_w_gate=jnp.pad(c_w_gate, ((0, 0), (0, LANES - C_GATE_RANK), (0, 0))).astype(BF16),
        c_gate_bias=rows(c_gate_bias), c_out_norm=rows(c_out_norm), ffn2_norm=rows(ffn2_norm), ple_norm=rows(ple_norm))
    sm = jax.nn.softmax(lb_logits.astype(F32), axis=0)
    lb_all = jnp.maximum(jnp.cumsum(sm, axis=0) - sm[0], 0.0).reshape(depth, 1, B_KW)
    fn = final_norm.reshape(1, -1)
    y_p, k_p, v_p, sb_p, sc_p = _trunk(x_prompt, p_prompt, wts, lb_all, fn, None)
    k_cache_t = jnp.transpose(cache_k_a, (0, 1, 3, 4, 2))
    v_cache_t = jnp.transpose(cache_v_a, (0, 1, 3, 4, 2))
    y_s, k_s, v_s, sb_s, sc_s = _trunk(x_sample, p_sample, wts, lb_all, fn, (k_cache_t, v_cache_t, state_b, state_c))
    return (y_p, y_s, k_p, v_p, sb_p, sc_p, k_s, v_s, sb_s, sc_s)
```

```python
import functools
import math

import numpy as np
import jax
import jax.numpy as jnp
from jax import lax
from jax.experimental import pallas as pl
from jax.experimental.pallas import tpu as pltpu

F32 = jnp.float32
BF16 = jnp.bfloat16

RMS_EPS = 1e-6
NEG_BIG = -1e30
MIN_F = 1e-30
A_HEADS = 8
A_HEAD_DIM = 64
A_CONFIGS = ((128, 1), (512, 4), (2048, 16))
A_MAX_WINDOW = 2048
A_SUB = 128
B_HEADS = 4
B_KEY_DIM = 128
B_VAL_DIM = 64
C_HEADS = 4
C_KEY_DIM = 32
C_VAL_DIM = 64
C_GATE_RANK = 16
C_GATE_TEMP = 16.0
A_WIDTH = A_HEADS * A_HEAD_DIM
B_KW = B_HEADS * B_KEY_DIM
B_WIDTH = B_HEADS * B_VAL_DIM
C_KW = C_HEADS * C_KEY_DIM
C_WIDTH = C_HEADS * C_VAL_DIM
ZB_WIDTH = 2 * B_KW + 2 * B_WIDTH
ZC_WIDTH = 2 * C_KW + 2 * C_WIDTH + 128
PROJ_PAD = 3 * A_WIDTH + ZB_WIDTH + ZC_WIDTH

LANES = 128
MXU_WIDTH = 256
VMEM_LIMIT_BYTES = 56 * 1024 * 1024

ROW_TILE = 512
FF_CHUNK = 1536
REC_CHUNK = 128
REC_CHUNKS_PER_STEP = 4
REC_SEQS_PER_STEP = 8
ATT_ROWS = 1024
DEC_PAD = 16


def _params(n_axes):
    return pltpu.CompilerParams(dimension_semantics=("arbitrary",) * n_axes,
                                vmem_limit_bytes=VMEM_LIMIT_BYTES)


def _const_spec(shape):
    zeros = (0,) * len(shape)
    return pl.BlockSpec(shape, lambda *_: zeros, pipeline_mode=pl.Buffered(1))


def _layer_spec(stacked, layer):
    shape = tuple(stacked.shape[1:])
    zeros = (0,) * len(shape)
    return pl.BlockSpec((None,) + shape, lambda *_: (layer,) + zeros, pipeline_mode=pl.Buffered(1))


def _dot(a, b):
    return jnp.dot(a, b, preferred_element_type=F32)


def _dot_nt(a, b):
    return lax.dot_general(a, b, (((1,), (1,)), ((), ())), preferred_element_type=F32)


def _dot_tn(a, b):
    return lax.dot_general(a, b, (((0,), (0,)), ((), ())), preferred_element_type=F32)


def _dot_split2(x, w):
    hi = x.astype(BF16)
    lo = (x - hi.astype(F32)).astype(BF16)
    return _dot(hi, w) + _dot(lo, w)


def _rms(x, g):
    ms = jnp.mean(x * x, axis=-1, keepdims=True)
    return x * lax.rsqrt(ms + RMS_EPS) * g


def _sigmoid(x):
    return 1.0 / (1.0 + jnp.exp(-x))


def _ff_chunks(dff):
    unit = MXU_WIDTH if dff % MXU_WIDTH == 0 else LANES
    assert dff % unit == 0
    tiles = dff // unit
    n = -(-dff // FF_CHUNK)
    sizes = [(tiles // n + (1 if i < tiles % n else 0)) * unit for i in range(n)]
    starts = np.cumsum([0] + sizes[:-1]).tolist()
    return tuple(zip(starts, sizes))


def _ffn_body(*refs, chunks, with_ple, with_final):
    refs = list(refs)
    h_ref, nrm_ref, wg_ref, wu_ref, wd_ref = refs[:5]
    pos = 5
    if with_ple:
        p_ref, pn_ref, pwg_ref, pwp_ref = refs[pos:pos + 4]
        pos += 4
    if with_final:
        fn_ref = refs[pos]
        pos += 1
    o_ref = refs[pos]
    h = h_ref[...]
    xn = _rms(h, nrm_ref[...]).astype(BF16)
    acc = None
    for lo, chunk in chunks:
        g = _dot(xn, wg_ref[:, lo:lo + chunk])
        u = _dot(xn, wu_ref[:, lo:lo + chunk])
        a = (g * _sigmoid(g) * u).astype(BF16)
        d = _dot(a, wd_ref[lo:lo + chunk, :])
        acc = d if acc is None else acc + d
    h = h + 0.5 * acc
    if with_ple:
        hn = _rms(h, pn_ref[...]).astype(BF16)
        gate = _sigmoid(_dot(hn, pwg_ref[...]))
        proj = _dot(p_ref[...].astype(BF16), pwp_ref[...])
        h = h + gate * proj
    o_ref[...] = h
    if with_final:
        refs[pos + 1][...] = _rms(h, fn_ref[...])


def _ffn(h, layer, nrm, wg, wu, wd, ple=None, final_norm=None):
    m, d = h.shape
    dff = wg.shape[-1]
    tm = min(ROW_TILE, m)
    assert m % tm == 0
    row = lambda w: pl.BlockSpec((tm, w), lambda i: (i, 0))
    ins = [h, nrm, wg, wu, wd]
    specs = [row(d)] + [_layer_spec(a, layer) for a in ins[1:]]
    if ple is not None:
        ins += list(ple)
        specs += [pl.BlockSpec((None, tm, ple[0].shape[-1]), lambda i: (layer, i, 0))]
        specs += [_layer_spec(a, layer) for a in ple[1:]]
    if final_norm is not None:
        ins.append(final_norm)
        specs.append(_const_spec((1, d)))
    n_out = 2 if final_norm is not None else 1
    out = pl.pallas_call(
        functools.partial(_ffn_body, chunks=_ff_chunks(dff), with_ple=ple is not None,
                          with_final=final_norm is not None),
        grid=(m // tm,),
        in_specs=specs,
        out_specs=[row(d)] * n_out,
        out_shape=[jax.ShapeDtypeStruct((m, d), F32)] * n_out,
        compiler_params=_params(1),
        name="ffn",
    )(*ins)
    return out


def _proj_body(h_ref, nrm_ref, w_ref, *refs, tm, dils):
    n_a = 1 + len(dils)
    a_refs = refs[:n_a]
    kf_ref, vf_ref, zb_ref, zc_ref = refs[n_a:n_a + 4]
    xn = _rms(h_ref[...], nrm_ref[...]).astype(BF16)
    a = A_WIDTH
    b0 = 3 * a
    for part in range(3):
        cols = slice(part * a, (part + 1) * a)
        z = _dot(xn, w_ref[:, cols])
        if part == 0:
            z = z * (A_HEAD_DIM ** -0.5)
        elif part == 1:
            kf_ref[...] = z
        else:
            vf_ref[...] = z
        a_refs[0][0, :, cols] = z.astype(a_refs[0].dtype)
        if dils:
            stage = refs[-1]
            for cb in range(a // LANES):
                stage[cb] = z[:, cb * LANES:(cb + 1) * LANES]
        for ref, d in zip(a_refs[1:], dils):
            for r in range(d):
                for cb in range(a // LANES):
                    lo = part * a + cb * LANES
                    ref[r, :, lo:lo + LANES] = stage[cb, pl.ds(r, tm // d, stride=d), :].astype(ref.dtype)
    zb_ref[...] = _dot(xn, w_ref[:, b0:b0 + ZB_WIDTH])
    c0 = b0 + ZB_WIDTH
    zc_ref[:, 0:C_KW] = _dot(xn, w_ref[:, c0:c0 + C_KW]) * (C_KEY_DIM ** -0.5)
    zc_ref[:, C_KW:] = _dot(xn, w_ref[:, c0 + C_KW:c0 + ZC_WIDTH])


def _proj(h, layer, nrm, w, keep, dils, a_dtype):
    m, d = h.shape
    tm = min(ROW_TILE, m)
    assert m % tm == 0 and keep % tm == 0
    nt, nk = m // tm, keep // tm
    row = lambda wd: pl.BlockSpec((tm, wd), lambda i: (i, 0))
    tail = pl.BlockSpec((tm, A_WIDTH), lambda i: (jnp.maximum(i - (nt - nk), 0), 0))
    sds = jax.ShapeDtypeStruct
    all_d = (1,) + tuple(dils)
    return pl.pallas_call(
        functools.partial(_proj_body, tm=tm, dils=tuple(dils)),
        grid=(nt,),
        in_specs=[row(d), _layer_spec(nrm, layer), _layer_spec(w, layer)],
        out_specs=[pl.BlockSpec((dd, tm // dd, 3 * A_WIDTH), lambda i: (0, i, 0)) for dd in all_d]
                  + [tail, tail, row(ZB_WIDTH), row(ZC_WIDTH)],
        out_shape=[sds((dd, m // dd, 3 * A_WIDTH), a_dtype) for dd in all_d]
                  + [sds((keep, A_WIDTH), F32), sds((keep, A_WIDTH), F32),
                     sds((m, ZB_WIDTH), F32), sds((m, ZC_WIDTH), F32)],
        scratch_shapes=[pltpu.VMEM((A_WIDTH // LANES, tm, LANES), F32)] if dils else [],
        compiler_params=_params(1),
        name="proj",
    )(h, nrm, w)


def _alibi_slopes():
    return (2.0 ** (-8.0 * np.arange(1, A_HEADS + 1, dtype=np.float32) / A_HEADS)).astype(np.float32)


def _attn_bias(dil):
    qi = np.arange(A_SUB)[:, None]
    ki = np.arange(2 * A_SUB)[None, :]
    j = qi + A_SUB - ki
    band = (j >= 0) & (j <= A_SUB)
    bias = -_alibi_slopes()[:, None, None] * (j * dil).astype(np.float32)[None]
    return jnp.asarray(np.where(band[None], bias, np.float32(NEG_BIG)).astype(np.float32))


def _attn_body(q_ref, kp_ref, kc_ref, vp_ref, vc_ref, bias_ref, o_ref, lse_ref, kk, vv, *, n_sub):
    qb = A_SUB
    kk[0:A_SUB, :] = kp_ref[...]
    vv[0:A_SUB, :] = vp_ref[...]
    kk[A_SUB:, :] = kc_ref[...]
    vv[A_SUB:, :] = vc_ref[...]
    lane = lax.broadcasted_iota(jnp.int32, (qb, LANES), 1)
    low = lane < A_HEAD_DIM
    low_f = low.astype(F32)
    head_lanes = (low_f.astype(BF16), (1.0 - low_f).astype(BF16))
    first = jnp.where(pl.program_id(1) > 0, 0, A_SUB)
    key_ok = lax.broadcasted_iota(jnp.int32, (qb, 2 * A_SUB), 1) >= first
    for i in range(n_sub):
        rows = slice(i * qb, (i + 1) * qb)
        krows = slice(i * qb, i * qb + 2 * A_SUB)
        lse_tile = jnp.zeros((qb, LANES), F32)
        for hp in range(A_HEADS // 2):
            cols = slice(hp * LANES, (hp + 1) * LANES)
            q2 = q_ref[rows, cols]
            k2 = kk[krows, cols]
            v2 = vv[krows, cols]
            outs = []
            for hh in range(2):
                h = 2 * hp + hh
                s = _dot_nt(q2 * head_lanes[hh], k2) + bias_ref[h]
                if i == 0:
                    s = jnp.where(key_ok, s, NEG_BIG)
                mx = jnp.max(s, axis=-1, keepdims=True)
                p = jnp.exp(s - mx)
                den = jnp.sum(p, axis=-1, keepdims=True)
                outs.append(_dot(p.astype(BF16), v2) / den)
                lse_tile = jnp.where(lane == h, mx + jnp.log(den), lse_tile)
            o_ref[rows, cols] = jnp.where(low, outs[0], outs[1]).astype(o_ref.dtype)
        lse_ref[rows, :] = lse_tile


def _attn_prompt(qkv, dil):
    _, tv, _ = qkv.shape
    assert qkv.shape[0] == dil and tv % A_SUB == 0
    bq = min(ATT_ROWS, tv)
    assert tv % bq == 0 and bq % A_SUB == 0
    cur = lambda part: pl.BlockSpec((None, bq, A_WIDTH), lambda r, n: (r, n, part))
    prev = lambda part: pl.BlockSpec((None, A_SUB, A_WIDTH),
                                     lambda r, n: (r, jnp.maximum(n * (bq // A_SUB) - 1, 0), part))
    return pl.pallas_call(
        functools.partial(_attn_body, n_sub=bq // A_SUB),
        grid=(dil, tv // bq),
        in_specs=[cur(0), prev(1), cur(1), prev(2), cur(2), _const_spec((A_HEADS, A_SUB, 2 * A_SUB))],
        out_specs=[cur(0), pl.BlockSpec((None, bq, LANES), lambda r, n: (r, n, 0))],
        out_shape=[jax.ShapeDtypeStruct((dil, tv, A_WIDTH), BF16), jax.ShapeDtypeStruct((dil, tv, LANES), F32)],
        scratch_shapes=[pltpu.VMEM((A_SUB + bq, A_WIDTH), BF16), pltpu.VMEM((A_SUB + bq, A_WIDTH), BF16)],
        compiler_params=_params(2),
        name=f"attn_prompt_d{dil}",
    )(qkv, qkv, qkv, qkv, qkv, _attn_bias(dil))


def _decode_tables(n_past):
    dist = n_past - np.arange(n_past)
    mult = np.zeros(n_past, np.float32)
    for window, dil in A_CONFIGS:
        mult += ((dist % dil == 0) & (dist // dil <= window // dil)).astype(np.float32)
    bias = -_alibi_slopes()[:, None] * dist.astype(np.float32)[None, :]
    return jnp.asarray(bias), jnp.asarray(mult[None, :])


def _attn_decode_body(q_ref, kn_ref, vn_ref, kc_ref, vc_ref, bias_ref, mult_ref, o_ref):
    q_t = q_ref[...]
    s_new = jnp.sum(q_t * kn_ref[...], axis=0, keepdims=True)
    mult = mult_ref[...]
    live = mult > 0.0
    n_cfg = float(len(A_CONFIGS))
    lane = lax.broadcasted_iota(jnp.int32, q_t.shape, 1)
    out = jnp.zeros(q_t.shape, F32)
    for h in range(A_HEADS):
        s = jnp.sum(kc_ref[h] * q_t[:, h:h + 1], axis=0, keepdims=True) + bias_ref[h:h + 1, :]
        s = jnp.where(live, s, NEG_BIG)
        s0 = s_new[:, h:h + 1]
        mx = jnp.maximum(jnp.max(s, axis=-1, keepdims=True), s0)
        p = jnp.exp(s - mx) * mult
        p0 = jnp.exp(s0 - mx) * n_cfg
        den = jnp.sum(p, axis=-1, keepdims=True) + p0
        acc = jnp.sum(vc_ref[h] * p, axis=-1, keepdims=True) + p0 * vn_ref[:, h:h + 1]
        out = jnp.where(lane == h, acc / den, out)
    o_ref[...] = out


def _attn_decode(q_t, kn_t, vn_t, k_cache_t, v_cache_t, layer):
    nb = q_t.shape[0]
    n_past = k_cache_t.shape[-1]
    assert n_past == A_MAX_WINDOW
    bias, mult = _decode_tables(n_past)
    new = pl.BlockSpec((None, A_HEAD_DIM, A_HEADS), lambda b: (b, 0, 0))
    past = pl.BlockSpec((None, None, A_HEADS, A_HEAD_DIM, n_past), lambda b: (layer, b, 0, 0, 0))
    return pl.pallas_call(
        _attn_decode_body,
        grid=(nb,),
        in_specs=[new, new, new, past, past, _const_spec(bias.shape), _const_spec(mult.shape)],
        out_specs=new,
        out_shape=jax.ShapeDtypeStruct((nb, A_HEAD_DIM, A_HEADS), F32),
        compiler_params=_params(1),
        name="attn_decode",
    )(q_t, kn_t, vn_t, k_cache_t, v_cache_t, bias, mult)


def _rec_tables(c, single, stacked_heads):
    nl = 0 if single else int(math.log2(c))
    assert single or 2 ** nl == c
    t = np.arange(c)[:, None]
    i = np.arange(c)[None, :]
    e_rows, masks = [], []
    for lvl in range(nl):
        half = c >> (lvl + 1)
        mid = (t // (2 * half)) * (2 * half) + half
        right = t >= mid
        e_rows.append(np.where(right, (i >= mid) & (i <= t), (i > t) & (i < mid)))
        same = (t // (2 * half)) == (i // (2 * half))
        masks.append(same & right & (i < mid))
    masks.append(t == i)
    e_rows.append(i <= t)
    e_rows.append(i > t)
    e = np.concatenate(e_rows, 0).astype(np.float32)
    m = np.stack(masks, 0).astype(np.float32)
    if stacked_heads > 1:
        m = np.tile(m, (1, stacked_heads, 1))
    return jnp.asarray(np.concatenate([e, e], 1), dtype=BF16), jnp.asarray(m)


def _rec_core(q, k, lf, v, e_ref, m_ref, st_ref, o_ref, *, c, kind, single):
    nl = 0 if single else int(math.log2(c))
    n_win = q.shape[1] // LANES
    win = lambda x, w: x[:, w * LANES:(w + 1) * LANES]
    if single:
        e_cum = jnp.exp(lf)
        ke_f = k
    else:
        l1 = lf.astype(BF16)
        l2 = (lf - l1.astype(F32)).astype(BF16)
        decay = jnp.exp(_dot(e_ref[...], jnp.concatenate([l1, l2], axis=0)))
        seg = lambda n: decay[n * c:(n + 1) * c]
        e_cum = seg(nl)
        ke_f = k * seg(nl + 1)
    yield
    row = lax.broadcasted_iota(jnp.int32, (c, LANES), 0)

    def level_operand(n, w):
        half = c >> (n + 1)
        qw, kw = win(q, w), win(k, w)
        if half >= 8:
            pieces = [(qw if j % 2 else kw)[j * half:(j + 1) * half] for j in range(c // half)]
            side = jnp.concatenate(pieces, axis=0)
        else:
            side = jnp.where((row & half) != 0, qw, kw)
        return (side * win(seg(n), w)).astype(BF16)

    u = [[level_operand(n, w) for w in range(n_win)] for n in range(nl)]
    uq = [win(q, w).astype(BF16) for w in range(n_win)]
    uk = [win(k, w).astype(BF16) for w in range(n_win)]
    qg = (q * e_cum).astype(BF16)
    ke = ke_f.astype(BF16)
    e_tot = e_cum[0:1, :] if single else e_cum[c - 1:c, :]
    vb = v.astype(BF16)
    yield
    lane = lax.broadcasted_iota(jnp.int32, (c, LANES), 1)
    low = lane < B_VAL_DIM
    if kind == "c":
        head = lax.broadcasted_iota(jnp.int32, (1, LANES), 1) // C_KEY_DIM
        hm = [(head == h).astype(F32).astype(BF16) for h in range(C_HEADS)]
        stack = lambda x: jnp.concatenate([x * hm[h] for h in range(C_HEADS)], axis=0)
        att_all = _dot_nt(stack(uq[0]), uk[0]) * m_ref[nl]
        for n in range(nl):
            att_all = att_all + _dot_nt(stack(u[n][0]), u[n][0]) * m_ref[n]
        att_all = att_all.astype(BF16)
        atts = [att_all[h * c:(h + 1) * c] for h in range(C_HEADS)]
    else:
        atts = []
        for h in range(B_HEADS):
            att = _dot_nt(uq[h], uk[h]) * m_ref[nl]
            for n in range(nl):
                att = att + _dot_nt(u[n][h], u[n][h]) * m_ref[n]
            atts.append(att.astype(BF16))
    intra = [jnp.where(low, _dot(atts[2 * pair], win(vb, pair)), _dot(atts[2 * pair + 1], win(vb, pair)))
             for pair in range(2)]
    half_rows = [slice(hh * B_VAL_DIM, (hh + 1) * B_VAL_DIM) for hh in range(2)]
    if kind == "c":
        srow = lax.broadcasted_iota(jnp.int32, (LANES, LANES), 0) // B_VAL_DIM
        slane = lax.broadcasted_iota(jnp.int32, (LANES, LANES), 1) // C_KEY_DIM
        added = [_dot_tn(win(vb, pair), ke) * (slane == srow + 2 * pair).astype(F32) for pair in range(2)]
    else:
        added = [jnp.concatenate([_dot_tn(win(vb, pair), win(ke, 2 * pair + hh))[half_rows[hh], :]
                                  for hh in range(2)], axis=0) for pair in range(2)]
    yield
    for pair in range(2):
        st = st_ref[pair]
        stb = st.astype(BF16)
        if kind == "c":
            o_ref[:, pair * LANES:(pair + 1) * LANES] = intra[pair] + _dot_nt(qg, stb)
            st_ref[pair] = st * e_tot + added[pair]
            continue
        carried = [_dot_nt(win(qg, 2 * pair + hh), stb) for hh in range(2)]
        o_ref[:, pair * LANES:(pair + 1) * LANES] = intra[pair] + jnp.where(low, carried[0], carried[1])
        keep = jnp.concatenate([st[half_rows[hh], :] * win(e_tot, 2 * pair + hh) for hh in range(2)], axis=0)
        st_ref[pair] = keep + added[pair]


_REC_STAGES = 4
_REC_KINDS = ("b", "c")
_REC_N_PARAMS = {"b": 1, "c": 2}


def _rec_body(*refs, c, n_valid, n_b, n_sub):
    ins, pos = {}, 0
    for kind in _REC_KINDS:
        n = 4 + _REC_N_PARAMS[kind]
        ins[kind] = refs[pos:pos + n]
        pos += n
    outs = {kind: refs[pos + 2 * i:pos + 2 * i + 2] for i, kind in enumerate(_REC_KINDS)}
    pos += 2 * len(_REC_KINDS)
    states = {kind: refs[pos + i] for i, kind in enumerate(_REC_KINDS)}

    @pl.when(pl.program_id(1) == 0)
    def _():
        for kind in _REC_KINDS:
            states[kind][...] = ins[kind][-1][...]

    chunks = []
    for bb in range(n_b):
        for j in range(n_sub):
            for kind in _REC_KINDS:
                z_ref, *params, e_ref, m_ref, _ = ins[kind]
                chunks.append(_rec_chunk(z_ref.at[bb, pl.ds(j * c, c)], params, e_ref, m_ref, states[kind].at[bb],
                                         outs[kind][0].at[bb, pl.ds(j * c, c)], kind=kind, c=c, n_valid=n_valid))
    for _ in range(_REC_STAGES - 1):
        for chunk in chunks:
            next(chunk)
    for chunk in chunks:
        for _ in chunk:
            raise AssertionError("more stages than _REC_STAGES")

    @pl.when(pl.program_id(1) == pl.num_programs(1) - 1)
    def _():
        for kind in _REC_KINDS:
            outs[kind][1][...] = states[kind][...]


def _rec_chunk(z_ref, params, e_ref, m_ref, st_ref, o_ref, *, kind, c, n_valid):
    z = z_ref[...]
    if kind == "b":
        lb = params[0][...]
        f = lb + (1.0 - lb) * _sigmoid(z[:, B_KW:2 * B_KW])
        lf = jnp.log(jnp.maximum(f, MIN_F))
        k = 1.0 - f
        zq = z[:, 0:B_KW]
        q = zq * _sigmoid(zq)
        v = z[:, 2 * B_KW:2 * B_KW + B_WIDTH]
    else:
        cwg_ref, cb_ref = params
        q = z[:, 0:C_KW]
        k = z[:, C_KW:2 * C_KW]
        v = z[:, 2 * C_KW:2 * C_KW + C_WIDTH]
        x = _dot(z[:, ZC_WIDTH - LANES:].astype(BF16), cwg_ref[...]) + cb_ref[...]
        lf = (jnp.minimum(x, 0.0) - jnp.log1p(jnp.exp(-jnp.abs(x)))) / C_GATE_TEMP
    if n_valid < c:
        live = lax.broadcasted_iota(jnp.int32, (c, 1), 0) < n_valid
        q = jnp.where(live, q, 0.0)
        k = jnp.where(live, k, 0.0)
        lf = jnp.where(live, lf, 0.0)
    yield from _rec_core(q, k, lf, v, e_ref, m_ref, st_ref, o_ref, c=c, kind=kind, single=n_valid == 1)


def _recurrences(zs, layer, params, s0s, s0_layer, n_valid=None):
    nb, t, _ = zs["b"].shape
    c = min(REC_CHUNK, t)
    assert t % c == 0
    n_valid = c if n_valid is None else n_valid
    assert n_valid in (1, c)
    n_sub = math.gcd(t // c, REC_CHUNKS_PER_STEP)
    n_b = math.gcd(nb, REC_SEQS_PER_STEP)
    st_spec = pl.BlockSpec((n_b, 2, LANES, LANES), lambda b, j: (b, 0, 0, 0))
    s0_spec = pl.BlockSpec((None, n_b, 2, LANES, LANES), lambda b, j: (s0_layer, b, 0, 0, 0))
    args, in_specs = [], []
    for kind in _REC_KINDS:
        assert len(params[kind]) == _REC_N_PARAMS[kind] and zs[kind].shape[:2] == (nb, t)
        e, m = _rec_tables(c, n_valid == 1, C_HEADS if kind == "c" else 1)
        args += [zs[kind], *params[kind], e, m, s0s[kind]]
        in_specs += [pl.BlockSpec((n_b, n_sub * c, zs[kind].shape[2]), lambda b, j: (b, j, 0))]
        in_specs += [_layer_spec(x, layer) for x in params[kind]]
        in_specs += [_const_spec(e.shape), _const_spec(m.shape), s0_spec]
    n_kinds = len(_REC_KINDS)
    res = pl.pallas_call(
        functools.partial(_rec_body, c=c, n_valid=n_valid, n_b=n_b, n_sub=n_sub),
        grid=(nb // n_b, t // (c * n_sub)),
        in_specs=in_specs,
        out_specs=[pl.BlockSpec((n_b, n_sub * c, 2 * LANES), lambda b, j: (b, j, 0)), st_spec] * n_kinds,
        out_shape=[jax.ShapeDtypeStruct((nb, t, 2 * LANES), F32),
                   jax.ShapeDtypeStruct((nb, 2, LANES, LANES), F32)] * n_kinds,
        scratch_shapes=[pltpu.VMEM((n_b, 2, LANES, LANES), F32)] * n_kinds,
        compiler_params=_params(2),
        name="recurrences",
    )(*args)
    return {kind: (res[2 * i], res[2 * i + 1]) for i, kind in enumerate(_REC_KINDS)}


def _mix_body(*refs, tm, dils):
    h_ref = refs[0]
    n_att = 2 * len(dils) if dils else 1
    att_refs = refs[1:1 + n_att]
    ob_ref, bg_ref, oc_ref, cg_ref, bn_ref, cn_ref, ex_ref, bd_ref, w_ref, out_ref, mixed = refs[1 + n_att:12 + n_att]
    if dils:
        o_refs, l_refs = att_refs[:len(dils)], att_refs[len(dils):]
        o_nat, l_nat = refs[12 + n_att], refs[13 + n_att]
        n_cb = A_WIDTH // LANES
        lses = []
        for c, d in enumerate(dils):
            if d == 1:
                lses.append(l_refs[c][0])
                continue
            for r in range(d):
                l_nat[c, pl.ds(r, tm // d, stride=d), :] = l_refs[c][r]
                for cb in range(n_cb):
                    o_nat[c, cb, pl.ds(r, tm // d, stride=d), :] = (
                        o_refs[c][r, :, cb * LANES:(cb + 1) * LANES].astype(F32))
            lses.append(l_nat[c])
        mx = functools.reduce(jnp.maximum, lses)
        es = [jnp.exp(l - mx) for l in lses]
        den = functools.reduce(lambda a, b: a + b, es)
        ex = ex_ref[...]
        wts = [_dot_split2(e / den, ex) for e in es]
        for cb in range(n_cb):
            cs = slice(cb * LANES, (cb + 1) * LANES)
            terms = [wts[c][:, cs] * (o_refs[c][0, :, cs] if d == 1 else o_nat[c, cb]) for c, d in enumerate(dils)]
            mixed[:, cs] = functools.reduce(lambda a, b: a + b, terms).astype(BF16)
    else:
        mixed[:, 0:A_WIDTH] = att_refs[0][...].astype(BF16)
    bd = bd_ref[...]
    ob = ob_ref[...]
    obn = ob * lax.rsqrt(_dot_split2(ob * ob, bd) + RMS_EPS) * bn_ref[...]
    mixed[:, A_WIDTH:A_WIDTH + B_WIDTH] = (obn * _sigmoid(bg_ref[...])).astype(BF16)
    oc = oc_ref[...]
    ocn = oc * lax.rsqrt(_dot_split2(oc * oc, bd) + RMS_EPS) * cn_ref[...]
    cg = cg_ref[...]
    mixed[:, A_WIDTH + B_WIDTH:] = (ocn * (cg * _sigmoid(cg))).astype(BF16)
    out_ref[...] = h_ref[...] + _dot(mixed[...], w_ref[...])


def _mix(h, layer, att, dils, ob, zb, oc, zc, b_norm, c_norm, w_out):
    m, d = h.shape
    tm = min(ROW_TILE, m)
    assert m % tm == 0
    row = lambda w, blk=0: pl.BlockSpec((tm, w), lambda i: (i, blk))
    if dils:
        att_specs = [pl.BlockSpec((dd, tm // dd, wd), lambda i: (0, i, 0))
                     for wd in (A_WIDTH, LANES) for dd in dils]
        att_scratch = [pltpu.VMEM((len(dils), A_WIDTH // LANES, tm, LANES), F32),
                       pltpu.VMEM((len(dils), tm, LANES), F32)]
    else:
        att_specs, att_scratch = [row(A_WIDTH)], []
    expand = np.zeros((LANES, A_WIDTH), np.float32)
    for hd in range(A_HEADS):
        expand[hd, hd * A_HEAD_DIM:(hd + 1) * A_HEAD_DIM] = 1.0
    grp = np.arange(B_WIDTH) // B_VAL_DIM
    blockmean = (grp[:, None] == grp[None, :]).astype(np.float32) / B_VAL_DIM
    d_mix = A_WIDTH + B_WIDTH + C_WIDTH
    return pl.pallas_call(
        functools.partial(_mix_body, tm=tm, dils=tuple(dils)),
        grid=(m // tm,),
        in_specs=[row(d)] + att_specs
                 + [row(B_WIDTH), row(B_WIDTH, (2 * B_KW + B_WIDTH) // B_WIDTH),
                    row(C_WIDTH), row(C_WIDTH, (2 * C_KW + C_WIDTH) // C_WIDTH)]
                 + [_layer_spec(b_norm, layer), _layer_spec(c_norm, layer), _const_spec((LANES, A_WIDTH)),
                    _const_spec((B_WIDTH, B_WIDTH)), _layer_spec(w_out, layer)],
        out_specs=row(d),
        out_shape=jax.ShapeDtypeStruct((m, d), F32),
        scratch_shapes=[pltpu.VMEM((tm, d_mix), BF16)] + att_scratch,
        compiler_params=_params(1),
        name="mix_out",
    )(h, *att, ob, zb, oc, zc, b_norm, c_norm,
      jnp.asarray(expand, dtype=BF16), jnp.asarray(blockmean, dtype=BF16), w_out)


def _prep_w_in(w_in):
    sizes = (A_WIDTH, A_WIDTH, A_WIDTH, B_KW, B_KW, B_WIDTH, B_WIDTH, C_KW, C_KW, C_WIDTH, C_GATE_RANK, C_WIDTH)
    cuts = np.cumsum(sizes)[:-1].tolist()
    parts = jnp.split(w_in, cuts, axis=-1)
    clr = jnp.pad(parts[10], ((0, 0),) * (w_in.ndim - 1) + ((0, LANES - C_GATE_RANK),))
    return jnp.concatenate(parts[:10] + [parts[11], clr], axis=-1).astype(BF16)


def _state_to_pairs(s):
    nl, nb, nh, dk, dv = s.shape
    st = jnp.swapaxes(s, 3, 4)
    if dk < LANES:
        st = jnp.stack([jnp.pad(st[:, :, h], ((0, 0), (0, 0), (0, 0), (h * dk, LANES - (h + 1) * dk)))
                        for h in range(nh)], 2)
    return st.reshape(nl, nb, 2, 2 * dv, LANES)


def _pairs_to_state(st, dk):
    nb = st.shape[0]
    st = st.reshape(nb, 4, B_VAL_DIM, LANES)
    if dk < LANES:
        st = jnp.stack([st[:, h, :, h * dk:(h + 1) * dk] for h in range(4)], 1)
    return jnp.swapaxes(st, 2, 3)


def _trunk(x, p, wts, lb_all, final_norm, caches):
    nb, t, d = x.shape
    decode = caches is not None
    assert (t == 1) if decode else (nb == 1)
    m = nb * t
    h = x.reshape(m, d)
    w = wts
    depth = w["w_in"].shape[0]
    keep = m if decode else min(A_MAX_WINDOW, t)
    k_rows, v_rows, sb_out, sc_out = [], [], [], []
    y = None
    p = p.reshape(depth, m, -1)
    if decode:
        sb0, sc0 = _state_to_pairs(caches[2]), _state_to_pairs(caches[3])
    else:
        sb0 = sc0 = jnp.zeros((1, 1, 2, LANES, LANES), F32)
    for i in range(depth):
        h, = _ffn(h, i, w["ffn1_norm"], w["ffn1_w_gate"], w["ffn1_w_up"], w["ffn1_w_down"])
        if decode:
            qkv, kf, vf, zb, zc = _proj(h, i, w["mix_norm"], w["w_in"], keep, (), F32)
            pad = lambda a: jnp.pad(a.reshape(nb, 1, -1), ((0, 0), (0, DEC_PAD - 1), (0, 0)))
            cols = lambda a: jnp.swapaxes(a.reshape(nb, A_HEADS, A_HEAD_DIM), 1, 2)
            o_t = _attn_decode(cols(qkv[0, :, 0:A_WIDTH]), cols(kf), cols(vf), caches[0], caches[1], i)
            att, dils = [jnp.swapaxes(o_t, 1, 2).reshape(m, A_WIDTH)], ()
            zb3, zc3 = pad(zb), pad(zc)
            n_valid, s0_layer = 1, i
        else:
            dils = tuple(dil for _, dil in A_CONFIGS)
            assert dils[0] == 1
            *qkvs, kf, vf, zb, zc = _proj(h, i, w["mix_norm"], w["w_in"], keep, dils[1:], BF16)
            res = [_attn_prompt(a, dil) for a, dil in zip(qkvs, dils)]
            att = [r[0] for r in res] + [r[1] for r in res]
            zb3, zc3 = zb[None], zc[None]
            n_valid, s0_layer = None, 0
        rec = _recurrences(dict(b=zb3, c=zc3), i, dict(b=[lb_all], c=[w["c_w_gate"], w["c_gate_bias"]]),
                           dict(b=sb0, c=sc0), s0_layer, n_valid)
        (ob, sb), (oc, sc) = rec["b"], rec["c"]
        if decode:
            ob, oc = ob[:, 0], oc[:, 0]
        else:
            ob, oc = ob[0], oc[0]
        h = _mix(h, i, att, dils, ob, zb, oc, zc, w["b_out_norm"], w["c_out_norm"], w["w_out"])
        ple = (p, w["ple_norm"], w["ple_w_gate"], w["ple_w_proj"])
        if i == depth - 1:
            h, y = _ffn(h, i, w["ffn2_norm"], w["ffn2_w_gate"], w["ffn2_w_up"], w["ffn2_w_down"], ple, final_norm)
        else:
            h, = _ffn(h, i, w["ffn2_norm"], w["ffn2_w_gate"], w["ffn2_w_up"], w["ffn2_w_down"], ple)
        k_rows.append(kf.reshape(nb, keep // nb, A_HEADS, A_HEAD_DIM))
        v_rows.append(vf.reshape(nb, keep // nb, A_HEADS, A_HEAD_DIM))
        sb_out.append(_pairs_to_state(sb, B_KEY_DIM))
        sc_out.append(_pairs_to_state(sc, C_KEY_DIM))
    return (y.reshape(nb, t, d), jnp.stack(k_rows), jnp.stack(v_rows), jnp.stack(sb_out), jnp.stack(sc_out))


def kernel(x_prompt, x_sample, cache_k_a, cache_v_a, state_b, state_c, p_prompt, p_sample, ffn1_norm, ffn1_w_gate, ffn1_w_up, ffn1_w_down, mix_norm, w_in, lb_logits, b_out_norm, c_w_gate, c_gate_bias, c_out_norm, w_out, ffn2_norm, ffn2_w_gate, ffn2_w_up, ffn2_w_down, ple_norm, ple_w_gate, ple_w_proj, final_norm):
    depth = w_in.shape[0]
    rows = lambda a: a.reshape(depth, 1, -1)
    wts = {name: a.astype(BF16) for name, a in dict(
        ffn1_w_gate=ffn1_w_gate, ffn1_w_up=ffn1_w_up, ffn1_w_down=ffn1_w_down, w_out=w_out, ffn2_w_gate=ffn2_w_gate,
        ffn2_w_up=ffn2_w_up, ffn2_w_down=ffn2_w_down, ple_w_gate=ple_w_gate, ple_w_proj=ple_w_proj).items()}
    wts.update(
        ffn1_norm=rows(ffn1_norm), mix_norm=rows(mix_norm), w_in=_prep_w_in(w_in), b_out_norm=rows(b_out_norm),
        c_w_gate=jnp.pad(c_w_gate, ((0, 0), (0, LANES - C_GATE_RANK), (0, 0))).astype(BF16),
        c_gate_bias=rows(c_gate_bias), c_out_norm=rows(c_out_norm), ffn2_norm=rows(ffn2_norm), ple_norm=rows(ple_norm))
    sm = jax.nn.softmax(lb_logits.astype(F32), axis=0)
    lb_all = jnp.maximum(jnp.cumsum(sm, axis=0) - sm[0], 0.0).reshape(depth, 1, B_KW)
    fn = final_norm.reshape(1, -1)
    y_p, k_p, v_p, sb_p, sc_p = _trunk(x_prompt, p_prompt, wts, lb_all, fn, None)
    k_cache_t = jnp.transpose(cache_k_a, (0, 1, 3, 4, 2))
    v_cache_t = jnp.transpose(cache_v_a, (0, 1, 3, 4, 2))
    y_s, k_s, v_s, sb_s, sc_s = _trunk(x_sample, p_sample, wts, lb_all, fn, (k_cache_t, v_cache_t, state_b, state_c))
    return (y_p, y_s, k_p, v_p, sb_p, sc_p, k_s, v_s, sb_s, sc_s)
```

```python
import functools
import math

import numpy as np
import jax
import jax.numpy as jnp
from jax import lax
from jax.experimental import pallas as pl
from jax.experimental.pallas import tpu as pltpu

F32 = jnp.float32
BF16 = jnp.bfloat16

RMS_EPS = 1e-6
NEG_BIG = -1e30
MIN_F = 1e-30
A_HEADS = 8
A_HEAD_DIM = 64
A_CONFIGS = ((128, 1), (512, 4), (2048, 16))
A_MAX_WINDOW = 2048
A_SUB = 128
B_HEADS = 4
B_KEY_DIM = 128
B_VAL_DIM = 64
C_HEADS = 4
C_KEY_DIM = 32
C_VAL_DIM = 64
C_GATE_RANK = 16
C_GATE_TEMP = 16.0
A_WIDTH = A_HEADS * A_HEAD_DIM
B_KW = B_HEADS * B_KEY_DIM
B_WIDTH = B_HEADS * B_VAL_DIM
C_KW = C_HEADS * C_KEY_DIM
C_WIDTH = C_HEADS * C_VAL_DIM
ZB_WIDTH = 2 * B_KW + 2 * B_WIDTH
ZC_WIDTH = 2 * C_KW + 2 * C_WIDTH + 128

LANES = 128
MXU_WIDTH = 256
VMEM_LIMIT_BYTES = 56 * 1024 * 1024

ROW_TILE = 512
FF_CHUNK = 1536
REC_CHUNK = 128
REC_CHUNKS_PER_STEP = 4
REC_SEQS_PER_STEP = 8
ATT_ROWS = 1024
DEC_PAD = 16


def _params(n_axes):
    return pltpu.CompilerParams(dimension_semantics=("arbitrary",) * n_axes,
                                vmem_limit_bytes=VMEM_LIMIT_BYTES)


def _const_spec(shape):
    zeros = (0,) * len(shape)
    return pl.BlockSpec(shape, lambda *_: zeros, pipeline_mode=pl.Buffered(1))


def _layer_spec(stacked, layer):
    shape = tuple(stacked.shape[1:])
    zeros = (0,) * len(shape)
    return pl.BlockSpec((None,) + shape, lambda *_: (layer,) + zeros, pipeline_mode=pl.Buffered(1))


def _dot(a, b):
    return jnp.dot(a, b, preferred_element_type=F32)


def _dot_nt(a, b):
    return lax.dot_general(a, b, (((1,), (1,)), ((), ())), preferred_element_type=F32)


def _dot_tn(a, b):
    return lax.dot_general(a, b, (((0,), (0,)), ((), ())), preferred_element_type=F32)


def _dot_split2(x, w):
    hi = x.astype(BF16)
    lo = (x - hi.astype(F32)).astype(BF16)
    return _dot(hi, w) + _dot(lo, w)


def _rms(x, g):
    ms = jnp.mean(x * x, axis=-1, keepdims=True)
    return x * lax.rsqrt(ms + RMS_EPS) * g


def _sigmoid(x):
    return 1.0 / (1.0 + jnp.exp(-x))


def _ff_chunks(dff):
    unit = MXU_WIDTH if dff % MXU_WIDTH == 0 else LANES
    assert dff % unit == 0
    tiles = dff // unit
    n = -(-dff // FF_CHUNK)
    sizes = [(tiles // n + (1 if i < tiles % n else 0)) * unit for i in range(n)]
    starts = np.cumsum([0] + sizes[:-1]).tolist()
    return tuple(zip(starts, sizes))


def _ffn_body(*refs, chunks, with_ple, with_final, n_side_in):
    refs = list(refs)
    if n_side_in:
        _attn_decode_body(*refs[:n_side_in], refs[-1])
        refs = refs[n_side_in:-1]
    h_ref, nrm_ref, wg_ref, wu_ref, wd_ref = refs[:5]
    pos = 5
    if with_ple:
        p_ref, pn_ref, pwg_ref, pwp_ref = refs[pos:pos + 4]
        pos += 4
    if with_final:
        fn_ref = refs[pos]
        pos += 1
    o_ref = refs[pos]
    h = h_ref[...]
    xn = _rms(h, nrm_ref[...]).astype(BF16)
    acc = None
    for lo, chunk in chunks:
        g = _dot(xn, wg_ref[:, lo:lo + chunk])
        u = _dot(xn, wu_ref[:, lo:lo + chunk])
        a = (g * _sigmoid(g) * u).astype(BF16)
        d = _dot(a, wd_ref[lo:lo + chunk, :])
        acc = d if acc is None else acc + d
    h = h + 0.5 * acc
    if with_ple:
        hn = _rms(h, pn_ref[...]).astype(BF16)
        gate = _sigmoid(_dot(hn, pwg_ref[...]))
        proj = _dot(p_ref[...].astype(BF16), pwp_ref[...])
        h = h + gate * proj
    o_ref[...] = h
    if with_final:
        refs[pos + 1][...] = _rms(h, fn_ref[...])


def _ffn(h, layer, nrm, wg, wu, wd, ple=None, final_norm=None, side=None):
    m, d = h.shape
    dff = wg.shape[-1]
    tm = min(ROW_TILE, m)
    assert m % tm == 0
    row = lambda w: pl.BlockSpec((tm, w), lambda i: (i, 0))
    ins = [h, nrm, wg, wu, wd]
    specs = [row(d)] + [_layer_spec(a, layer) for a in ins[1:]]
    if ple is not None:
        ins += list(ple)
        specs += [pl.BlockSpec((None, tm, ple[0].shape[-1]), lambda i: (layer, i, 0))]
        specs += [_layer_spec(a, layer) for a in ple[1:]]
    if final_norm is not None:
        ins.append(final_norm)
        specs.append(_const_spec((1, d)))
    n_out = 2 if final_norm is not None else 1
    out_specs = [row(d)] * n_out
    out_shape = [jax.ShapeDtypeStruct((m, d), F32)] * n_out
    n_side_in = 0
    if side is not None:
        side_args, side_specs, side_out_spec, side_out_shape = _attn_decode_operands(*side)
        assert side_out_shape.shape[0] == m // tm
        ins, specs, n_side_in = side_args + ins, side_specs + specs, len(side_args)
        out_specs, out_shape = out_specs + [side_out_spec], out_shape + [side_out_shape]
    return pl.pallas_call(
        functools.partial(_ffn_body, chunks=_ff_chunks(dff), with_ple=ple is not None,
                          with_final=final_norm is not None, n_side_in=n_side_in),
        grid=(m // tm,),
        in_specs=specs,
        out_specs=out_specs,
        out_shape=out_shape,
        compiler_params=_params(1),
        name="ffn",
    )(*ins)


def _proj_body(h_ref, nrm_ref, w_ref, *refs, tm, dils):
    n_a = 1 + len(dils)
    a_refs = refs[:n_a]
    kf_ref, vf_ref, zb_ref, zc_ref = refs[n_a:n_a + 4]
    xn = _rms(h_ref[...], nrm_ref[...]).astype(BF16)
    a = A_WIDTH
    b0 = 3 * a
    for part in range(3):
        cols = slice(part * a, (part + 1) * a)
        z = _dot(xn, w_ref[:, cols])
        if part == 0:
            z = z * (A_HEAD_DIM ** -0.5)
        elif part == 1:
            kf_ref[...] = z
        else:
            vf_ref[...] = z
        a_refs[0][0, :, cols] = z.astype(a_refs[0].dtype)
        if dils:
            stage = refs[-1]
            for cb in range(a // LANES):
                stage[cb] = z[:, cb * LANES:(cb + 1) * LANES]
        for ref, d in zip(a_refs[1:], dils):
            for r in range(d):
                for cb in range(a // LANES):
                    lo = part * a + cb * LANES
                    ref[r, :, lo:lo + LANES] = stage[cb, pl.ds(r, tm // d, stride=d), :].astype(ref.dtype)
    zb_ref[...] = _dot(xn, w_ref[:, b0:b0 + ZB_WIDTH])
    c0 = b0 + ZB_WIDTH
    qkv_c = 2 * C_KW + C_WIDTH
    zc = _dot(xn, w_ref[:, c0:c0 + qkv_c])
    zc_ref[:, 0:C_KW] = zc[:, 0:C_KW] * (C_KEY_DIM ** -0.5)
    zc_ref[:, C_KW:qkv_c] = zc[:, C_KW:]
    tail = _dot(xn, w_ref[:, c0 + qkv_c:c0 + qkv_c + C_GATE_RANK + C_WIDTH])
    zc_ref[:, qkv_c:qkv_c + C_WIDTH] = tail[:, C_GATE_RANK:]
    lane = lax.broadcasted_iota(jnp.int32, (tm, LANES), 1)
    zc_ref[:, qkv_c + C_WIDTH:] = jnp.where(lane < C_GATE_RANK, tail[:, 0:LANES], 0.0)


def _proj(h, layer, nrm, w, keep, dils, a_dtype):
    m, d = h.shape
    tm = min(ROW_TILE, m)
    assert m % tm == 0 and keep % tm == 0
    nt, nk = m // tm, keep // tm
    row = lambda wd: pl.BlockSpec((tm, wd), lambda i: (i, 0))
    tail = pl.BlockSpec((tm, A_WIDTH), lambda i: (jnp.maximum(i - (nt - nk), 0), 0))
    sds = jax.ShapeDtypeStruct
    all_d = (1,) + tuple(dils)
    return pl.pallas_call(
        functools.partial(_proj_body, tm=tm, dils=tuple(dils)),
        grid=(nt,),
        in_specs=[row(d), _layer_spec(nrm, layer), _layer_spec(w, layer)],
        out_specs=[pl.BlockSpec((dd, tm // dd, 3 * A_WIDTH), lambda i: (0, i, 0)) for dd in all_d]
                  + [tail, tail, row(ZB_WIDTH), row(ZC_WIDTH)],
        out_shape=[sds((dd, m // dd, 3 * A_WIDTH), a_dtype) for dd in all_d]
                  + [sds((keep, A_WIDTH), F32), sds((keep, A_WIDTH), F32),
                     sds((m, ZB_WIDTH), F32), sds((m, ZC_WIDTH), F32)],
        scratch_shapes=[pltpu.VMEM((A_WIDTH // LANES, tm, LANES), F32)] if dils else [],
        compiler_params=_params(1),
        name="proj",
    )(h, nrm, w)


def _alibi_slopes():
    return (2.0 ** (-8.0 * np.arange(1, A_HEADS + 1, dtype=np.float32) / A_HEADS)).astype(np.float32)


def _attn_bias(dil):
    qi = np.arange(A_SUB)[:, None]
    ki = np.arange(2 * A_SUB)[None, :]
    j = qi + A_SUB - ki
    band = (j >= 0) & (j <= A_SUB)
    bias = -_alibi_slopes()[:, None, None] * (j * dil).astype(np.float32)[None]
    return jnp.asarray(np.where(band[None], bias, np.float32(NEG_BIG)).astype(np.float32))


def _attn_body(q_ref, kp_ref, kc_ref, vp_ref, vc_ref, bias_ref, o_ref, lse_ref, kk, vv, *, n_sub):
    qb = A_SUB
    kk[0:A_SUB, :] = kp_ref[...]
    vv[0:A_SUB, :] = vp_ref[...]
    kk[A_SUB:, :] = kc_ref[...]
    vv[A_SUB:, :] = vc_ref[...]
    lane = lax.broadcasted_iota(jnp.int32, (qb, LANES), 1)
    low = lane < A_HEAD_DIM
    low_f = low.astype(F32)
    head_lanes = (low_f.astype(BF16), (1.0 - low_f).astype(BF16))
    first = jnp.where(pl.program_id(1) > 0, 0, A_SUB)
    key_ok = lax.broadcasted_iota(jnp.int32, (qb, 2 * A_SUB), 1) >= first
    for i in range(n_sub):
        rows = slice(i * qb, (i + 1) * qb)
        krows = slice(i * qb, i * qb + 2 * A_SUB)
        lse_tile = jnp.zeros((qb, LANES), F32)
        for hp in range(A_HEADS // 2):
            cols = slice(hp * LANES, (hp + 1) * LANES)
            q2 = q_ref[rows, cols]
            k2 = kk[krows, cols]
            v2 = vv[krows, cols]
            outs = []
            for hh in range(2):
                h = 2 * hp + hh
                s = _dot_nt(q2 * head_lanes[hh], k2) + bias_ref[h]
                if i == 0:
                    s = jnp.where(key_ok, s, NEG_BIG)
                mx = jnp.max(s, axis=-1, keepdims=True)
                p = jnp.exp(s - mx)
                den = jnp.sum(p, axis=-1, keepdims=True)
                outs.append(_dot(p.astype(BF16), v2) / den)
                lse_tile = jnp.where(lane == h, mx + jnp.log(den), lse_tile)
            o_ref[rows, cols] = jnp.where(low, outs[0], outs[1]).astype(o_ref.dtype)
        lse_ref[rows, :] = lse_tile


def _attn_prompt(qkv, dil):
    _, tv, _ = qkv.shape
    assert qkv.shape[0] == dil and tv % A_SUB == 0
    bq = min(ATT_ROWS, tv)
    assert tv % bq == 0 and bq % A_SUB == 0
    cur = lambda part: pl.BlockSpec((None, bq, A_WIDTH), lambda r, n: (r, n, part))
    prev = lambda part: pl.BlockSpec((None, A_SUB, A_WIDTH),
                                     lambda r, n: (r, jnp.maximum(n * (bq // A_SUB) - 1, 0), part))
    return pl.pallas_call(
        functools.partial(_attn_body, n_sub=bq // A_SUB),
        grid=(dil, tv // bq),
        in_specs=[cur(0), prev(1), cur(1), prev(2), cur(2), _const_spec((A_HEADS, A_SUB, 2 * A_SUB))],
        out_specs=[cur(0), pl.BlockSpec((None, bq, LANES), lambda r, n: (r, n, 0))],
        out_shape=[jax.ShapeDtypeStruct((dil, tv, A_WIDTH), BF16), jax.ShapeDtypeStruct((dil, tv, LANES), F32)],
        scratch_shapes=[pltpu.VMEM((A_SUB + bq, A_WIDTH), BF16), pltpu.VMEM((A_SUB + bq, A_WIDTH), BF16)],
        compiler_params=_params(2),
        name=f"attn_prompt_d{dil}",
    )(qkv, qkv, qkv, qkv, qkv, _attn_bias(dil))


def _decode_tables(n_past):
    dist = n_past - np.arange(n_past)
    mult = np.zeros(n_past, np.float32)
    for window, dil in A_CONFIGS:
        mult += ((dist % dil == 0) & (dist // dil <= window // dil)).astype(np.float32)
    bias = -_alibi_slopes()[:, None] * dist.astype(np.float32)[None, :]
    return jnp.asarray(bias), jnp.asarray(mult[None, :])


def _attn_decode_body(q_ref, kn_ref, vn_ref, kc_ref, vc_ref, bias_ref, mult_ref, o_ref):
    q_t = q_ref[...]
    s_new = jnp.sum(q_t * kn_ref[...], axis=0, keepdims=True)
    mult = mult_ref[...]
    live = mult > 0.0
    n_cfg = float(len(A_CONFIGS))
    lane = lax.broadcasted_iota(jnp.int32, q_t.shape, 1)
    out = jnp.zeros(q_t.shape, F32)
    for h in range(A_HEADS):
        s = jnp.sum(kc_ref[h] * q_t[:, h:h + 1], axis=0, keepdims=True) + bias_ref[h:h + 1, :]
        s = jnp.where(live, s, NEG_BIG)
        s0 = s_new[:, h:h + 1]
        mx = jnp.maximum(jnp.max(s, axis=-1, keepdims=True), s0)
        p = jnp.exp(s - mx) * mult
        p0 = jnp.exp(s0 - mx) * n_cfg
        den = jnp.sum(p, axis=-1, keepdims=True) + p0
        acc = jnp.sum(vc_ref[h] * p, axis=-1, keepdims=True) + p0 * vn_ref[:, h:h + 1]
        out = jnp.where(lane == h, acc / den, out)
    o_ref[...] = out


def _attn_decode_operands(q_t, kn_t, vn_t, k_cache_t, v_cache_t, layer):
    nb = q_t.shape[0]
    n_past = k_cache_t.shape[-1]
    assert n_past == A_MAX_WINDOW
    bias, mult = _decode_tables(n_past)
    new = pl.BlockSpec((None, A_HEAD_DIM, A_HEADS), lambda b: (b, 0, 0))
    past = pl.BlockSpec((None, None, A_HEADS, A_HEAD_DIM, n_past), lambda b: (layer, b, 0, 0, 0))
    return ([q_t, kn_t, vn_t, k_cache_t, v_cache_t, bias, mult],
            [new, new, new, past, past, _const_spec(bias.shape), _const_spec(mult.shape)],
            new, jax.ShapeDtypeStruct((nb, A_HEAD_DIM, A_HEADS), F32))


def _attn_decode(*side):
    args, in_specs, out_spec, out_shape = _attn_decode_operands(*side)
    return pl.pallas_call(
        _attn_decode_body,
        grid=(out_shape.shape[0],),
        in_specs=in_specs,
        out_specs=out_spec,
        out_shape=out_shape,
        compiler_params=_params(1),
        name="attn_decode",
    )(*args)


def _rec_tables(c, single, stacked_heads):
    nl = 0 if single else int(math.log2(c))
    assert single or 2 ** nl == c
    t = np.arange(c)[:, None]
    i = np.arange(c)[None, :]
    e_rows, masks = [], []
    for lvl in range(nl):
        half = c >> (lvl + 1)
        mid = (t // (2 * half)) * (2 * half) + half
        right = t >= mid
        e_rows.append(np.where(right, (i >= mid) & (i <= t), (i > t) & (i < mid)))
        same = (t // (2 * half)) == (i // (2 * half))
        masks.append(same & right & (i < mid))
    masks.append(t == i)
    e_rows.append(i <= t)
    e_rows.append(i > t)
    e = np.concatenate(e_rows, 0).astype(np.float32)
    m = np.stack(masks, 0).astype(np.float32)
    if stacked_heads > 1:
        m = np.tile(m, (1, stacked_heads, 1))
    return jnp.asarray(np.concatenate([e, e], 1), dtype=BF16), jnp.asarray(m)


def _rec_core(q, k, lf, v, e_ref, m_ref, st_ref, o_ref, *, c, kind, single):
    nl = 0 if single else int(math.log2(c))
    n_win = q.shape[1] // LANES
    win = lambda x, w: x[:, w * LANES:(w + 1) * LANES]
    if single:
        e_cum = jnp.exp(lf)
        ke_f = k
    else:
        l1 = lf.astype(BF16)
        l2 = (lf - l1.astype(F32)).astype(BF16)
        decay = jnp.exp(_dot(e_ref[...], jnp.concatenate([l1, l2], axis=0)))
        seg = lambda n: decay[n * c:(n + 1) * c]
        e_cum = seg(nl)
        ke_f = k * seg(nl + 1)
    yield
    row = lax.broadcasted_iota(jnp.int32, (c, LANES), 0)

    def level_operand(n, w):
        half = c >> (n + 1)
        qw, kw = win(q, w), win(k, w)
        if half >= 8:
            pieces = [(qw if j % 2 else kw)[j * half:(j + 1) * half] for j in range(c // half)]
            side = jnp.concatenate(pieces, axis=0)
        else:
            side = jnp.where((row & half) != 0, qw, kw)
        return (side * win(seg(n), w)).astype(BF16)

    u = [[level_operand(n, w) for w in range(n_win)] for n in range(nl)]
    uq = [win(q, w).astype(BF16) for w in range(n_win)]
    uk = [win(k, w).astype(BF16) for w in range(n_win)]
    qg = (q * e_cum).astype(BF16)
    ke = ke_f.astype(BF16)
    e_tot = e_cum[0:1, :] if single else e_cum[c - 1:c, :]
    vb = v.astype(BF16)
    yield
    lane = lax.broadcasted_iota(jnp.int32, (c, LANES), 1)
    low = lane < B_VAL_DIM
    if kind == "c":
        head = lax.broadcasted_iota(jnp.int32, (1, LANES), 1) // C_KEY_DIM
        hm = [(head == h).astype(F32).astype(BF16) for h in range(C_HEADS)]
        stack = lambda x: jnp.concatenate([x * hm[h] for h in range(C_HEADS)], axis=0)
        att_all = _dot_nt(stack(uq[0]), uk[0]) * m_ref[nl]
        for n in range(nl):
            att_all = att_all + _dot_nt(stack(u[n][0]), u[n][0]) * m_ref[n]
        att_all = att_all.astype(BF16)
        atts = [att_all[h * c:(h + 1) * c] for h in range(C_HEADS)]
    else:
        atts = []
        for h in range(B_HEADS):
            att = _dot_nt(uq[h], uk[h]) * m_ref[nl]
            for n in range(nl):
                att = att + _dot_nt(u[n][h], u[n][h]) * m_ref[n]
            atts.append(att.astype(BF16))
    intra = [jnp.where(low, _dot(atts[2 * pair], win(vb, pair)), _dot(atts[2 * pair + 1], win(vb, pair)))
             for pair in range(2)]
    half_rows = [slice(hh * B_VAL_DIM, (hh + 1) * B_VAL_DIM) for hh in range(2)]
    if kind == "c":
        srow = lax.broadcasted_iota(jnp.int32, (LANES, LANES), 0) // B_VAL_DIM
        slane = lax.broadcasted_iota(jnp.int32, (LANES, LANES), 1) // C_KEY_DIM
        added = [_dot_tn(win(vb, pair), ke) * (slane == srow + 2 * pair).astype(F32) for pair in range(2)]
    else:
        added = [jnp.concatenate([_dot_tn(win(vb, pair), win(ke, 2 * pair + hh))[half_rows[hh], :]
                                  for hh in range(2)], axis=0) for pair in range(2)]
    yield
    for pair in range(2):
        st = st_ref[pair]
        stb = st.astype(BF16)
        if kind == "c":
            o_ref[:, pair * LANES:(pair + 1) * LANES] = intra[pair] + _dot_nt(qg, stb)
            st_ref[pair] = st * e_tot + added[pair]
            continue
        carried = [_dot_nt(win(qg, 2 * pair + hh), stb) for hh in range(2)]
        o_ref[:, pair * LANES:(pair + 1) * LANES] = intra[pair] + jnp.where(low, carried[0], carried[1])
        keep = jnp.concatenate([st[half_rows[hh], :] * win(e_tot, 2 * pair + hh) for hh in range(2)], axis=0)
        st_ref[pair] = keep + added[pair]


_REC_STAGES = 4
_REC_KINDS = ("b", "c")
_REC_N_PARAMS = {"b": 1, "c": 2}


def _rec_body(*refs, c, n_valid, n_b, n_sub):
    ins, pos = {}, 0
    for kind in _REC_KINDS:
        n = 4 + _REC_N_PARAMS[kind]
        ins[kind] = refs[pos:pos + n]
        pos += n
    outs = {kind: refs[pos + 2 * i:pos + 2 * i + 2] for i, kind in enumerate(_REC_KINDS)}
    pos += 2 * len(_REC_KINDS)
    states = {kind: refs[pos + i] for i, kind in enumerate(_REC_KINDS)}

    @pl.when(pl.program_id(1) == 0)
    def _():
        for kind in _REC_KINDS:
            states[kind][...] = ins[kind][-1][...]

    chunks = []
    for bb in range(n_b):
        for j in range(n_sub):
            for kind in _REC_KINDS:
                z_ref, *params, e_ref, m_ref, _ = ins[kind]
                chunks.append(_rec_chunk(z_ref.at[bb, pl.ds(j * c, c)], params, e_ref, m_ref, states[kind].at[bb],
                                         outs[kind][0].at[bb, pl.ds(j * c, c)], kind=kind, c=c, n_valid=n_valid))
    for _ in range(_REC_STAGES - 1):
        for chunk in chunks:
            next(chunk)
    for chunk in chunks:
        for _ in chunk:
            raise AssertionError("more stages than _REC_STAGES")

    @pl.when(pl.program_id(1) == pl.num_programs(1) - 1)
    def _():
        for kind in _REC_KINDS:
            outs[kind][1][...] = states[kind][...]


def _rec_chunk(z_ref, params, e_ref, m_ref, st_ref, o_ref, *, kind, c, n_valid):
    z = z_ref[...]
    if kind == "b":
        lb = params[0][...]
        f = lb + (1.0 - lb) * _sigmoid(z[:, B_KW:2 * B_KW])
        lf = jnp.log(jnp.maximum(f, MIN_F))
        k = 1.0 - f
        zq = z[:, 0:B_KW]
        q = zq * _sigmoid(zq)
        v = z[:, 2 * B_KW:2 * B_KW + B_WIDTH]
    else:
        cwg_ref, cb_ref = params
        q = z[:, 0:C_KW]
        k = z[:, C_KW:2 * C_KW]
        v = z[:, 2 * C_KW:2 * C_KW + C_WIDTH]
        x = _dot(z[:, ZC_WIDTH - LANES:].astype(BF16), cwg_ref[...]) + cb_ref[...]
        lf = (jnp.minimum(x, 0.0) - jnp.log1p(jnp.exp(-jnp.abs(x)))) / C_GATE_TEMP
    if n_valid < c:
        live = lax.broadcasted_iota(jnp.int32, (c, 1), 0) < n_valid
        q = jnp.where(live, q, 0.0)
        k = jnp.where(live, k, 0.0)
        lf = jnp.where(live, lf, 0.0)
    yield from _rec_core(q, k, lf, v, e_ref, m_ref, st_ref, o_ref, c=c, kind=kind, single=n_valid == 1)


def _recurrences(zs, layer, params, s0s, s0_layer, n_valid=None):
    nb, t, _ = zs["b"].shape
    c = min(REC_CHUNK, t)
    assert t % c == 0
    n_valid = c if n_valid is None else n_valid
    assert n_valid in (1, c)
    n_sub = math.gcd(t // c, REC_CHUNKS_PER_STEP)
    n_b = math.gcd(nb, REC_SEQS_PER_STEP)
    st_spec = pl.BlockSpec((n_b, 2, LANES, LANES), lambda b, j: (b, 0, 0, 0))
    s0_spec = pl.BlockSpec((None, n_b, 2, LANES, LANES), lambda b, j: (s0_layer, b, 0, 0, 0))
    args, in_specs = [], []
    for kind in _REC_KINDS:
        assert len(params[kind]) == _REC_N_PARAMS[kind] and zs[kind].shape[:2] == (nb, t)
        e, m = _rec_tables(c, n_valid == 1, C_HEADS if kind == "c" else 1)
        args += [zs[kind], *params[kind], e, m, s0s[kind]]
        in_specs += [pl.BlockSpec((n_b, n_sub * c, zs[kind].shape[2]), lambda b, j: (b, j, 0))]
        in_specs += [_layer_spec(x, layer) for x in params[kind]]
        in_specs += [_const_spec(e.shape), _const_spec(m.shape), s0_spec]
    n_kinds = len(_REC_KINDS)
    res = pl.pallas_call(
        functools.partial(_rec_body, c=c, n_valid=n_valid, n_b=n_b, n_sub=n_sub),
        grid=(nb // n_b, t // (c * n_sub)),
        in_specs=in_specs,
        out_specs=[pl.BlockSpec((n_b, n_sub * c, 2 * LANES), lambda b, j: (b, j, 0)), st_spec] * n_kinds,
        out_shape=[jax.ShapeDtypeStruct((nb, t, 2 * LANES), F32),
                   jax.ShapeDtypeStruct((nb, 2, LANES, LANES), F32)] * n_kinds,
        scratch_shapes=[pltpu.VMEM((n_b, 2, LANES, LANES), F32)] * n_kinds,
        compiler_params=_params(2),
        name="recurrences",
    )(*args)
    return {kind: (res[2 * i], res[2 * i + 1]) for i, kind in enumerate(_REC_KINDS)}


def _mix_body(*refs, tm, dils):
    h_ref = refs[0]
    n_att = 2 * len(dils) if dils else 1
    att_refs = refs[1:1 + n_att]
    ob_ref, bg_ref, oc_ref, cg_ref, bn_ref, cn_ref, ex_ref, bd_ref, w_ref, out_ref, mixed = refs[1 + n_att:12 + n_att]
    if dils:
        o_refs, l_refs = att_refs[:len(dils)], att_refs[len(dils):]
        o_nat, l_nat = refs[12 + n_att], refs[13 + n_att]
        n_cb = A_WIDTH // LANES
        lses = []
        for c, d in enumerate(dils):
            if d == 1:
                lses.append(l_refs[c][0])
                continue
            for r in range(d):
                l_nat[c, pl.ds(r, tm // d, stride=d), :] = l_refs[c][r]
                for cb in range(n_cb):
                    o_nat[c, cb, pl.ds(r, tm // d, stride=d), :] = (
                        o_refs[c][r, :, cb * LANES:(cb + 1) * LANES].astype(F32))
            lses.append(l_nat[c])
        mx = functools.reduce(jnp.maximum, lses)
        es = [jnp.exp(l - mx) for l in lses]
        den = functools.reduce(lambda a, b: a + b, es)
        ex = ex_ref[...]
        wts = [_dot_split2(e / den, ex) for e in es]
        for cb in range(n_cb):
            cs = slice(cb * LANES, (cb + 1) * LANES)
            terms = [wts[c][:, cs] * (o_refs[c][0, :, cs] if d == 1 else o_nat[c, cb]) for c, d in enumerate(dils)]
            mixed[:, cs] = functools.reduce(lambda a, b: a + b, terms).astype(BF16)
    else:
        mixed[:, 0:A_WIDTH] = att_refs[0][...].astype(BF16)
    bd = bd_ref[...]
    ob = ob_ref[...]
    obn = ob * lax.rsqrt(_dot_split2(ob * ob, bd) + RMS_EPS) * bn_ref[...]
    mixed[:, A_WIDTH:A_WIDTH + B_WIDTH] = (obn * _sigmoid(bg_ref[...])).astype(BF16)
    oc = oc_ref[...]
    ocn = oc * lax.rsqrt(_dot_split2(oc * oc, bd) + RMS_EPS) * cn_ref[...]
    cg = cg_ref[...]
    mixed[:, A_WIDTH + B_WIDTH:] = (ocn * (cg * _sigmoid(cg))).astype(BF16)
    out_ref[...] = h_ref[...] + _dot(mixed[...], w_ref[...])


def _mix(h, layer, att, dils, ob, zb, oc, zc, b_norm, c_norm, w_out):
    m, d = h.shape
    tm = min(ROW_TILE, m)
    assert m % tm == 0
    row = lambda w, blk=0: pl.BlockSpec((tm, w), lambda i: (i, blk))
    if dils:
        att_specs = [pl.BlockSpec((dd, tm // dd, wd), lambda i: (0, i, 0))
                     for wd in (A_WIDTH, LANES) for dd in dils]
        att_scratch = [pltpu.VMEM((len(dils), A_WIDTH // LANES, tm, LANES), F32),
                       pltpu.VMEM((len(dils), tm, LANES), F32)]
    else:
        att_specs, att_scratch = [row(A_WIDTH)], []
    expand = np.zeros((LANES, A_WIDTH), np.float32)
    for hd in range(A_HEADS):
        expand[hd, hd * A_HEAD_DIM:(hd + 1) * A_HEAD_DIM] = 1.0
    grp = np.arange(B_WIDTH) // B_VAL_DIM
    blockmean = (grp[:, None] == grp[None, :]).astype(np.float32) / B_VAL_DIM
    d_mix = A_WIDTH + B_WIDTH + C_WIDTH
    return pl.pallas_call(
        functools.partial(_mix_body, tm=tm, dils=tuple(dils)),
        grid=(m // tm,),
        in_specs=[row(d)] + att_specs
                 + [row(B_WIDTH), row(B_WIDTH, (2 * B_KW + B_WIDTH) // B_WIDTH),
                    row(C_WIDTH), row(C_WIDTH, (2 * C_KW + C_WIDTH) // C_WIDTH)]
                 + [_layer_spec(b_norm, layer), _layer_spec(c_norm, layer), _const_spec((LANES, A_WIDTH)),
                    _const_spec((B_WIDTH, B_WIDTH)), _layer_spec(w_out, layer)],
        out_specs=row(d),
        out_shape=jax.ShapeDtypeStruct((m, d), F32),
        scratch_shapes=[pltpu.VMEM((tm, d_mix), BF16)] + att_scratch,
        compiler_params=_params(1),
        name="mix_out",
    )(h, *att, ob, zb, oc, zc, b_norm, c_norm,
      jnp.asarray(expand, dtype=BF16), jnp.asarray(blockmean, dtype=BF16), w_out)


def _state_to_pairs(s):
    nl, nb, nh, dk, dv = s.shape
    st = jnp.swapaxes(s, 3, 4)
    if dk < LANES:
        st = jnp.stack([jnp.pad(st[:, :, h], ((0, 0), (0, 0), (0, 0), (h * dk, LANES - (h + 1) * dk)))
                        for h in range(nh)], 2)
    return st.reshape(nl, nb, 2, 2 * dv, LANES)


def _pairs_to_state(st, dk):
    nb = st.shape[0]
    st = st.reshape(nb, 4, B_VAL_DIM, LANES)
    if dk < LANES:
        st = jnp.stack([st[:, h, :, h * dk:(h + 1) * dk] for h in range(4)], 1)
    return jnp.swapaxes(st, 2, 3)


def _layer_tail(h, i, w, lb_all, p, final_norm, zb, zc, att, dils, states, s0_layer, decode):
    depth = w["w_in"].shape[0]
    if decode:
        nb = h.shape[0]
        pad = lambda a: jnp.pad(a.reshape(nb, 1, -1), ((0, 0), (0, DEC_PAD - 1), (0, 0)))
        zb3, zc3, n_valid = pad(zb), pad(zc), 1
    else:
        zb3, zc3, n_valid = zb[None], zc[None], None
    rec = _recurrences(dict(b=zb3, c=zc3), i, dict(b=[lb_all], c=[w["c_w_gate"], w["c_gate_bias"]]),
                       states, s0_layer, n_valid)
    (ob, sb), (oc, sc) = rec["b"], rec["c"]
    ob, oc = (ob[:, 0], oc[:, 0]) if decode else (ob[0], oc[0])
    h = _mix(h, i, att, dils, ob, zb, oc, zc, w["b_out_norm"], w["c_out_norm"], w["w_out"])
    ple = (p, w["ple_norm"], w["ple_w_gate"], w["ple_w_proj"])
    last = final_norm if i == depth - 1 else None
    h, *y = _ffn(h, i, w["ffn2_norm"], w["ffn2_w_gate"], w["ffn2_w_up"], w["ffn2_w_down"], ple, last)
    return h, (y[0] if y else None), _pairs_to_state(sb, B_KEY_DIM), _pairs_to_state(sc, C_KEY_DIM)


def kernel(x_prompt, x_sample, cache_k_a, cache_v_a, state_b, state_c, p_prompt, p_sample, ffn1_norm, ffn1_w_gate, ffn1_w_up, ffn1_w_down, mix_norm, w_in, lb_logits, b_out_norm, c_w_gate, c_gate_bias, c_out_norm, w_out, ffn2_norm, ffn2_w_gate, ffn2_w_up, ffn2_w_down, ple_norm, ple_w_gate, ple_w_proj, final_norm):
    depth = w_in.shape[0]
    rows = lambda a: a.reshape(depth, 1, -1)
    w = {name: a.astype(BF16) for name, a in dict(
        ffn1_w_gate=ffn1_w_gate, ffn1_w_up=ffn1_w_up, ffn1_w_down=ffn1_w_down, w_out=w_out, ffn2_w_gate=ffn2_w_gate,
        ffn2_w_up=ffn2_w_up, ffn2_w_down=ffn2_w_down, ple_w_gate=ple_w_gate, ple_w_proj=ple_w_proj).items()}
    w.update(
        ffn1_norm=rows(ffn1_norm), mix_norm=rows(mix_norm), w_in=w_in.astype(BF16), b_out_norm=rows(b_out_norm),
        c_w_gate=jnp.pad(c_w_gate, ((0, 0), (0, LANES - C_GATE_RANK), (0, 0))).astype(BF16),
        c_gate_bias=rows(c_gate_bias), c_out_norm=rows(c_out_norm), ffn2_norm=rows(ffn2_norm), ple_norm=rows(ple_norm))
    sm = jax.nn.softmax(lb_logits.astype(F32), axis=0)
    lb_all = jnp.maximum(jnp.cumsum(sm, axis=0) - sm[0], 0.0).reshape(depth, 1, B_KW)
    fn = final_norm.reshape(1, -1)

    (one, t, d), (nb, one_tok, _) = x_prompt.shape, x_sample.shape
    assert one == 1 and one_tok == 1
    hp, hs = x_prompt.reshape(t, d), x_sample.reshape(nb, d)
    pp, ps = p_prompt.reshape(depth, t, -1), p_sample.reshape(depth, nb, -1)
    keep = min(A_MAX_WINDOW, t)
    k_cache_t = jnp.transpose(cache_k_a, (0, 1, 3, 4, 2))
    v_cache_t = jnp.transpose(cache_v_a, (0, 1, 3, 4, 2))
    states_s = dict(b=_state_to_pairs(state_b), c=_state_to_pairs(state_c))
    states_p = dict(b=jnp.zeros((1, 1, 2, LANES, LANES), F32), c=jnp.zeros((1, 1, 2, LANES, LANES), F32))
    dils = tuple(dil for _, dil in A_CONFIGS)
    assert dils[0] == 1
    cols = lambda a: jnp.swapaxes(a.reshape(nb, A_HEADS, A_HEAD_DIM), 1, 2)
    ffn1 = lambda h, i, side=None: _ffn(h, i, w["ffn1_norm"], w["ffn1_w_gate"], w["ffn1_w_up"], w["ffn1_w_down"],
                                        side=side)
    outs = {key: [] for key in ("k_p", "v_p", "sb_p", "sc_p", "k_s", "v_s", "sb_s", "sc_s")}
    for i in range(depth):
        hs, = ffn1(hs, i)
        qkv_s, kf_s, vf_s, zb_s, zc_s = _proj(hs, i, w["mix_norm"], w["w_in"], nb, (), F32)
        side = (cols(qkv_s[0, :, 0:A_WIDTH]), cols(kf_s), cols(vf_s), k_cache_t, v_cache_t, i)
        if t // min(ROW_TILE, t) == nb:
            hp, o_t = ffn1(hp, i, side)
        else:
            hp, = ffn1(hp, i)
            o_t = _attn_decode(*side)
        *qkvs, kf_p, vf_p, zb_p, zc_p = _proj(hp, i, w["mix_norm"], w["w_in"], keep, dils[1:], BF16)
        res = [_attn_prompt(a, dil) for a, dil in zip(qkvs, dils)]
        att_p = [r[0] for r in res] + [r[1] for r in res]
        hp, y_p, sb, sc = _layer_tail(hp, i, w, lb_all, pp, fn, zb_p, zc_p, att_p, dils, states_p, 0, False)
        outs["k_p"].append(kf_p.reshape(1, keep, A_HEADS, A_HEAD_DIM))
        outs["v_p"].append(vf_p.reshape(1, keep, A_HEADS, A_HEAD_DIM))
        outs["sb_p"].append(sb)
        outs["sc_p"].append(sc)
        att_s = [jnp.swapaxes(o_t, 1, 2).reshape(nb, A_WIDTH)]
        hs, y_s, sb, sc = _layer_tail(hs, i, w, lb_all, ps, fn, zb_s, zc_s, att_s, (), states_s, i, True)
        outs["k_s"].append(kf_s.reshape(nb, 1, A_HEADS, A_HEAD_DIM))
        outs["v_s"].append(vf_s.reshape(nb, 1, A_HEADS, A_HEAD_DIM))
        outs["sb_s"].append(sb)
        outs["sc_s"].append(sc)
    st = {key: jnp.stack(val) for key, val in outs.items()}
    return (y_p.reshape(1, t, d), y_s.reshape(nb, 1, d), st["k_p"], st["v_p"], st["sb_p"], st["sc_p"],
            st["k_s"], st["v_s"], st["sb_s"], st["sc_s"])
```

```python
import functools
import math

import numpy as np
import jax
import jax.numpy as jnp
from jax import lax
from jax.experimental import pallas as pl
from jax.experimental.pallas import tpu as pltpu

F32 = jnp.float32
BF16 = jnp.bfloat16

RMS_EPS = 1e-6
NEG_BIG = -1e30
MIN_F = 1e-30
A_HEADS = 8
A_HEAD_DIM = 64
A_CONFIGS = ((128, 1), (512, 4), (2048, 16))
A_MAX_WINDOW = 2048
A_SUB = 128
B_HEADS = 4
B_KEY_DIM = 128
B_VAL_DIM = 64
C_HEADS = 4
C_KEY_DIM = 32
C_VAL_DIM = 64
C_GATE_RANK = 16
C_GATE_TEMP = 16.0
A_WIDTH = A_HEADS * A_HEAD_DIM
B_KW = B_HEADS * B_KEY_DIM
B_WIDTH = B_HEADS * B_VAL_DIM
C_KW = C_HEADS * C_KEY_DIM
C_WIDTH = C_HEADS * C_VAL_DIM
ZB_WIDTH = 2 * B_KW + 2 * B_WIDTH
ZC_WIDTH = 2 * C_KW + 2 * C_WIDTH + 128

LANES = 128
MXU_WIDTH = 256
VMEM_LIMIT_BYTES = 56 * 1024 * 1024

ROW_TILE = 512
FF_CHUNK = 1536
REC_CHUNK = 128
REC_CHUNKS_PER_STEP = 4
REC_SEQS_PER_STEP = 8
ATT_ROWS = 1024
DEC_PAD = 16


def _params(n_axes):
    return pltpu.CompilerParams(dimension_semantics=("arbitrary",) * n_axes,
                                vmem_limit_bytes=VMEM_LIMIT_BYTES)


def _const_spec(shape):
    zeros = (0,) * len(shape)
    return pl.BlockSpec(shape, lambda *_: zeros, pipeline_mode=pl.Buffered(1))


def _layer_spec(stacked, layer):
    shape = tuple(stacked.shape[1:])
    zeros = (0,) * len(shape)
    return pl.BlockSpec((None,) + shape, lambda *_: (layer,) + zeros, pipeline_mode=pl.Buffered(1))


def _dot(a, b):
    return jnp.dot(a, b, preferred_element_type=F32)


def _dot_nt(a, b):
    return lax.dot_general(a, b, (((1,), (1,)), ((), ())), preferred_element_type=F32)


def _dot_tn(a, b):
    return lax.dot_general(a, b, (((0,), (0,)), ((), ())), preferred_element_type=F32)


def _dot_split2(x, w):
    hi = x.astype(BF16)
    lo = (x - hi.astype(F32)).astype(BF16)
    if w.shape[0] == 2 * x.shape[1]:
        return _dot(jnp.concatenate([hi, lo], axis=1), w)
    return _dot(hi, w) + _dot(lo, w)


def _rms(x, g):
    ms = jnp.mean(x * x, axis=-1, keepdims=True)
    return x * lax.rsqrt(ms + RMS_EPS) * g


def _sigmoid(x):
    return 1.0 / (1.0 + jnp.exp(-x))


def _ff_chunks(dff):
    unit = MXU_WIDTH if dff % MXU_WIDTH == 0 else LANES
    assert dff % unit == 0
    tiles = dff // unit
    n = -(-dff // FF_CHUNK)
    sizes = [(tiles // n + (1 if i < tiles % n else 0)) * unit for i in range(n)]
    starts = np.cumsum([0] + sizes[:-1]).tolist()
    return tuple(zip(starts, sizes))


def _ffn_body(*refs, chunks, with_ple, with_final, n_side_in):
    refs = list(refs)
    if n_side_in:
        _attn_decode_body(*refs[:n_side_in], refs[-1])
        refs = refs[n_side_in:-1]
    h_ref, nrm_ref, wg_ref, wu_ref, wd_ref = refs[:5]
    pos = 5
    if with_ple:
        p_ref, pn_ref, pwg_ref, pwp_ref = refs[pos:pos + 4]
        pos += 4
    if with_final:
        fn_ref = refs[pos]
        pos += 1
    o_ref = refs[pos]
    h = h_ref[...]
    xn = _rms(h, nrm_ref[...]).astype(BF16)
    acc = None
    for lo, chunk in chunks:
        g = _dot(xn, wg_ref[:, lo:lo + chunk])
        u = _dot(xn, wu_ref[:, lo:lo + chunk])
        a = (g * _sigmoid(g) * u).astype(BF16)
        d = _dot(a, wd_ref[lo:lo + chunk, :])
        acc = d if acc is None else acc + d
    h = h + 0.5 * acc
    if with_ple:
        hn = _rms(h, pn_ref[...]).astype(BF16)
        gate = _sigmoid(_dot(hn, pwg_ref[...]))
        proj = _dot(p_ref[...].astype(BF16), pwp_ref[...])
        h = h + gate * proj
    o_ref[...] = h
    if with_final:
        refs[pos + 1][...] = _rms(h, fn_ref[...])


def _ffn(h, layer, nrm, wg, wu, wd, ple=None, final_norm=None, side=None):
    m, d = h.shape
    dff = wg.shape[-1]
    tm = min(ROW_TILE, m)
    assert m % tm == 0
    row = lambda w: pl.BlockSpec((tm, w), lambda i: (i, 0))
    ins = [h, nrm, wg, wu, wd]
    specs = [row(d)] + [_layer_spec(a, layer) for a in ins[1:]]
    if ple is not None:
        ins += list(ple)
        specs += [pl.BlockSpec((None, tm, ple[0].shape[-1]), lambda i: (layer, i, 0))]
        specs += [_layer_spec(a, layer) for a in ple[1:]]
    if final_norm is not None:
        ins.append(final_norm)
        specs.append(_const_spec((1, d)))
    n_out = 2 if final_norm is not None else 1
    out_specs = [row(d)] * n_out
    out_shape = [jax.ShapeDtypeStruct((m, d), F32)] * n_out
    n_side_in = 0
    if side is not None:
        side_args, side_specs, side_out_spec, side_out_shape = _attn_decode_operands(*side)
        assert side_out_shape.shape[0] == m // tm
        ins, specs, n_side_in = side_args + ins, side_specs + specs, len(side_args)
        out_specs, out_shape = out_specs + [side_out_spec], out_shape + [side_out_shape]
    return pl.pallas_call(
        functools.partial(_ffn_body, chunks=_ff_chunks(dff), with_ple=ple is not None,
                          with_final=final_norm is not None, n_side_in=n_side_in),
        grid=(m // tm,),
        in_specs=specs,
        out_specs=out_specs,
        out_shape=out_shape,
        compiler_params=_params(1),
        name="ffn",
    )(*ins)


def _proj_body(h_ref, nrm_ref, w_ref, *refs, tm, dils):
    n_a = 1 + len(dils)
    a_refs = refs[:n_a]
    kf_ref, vf_ref, zb_ref, zc_ref = refs[n_a:n_a + 4]
    xn = _rms(h_ref[...], nrm_ref[...]).astype(BF16)
    a = A_WIDTH
    b0 = 3 * a
    for part in range(3):
        cols = slice(part * a, (part + 1) * a)
        z = _dot_nt(xn, w_ref[cols, :])
        if part == 0:
            z = z * (A_HEAD_DIM ** -0.5)
        elif part == 1:
            kf_ref[...] = z
        else:
            vf_ref[...] = z
        a_refs[0][0, :, cols] = z.astype(a_refs[0].dtype)
        if dils:
            stage = refs[-1]
            for cb in range(a // LANES):
                stage[cb] = z[:, cb * LANES:(cb + 1) * LANES]
        for ref, d in zip(a_refs[1:], dils):
            for r in range(d):
                for cb in range(a // LANES):
                    lo = part * a + cb * LANES
                    ref[r, :, lo:lo + LANES] = stage[cb, pl.ds(r, tm // d, stride=d), :].astype(ref.dtype)
    zb_ref[...] = _dot_nt(xn, w_ref[b0:b0 + ZB_WIDTH, :])
    c0 = b0 + ZB_WIDTH
    qkv_c = 2 * C_KW + C_WIDTH
    zc = _dot_nt(xn, w_ref[c0:c0 + qkv_c, :])
    zc_ref[:, 0:C_KW] = zc[:, 0:C_KW] * (C_KEY_DIM ** -0.5)
    zc_ref[:, C_KW:qkv_c] = zc[:, C_KW:]
    r0 = c0 + qkv_c
    zc_ref[:, qkv_c:qkv_c + C_WIDTH] = _dot_nt(xn, w_ref[r0 + C_GATE_RANK:r0 + C_GATE_RANK + C_WIDTH, :])
    lane = lax.broadcasted_iota(jnp.int32, (tm, LANES), 1)
    rank = _dot_nt(xn, w_ref[r0:r0 + LANES, :])
    zc_ref[:, qkv_c + C_WIDTH:] = jnp.where(lane < C_GATE_RANK, rank, 0.0)


def _proj(h, layer, nrm, w, keep, dils, a_dtype):
    m, d = h.shape
    tm = min(ROW_TILE, m)
    assert m % tm == 0 and keep % tm == 0
    nt, nk = m // tm, keep // tm
    row = lambda wd: pl.BlockSpec((tm, wd), lambda i: (i, 0))
    tail = pl.BlockSpec((tm, A_WIDTH), lambda i: (jnp.maximum(i - (nt - nk), 0), 0))
    sds = jax.ShapeDtypeStruct
    all_d = (1,) + tuple(dils)
    return pl.pallas_call(
        functools.partial(_proj_body, tm=tm, dils=tuple(dils)),
        grid=(nt,),
        in_specs=[row(d), _layer_spec(nrm, layer), _layer_spec(w, layer)],
        out_specs=[pl.BlockSpec((dd, tm // dd, 3 * A_WIDTH), lambda i: (0, i, 0)) for dd in all_d]
                  + [tail, tail, row(ZB_WIDTH), row(ZC_WIDTH)],
        out_shape=[sds((dd, m // dd, 3 * A_WIDTH), a_dtype) for dd in all_d]
                  + [sds((keep, A_WIDTH), F32), sds((keep, A_WIDTH), F32),
                     sds((m, ZB_WIDTH), F32), sds((m, ZC_WIDTH), F32)],
        scratch_shapes=[pltpu.VMEM((A_WIDTH // LANES, tm, LANES), F32)] if dils else [],
        compiler_params=_params(1),
        name="proj",
    )(h, nrm, w)


def _alibi_slopes():
    return (2.0 ** (-8.0 * np.arange(1, A_HEADS + 1, dtype=np.float32) / A_HEADS)).astype(np.float32)


def _attn_bias(dil):
    qi = np.arange(A_SUB)[:, None]
    ki = np.arange(2 * A_SUB)[None, :]
    j = qi + A_SUB - ki
    band = (j >= 0) & (j <= A_SUB)
    bias = -_alibi_slopes()[:, None, None] * (j * dil).astype(np.float32)[None]
    return jnp.asarray(np.where(band[None], bias, np.float32(NEG_BIG)).astype(np.float32))


def _attn_body(q_ref, kp_ref, kc_ref, vp_ref, vc_ref, bias_ref, o_ref, lse_ref, kk, vv, *, n_sub):
    qb = A_SUB
    kk[0:A_SUB, :] = kp_ref[...]
    vv[0:A_SUB, :] = vp_ref[...]
    kk[A_SUB:, :] = kc_ref[...]
    vv[A_SUB:, :] = vc_ref[...]
    lane = lax.broadcasted_iota(jnp.int32, (qb, LANES), 1)
    low = lane < A_HEAD_DIM
    low_f = low.astype(F32)
    head_lanes = (low_f.astype(BF16), (1.0 - low_f).astype(BF16))
    first = jnp.where(pl.program_id(1) > 0, 0, A_SUB)
    key_ok = lax.broadcasted_iota(jnp.int32, (qb, 2 * A_SUB), 1) >= first
    for i in range(n_sub):
        rows = slice(i * qb, (i + 1) * qb)
        krows = slice(i * qb, i * qb + 2 * A_SUB)
        lse_tile = jnp.zeros((qb, LANES), F32)
        for hp in range(A_HEADS // 2):
            cols = slice(hp * LANES, (hp + 1) * LANES)
            q2 = q_ref[rows, cols]
            k2 = kk[krows, cols]
            v2 = vv[krows, cols]
            outs = []
            for hh in range(2):
                h = 2 * hp + hh
                s = _dot_nt(q2 * head_lanes[hh], k2) + bias_ref[h]
                if i == 0:
                    s = jnp.where(key_ok, s, NEG_BIG)
                mx = jnp.max(s, axis=-1, keepdims=True)
                p = jnp.exp(s - mx)
                den = jnp.sum(p, axis=-1, keepdims=True)
                outs.append(_dot(p.astype(BF16), v2) / den)
                lse_tile = jnp.where(lane == h, mx + jnp.log(den), lse_tile)
            o_ref[rows, cols] = jnp.where(low, outs[0], outs[1]).astype(o_ref.dtype)
        lse_ref[rows, :] = lse_tile


def _attn_prompt(qkv, dil):
    _, tv, _ = qkv.shape
    assert qkv.shape[0] == dil and tv % A_SUB == 0
    bq = min(ATT_ROWS, tv)
    assert tv % bq == 0 and bq % A_SUB == 0
    cur = lambda part: pl.BlockSpec((None, bq, A_WIDTH), lambda r, n: (r, n, part))
    prev = lambda part: pl.BlockSpec((None, A_SUB, A_WIDTH),
                                     lambda r, n: (r, jnp.maximum(n * (bq // A_SUB) - 1, 0), part))
    return pl.pallas_call(
        functools.partial(_attn_body, n_sub=bq // A_SUB),
        grid=(dil, tv // bq),
        in_specs=[cur(0), prev(1), cur(1), prev(2), cur(2), _const_spec((A_HEADS, A_SUB, 2 * A_SUB))],
        out_specs=[cur(0), pl.BlockSpec((None, bq, LANES), lambda r, n: (r, n, 0))],
        out_shape=[jax.ShapeDtypeStruct((dil, tv, A_WIDTH), BF16), jax.ShapeDtypeStruct((dil, tv, LANES), F32)],
        scratch_shapes=[pltpu.VMEM((A_SUB + bq, A_WIDTH), BF16), pltpu.VMEM((A_SUB + bq, A_WIDTH), BF16)],
        compiler_params=_params(2),
        name=f"attn_prompt_d{dil}",
    )(qkv, qkv, qkv, qkv, qkv, _attn_bias(dil))


def _decode_tables(n_past):
    dist = n_past - np.arange(n_past)
    mult = np.zeros(n_past, np.float32)
    for window, dil in A_CONFIGS:
        mult += ((dist % dil == 0) & (dist // dil <= window // dil)).astype(np.float32)
    bias = -_alibi_slopes()[:, None] * dist.astype(np.float32)[None, :]
    return jnp.asarray(bias), jnp.asarray(mult[None, :])


def _attn_decode_body(q_ref, kn_ref, vn_ref, kc_ref, vc_ref, bias_ref, mult_ref, o_ref):
    q_t = q_ref[...]
    s_new = jnp.sum(q_t * kn_ref[...], axis=0, keepdims=True)
    mult = mult_ref[...]
    live = mult > 0.0
    n_cfg = float(len(A_CONFIGS))
    lane = lax.broadcasted_iota(jnp.int32, q_t.shape, 1)
    out = jnp.zeros(q_t.shape, F32)
    for h in range(A_HEADS):
        s = jnp.sum(kc_ref[h] * q_t[:, h:h + 1], axis=0, keepdims=True) + bias_ref[h:h + 1, :]
        s = jnp.where(live, s, NEG_BIG)
        s0 = s_new[:, h:h + 1]
        mx = jnp.maximum(jnp.max(s, axis=-1, keepdims=True), s0)
        p = jnp.exp(s - mx) * mult
        p0 = jnp.exp(s0 - mx) * n_cfg
        den = jnp.sum(p, axis=-1, keepdims=True) + p0
        acc = jnp.sum(vc_ref[h] * p, axis=-1, keepdims=True) + p0 * vn_ref[:, h:h + 1]
        out = jnp.where(lane == h, acc / den, out)
    o_ref[...] = out


def _attn_decode_operands(q_t, kn_t, vn_t, k_cache_t, v_cache_t, layer):
    nb = q_t.shape[0]
    n_past = k_cache_t.shape[-1]
    assert n_past == A_MAX_WINDOW
    bias, mult = _decode_tables(n_past)
    new = pl.BlockSpec((None, A_HEAD_DIM, A_HEADS), lambda b: (b, 0, 0))
    past = pl.BlockSpec((None, None, A_HEADS, A_HEAD_DIM, n_past), lambda b: (layer, b, 0, 0, 0))
    return ([q_t, kn_t, vn_t, k_cache_t, v_cache_t, bias, mult],
            [new, new, new, past, past, _const_spec(bias.shape), _const_spec(mult.shape)],
            new, jax.ShapeDtypeStruct((nb, A_HEAD_DIM, A_HEADS), F32))


def _attn_decode(*side):
    args, in_specs, out_spec, out_shape = _attn_decode_operands(*side)
    return pl.pallas_call(
        _attn_decode_body,
        grid=(out_shape.shape[0],),
        in_specs=in_specs,
        out_specs=out_spec,
        out_shape=out_shape,
        compiler_params=_params(1),
        name="attn_decode",
    )(*args)


def _rec_tables(c, single, stacked_heads):
    nl = 0 if single else int(math.log2(c))
    assert single or 2 ** nl == c
    t = np.arange(c)[:, None]
    i = np.arange(c)[None, :]
    e_rows, masks = [], []
    for lvl in range(nl):
        half = c >> (lvl + 1)
        mid = (t // (2 * half)) * (2 * half) + half
        right = t >= mid
        e_rows.append(np.where(right, (i >= mid) & (i <= t), (i > t) & (i < mid)))
        same = (t // (2 * half)) == (i // (2 * half))
        masks.append(same & right & (i < mid))
    masks.append(t == i)
    e_rows.append(i <= t)
    e_rows.append(i > t)
    e = np.concatenate(e_rows, 0).astype(np.float32)
    m = np.stack(masks, 0).astype(np.float32)
    if stacked_heads > 1:
        m = np.tile(m, (1, stacked_heads, 1))
    return jnp.asarray(np.concatenate([e, e], 1), dtype=BF16), jnp.asarray(m)


def _rec_core(q, k, lf, v, e_ref, m_ref, st_ref, o_ref, *, c, kind, single):
    nl = 0 if single else int(math.log2(c))
    n_win = q.shape[1] // LANES
    win = lambda x, w: x[:, w * LANES:(w + 1) * LANES]
    if single:
        e_cum = jnp.exp(lf)
        ke_f = k
    else:
        l1 = lf.astype(BF16)
        l2 = (lf - l1.astype(F32)).astype(BF16)
        decay = jnp.exp(_dot(e_ref[...], jnp.concatenate([l1, l2], axis=0)))
        seg = lambda n: decay[n * c:(n + 1) * c]
        e_cum = seg(nl)
        ke_f = k * seg(nl + 1)
    yield
    row = lax.broadcasted_iota(jnp.int32, (c, LANES), 0)

    def level_operand(n, w):
        half = c >> (n + 1)
        qw, kw = win(q, w), win(k, w)
        if half >= 8:
            pieces = [(qw if j % 2 else kw)[j * half:(j + 1) * half] for j in range(c // half)]
            side = jnp.concatenate(pieces, axis=0)
        else:
            side = jnp.where((row & half) != 0, qw, kw)
        return (side * win(seg(n), w)).astype(BF16)

    u = [[level_operand(n, w) for w in range(n_win)] for n in range(nl)]
    uq = [win(q, w).astype(BF16) for w in range(n_win)]
    uk = [win(k, w).astype(BF16) for w in range(n_win)]
    qg = (q * e_cum).astype(BF16)
    ke = ke_f.astype(BF16)
    e_tot = e_cum[0:1, :] if single else e_cum[c - 1:c, :]
    vb = v.astype(BF16)
    yield
    lane = lax.broadcasted_iota(jnp.int32, (c, LANES), 1)
    low = lane < B_VAL_DIM
    if kind == "c":
        head = lax.broadcasted_iota(jnp.int32, (1, LANES), 1) // C_KEY_DIM
        hm = [(head == h).astype(F32).astype(BF16) for h in range(C_HEADS)]
        stack = lambda x: jnp.concatenate([x * hm[h] for h in range(C_HEADS)], axis=0)
        att_all = _dot_nt(stack(uq[0]), uk[0]) * m_ref[nl]
        for n in range(nl):
            att_all = att_all + _dot_nt(stack(u[n][0]), u[n][0]) * m_ref[n]
        att_all = att_all.astype(BF16)
        atts = [att_all[h * c:(h + 1) * c] for h in range(C_HEADS)]
    else:
        atts = []
        for h in range(B_HEADS):
            att = _dot_nt(uq[h], uk[h]) * m_ref[nl]
            for n in range(nl):
                att = att + _dot_nt(u[n][h], u[n][h]) * m_ref[n]
            atts.append(att.astype(BF16))
    intra = [jnp.where(low, _dot(atts[2 * pair], win(vb, pair)), _dot(atts[2 * pair + 1], win(vb, pair)))
             for pair in range(2)]
    half_rows = [slice(hh * B_VAL_DIM, (hh + 1) * B_VAL_DIM) for hh in range(2)]
    if kind == "c":
        srow = lax.broadcasted_iota(jnp.int32, (LANES, LANES), 0) // B_VAL_DIM
        slane = lax.broadcasted_iota(jnp.int32, (LANES, LANES), 1) // C_KEY_DIM
        added = [_dot_tn(win(vb, pair), ke) * (slane == srow + 2 * pair).astype(F32) for pair in range(2)]
    else:
        added = [jnp.concatenate([_dot_tn(win(vb, pair), win(ke, 2 * pair + hh))[half_rows[hh], :]
                                  for hh in range(2)], axis=0) for pair in range(2)]
    yield
    for pair in range(2):
        st = st_ref[pair]
        stb = st.astype(BF16)
        if kind == "c":
            o_ref[:, pair * LANES:(pair + 1) * LANES] = intra[pair] + _dot_nt(qg, stb)
            st_ref[pair] = st * e_tot + added[pair]
            continue
        carried = [_dot_nt(win(qg, 2 * pair + hh), stb) for hh in range(2)]
        o_ref[:, pair * LANES:(pair + 1) * LANES] = intra[pair] + jnp.where(low, carried[0], carried[1])
        keep = jnp.concatenate([st[half_rows[hh], :] * win(e_tot, 2 * pair + hh) for hh in range(2)], axis=0)
        st_ref[pair] = keep + added[pair]


_REC_STAGES = 4
_REC_KINDS = ("b", "c")
_REC_N_PARAMS = {"b": 1, "c": 2}


def _rec_body(*refs, c, n_valid, n_b, n_sub):
    ins, pos = {}, 0
    for kind in _REC_KINDS:
        n = 4 + _REC_N_PARAMS[kind]
        ins[kind] = refs[pos:pos + n]
        pos += n
    outs = {kind: refs[pos + 2 * i:pos + 2 * i + 2] for i, kind in enumerate(_REC_KINDS)}
    pos += 2 * len(_REC_KINDS)
    states = {kind: refs[pos + i] for i, kind in enumerate(_REC_KINDS)}

    @pl.when(pl.program_id(1) == 0)
    def _():
        for kind in _REC_KINDS:
            states[kind][...] = ins[kind][-1][...]

    chunks = []
    for bb in range(n_b):
        for j in range(n_sub):
            for kind in _REC_KINDS:
                z_ref, *params, e_ref, m_ref, _ = ins[kind]
                chunks.append(_rec_chunk(z_ref.at[bb, pl.ds(j * c, c)], params, e_ref, m_ref, states[kind].at[bb],
                                         outs[kind][0].at[bb, pl.ds(j * c, c)], kind=kind, c=c, n_valid=n_valid))
    for _ in range(_REC_STAGES - 1):
        for chunk in chunks:
            next(chunk)
    for chunk in chunks:
        for _ in chunk:
            raise AssertionError("more stages than _REC_STAGES")

    @pl.when(pl.program_id(1) == pl.num_programs(1) - 1)
    def _():
        for kind in _REC_KINDS:
            outs[kind][1][...] = states[kind][...]


def _rec_chunk(z_ref, params, e_ref, m_ref, st_ref, o_ref, *, kind, c, n_valid):
    z = z_ref[...]
    if kind == "b":
        lb = params[0][...]
        f = lb + (1.0 - lb) * _sigmoid(z[:, B_KW:2 * B_KW])
        lf = jnp.log(jnp.maximum(f, MIN_F))
        k = 1.0 - f
        zq = z[:, 0:B_KW]
        q = zq * _sigmoid(zq)
        v = z[:, 2 * B_KW:2 * B_KW + B_WIDTH]
    else:
        cwg_ref, cb_ref = params
        q = z[:, 0:C_KW]
        k = z[:, C_KW:2 * C_KW]
        v = z[:, 2 * C_KW:2 * C_KW + C_WIDTH]
        x = _dot(z[:, ZC_WIDTH - LANES:].astype(BF16), cwg_ref[...]) + cb_ref[...]
        lf = (jnp.minimum(x, 0.0) - jnp.log1p(jnp.exp(-jnp.abs(x)))) / C_GATE_TEMP
    if n_valid < c:
        live = lax.broadcasted_iota(jnp.int32, (c, 1), 0) < n_valid
        q = jnp.where(live, q, 0.0)
        k = jnp.where(live, k, 0.0)
        lf = jnp.where(live, lf, 0.0)
    yield from _rec_core(q, k, lf, v, e_ref, m_ref, st_ref, o_ref, c=c, kind=kind, single=n_valid == 1)


def _recurrences(zs, layer, params, s0s, s0_layer, n_valid=None):
    nb, t, _ = zs["b"].shape
    c = min(REC_CHUNK, t)
    assert t % c == 0
    n_valid = c if n_valid is None else n_valid
    assert n_valid in (1, c)
    n_sub = math.gcd(t // c, REC_CHUNKS_PER_STEP)
    n_b = math.gcd(nb, REC_SEQS_PER_STEP)
    st_spec = pl.BlockSpec((n_b, 2, LANES, LANES), lambda b, j: (b, 0, 0, 0))
    s0_spec = pl.BlockSpec((None, n_b, 2, LANES, LANES), lambda b, j: (s0_layer, b, 0, 0, 0))
    args, in_specs = [], []
    for kind in _REC_KINDS:
        assert len(params[kind]) == _REC_N_PARAMS[kind] and zs[kind].shape[:2] == (nb, t)
        e, m = _rec_tables(c, n_valid == 1, C_HEADS if kind == "c" else 1)
        args += [zs[kind], *params[kind], e, m, s0s[kind]]
        in_specs += [pl.BlockSpec((n_b, n_sub * c, zs[kind].shape[2]), lambda b, j: (b, j, 0))]
        in_specs += [_layer_spec(x, layer) for x in params[kind]]
        in_specs += [_const_spec(e.shape), _const_spec(m.shape), s0_spec]
    n_kinds = len(_REC_KINDS)
    res = pl.pallas_call(
        functools.partial(_rec_body, c=c, n_valid=n_valid, n_b=n_b, n_sub=n_sub),
        grid=(nb // n_b, t // (c * n_sub)),
        in_specs=in_specs,
        out_specs=[pl.BlockSpec((n_b, n_sub * c, 2 * LANES), lambda b, j: (b, j, 0)), st_spec] * n_kinds,
        out_shape=[jax.ShapeDtypeStruct((nb, t, 2 * LANES), F32),
                   jax.ShapeDtypeStruct((nb, 2, LANES, LANES), F32)] * n_kinds,
        scratch_shapes=[pltpu.VMEM((n_b, 2, LANES, LANES), F32)] * n_kinds,
        compiler_params=_params(2),
        name="recurrences",
    )(*args)
    return {kind: (res[2 * i], res[2 * i + 1]) for i, kind in enumerate(_REC_KINDS)}


def _mix_body(*refs, tm, dils):
    h_ref = refs[0]
    n_att = 2 * len(dils) if dils else 1
    att_refs = refs[1:1 + n_att]
    ob_ref, bg_ref, oc_ref, cg_ref, bn_ref, cn_ref, ex_ref, bd_ref, w_ref, out_ref, mixed = refs[1 + n_att:12 + n_att]
    if dils:
        o_refs, l_refs = att_refs[:len(dils)], att_refs[len(dils):]
        o_nat, l_nat = refs[12 + n_att], refs[13 + n_att]
        n_cb = A_WIDTH // LANES
        lses = []
        for c, d in enumerate(dils):
            if d == 1:
                lses.append(l_refs[c][0])
                continue
            for r in range(d):
                l_nat[c, pl.ds(r, tm // d, stride=d), :] = l_refs[c][r]
                for cb in range(n_cb):
                    o_nat[c, cb, pl.ds(r, tm // d, stride=d), :] = (
                        o_refs[c][r, :, cb * LANES:(cb + 1) * LANES].astype(F32))
            lses.append(l_nat[c])
        mx = functools.reduce(jnp.maximum, lses)
        es = [jnp.exp(l - mx) for l in lses]
        den = functools.reduce(lambda a, b: a + b, es)
        ex = ex_ref[...]
        wts = [_dot_split2(e / den, ex) for e in es]
        for cb in range(n_cb):
            cs = slice(cb * LANES, (cb + 1) * LANES)
            terms = [wts[c][:, cs] * (o_refs[c][0, :, cs] if d == 1 else o_nat[c, cb]) for c, d in enumerate(dils)]
            mixed[:, cs] = functools.reduce(lambda a, b: a + b, terms).astype(BF16)
    else:
        mixed[:, 0:A_WIDTH] = att_refs[0][...].astype(BF16)
    bd = bd_ref[...]
    ob = ob_ref[...]
    obn = ob * lax.rsqrt(_dot_split2(ob * ob, bd) + RMS_EPS) * bn_ref[...]
    mixed[:, A_WIDTH:A_WIDTH + B_WIDTH] = (obn * _sigmoid(bg_ref[...])).astype(BF16)
    oc = oc_ref[...]
    ocn = oc * lax.rsqrt(_dot_split2(oc * oc, bd) + RMS_EPS) * cn_ref[...]
    cg = cg_ref[...]
    mixed[:, A_WIDTH + B_WIDTH:] = (ocn * (cg * _sigmoid(cg))).astype(BF16)
    out_ref[...] = h_ref[...] + _dot(mixed[...], w_ref[...])


def _mix(h, layer, att, dils, ob, zb, oc, zc, b_norm, c_norm, w_out):
    m, d = h.shape
    tm = min(ROW_TILE, m)
    assert m % tm == 0
    row = lambda w, blk=0: pl.BlockSpec((tm, w), lambda i: (i, blk))
    if dils:
        att_specs = [pl.BlockSpec((dd, tm // dd, wd), lambda i: (0, i, 0))
                     for wd in (A_WIDTH, LANES) for dd in dils]
        att_scratch = [pltpu.VMEM((len(dils), A_WIDTH // LANES, tm, LANES), F32),
                       pltpu.VMEM((len(dils), tm, LANES), F32)]
    else:
        att_specs, att_scratch = [row(A_WIDTH)], []
    expand = np.zeros((LANES, A_WIDTH), np.float32)
    for hd in range(A_HEADS):
        expand[hd, hd * A_HEAD_DIM:(hd + 1) * A_HEAD_DIM] = 1.0
    expand = np.concatenate([expand, expand], axis=0)
    grp = np.arange(B_WIDTH) // B_VAL_DIM
    blockmean = (grp[:, None] == grp[None, :]).astype(np.float32) / B_VAL_DIM
    blockmean = np.concatenate([blockmean, blockmean], axis=0)
    d_mix = A_WIDTH + B_WIDTH + C_WIDTH
    return pl.pallas_call(
        functools.partial(_mix_body, tm=tm, dils=tuple(dils)),
        grid=(m // tm,),
        in_specs=[row(d)] + att_specs
                 + [row(B_WIDTH), row(B_WIDTH, (2 * B_KW + B_WIDTH) // B_WIDTH),
                    row(C_WIDTH), row(C_WIDTH, (2 * C_KW + C_WIDTH) // C_WIDTH)]
                 + [_layer_spec(b_norm, layer), _layer_spec(c_norm, layer), _const_spec(expand.shape),
                    _const_spec(blockmean.shape), _layer_spec(w_out, layer)],
        out_specs=row(d),
        out_shape=jax.ShapeDtypeStruct((m, d), F32),
        scratch_shapes=[pltpu.VMEM((tm, d_mix), BF16)] + att_scratch,
        compiler_params=_params(1),
        name="mix_out",
    )(h, *att, ob, zb, oc, zc, b_norm, c_norm,
      jnp.asarray(expand, dtype=BF16), jnp.asarray(blockmean, dtype=BF16), w_out)


def _state_to_pairs(s):
    nl, nb, nh, dk, dv = s.shape
    st = jnp.swapaxes(s, 3, 4)
    if dk < LANES:
        st = jnp.stack([jnp.pad(st[:, :, h], ((0, 0), (0, 0), (0, 0), (h * dk, LANES - (h + 1) * dk)))
                        for h in range(nh)], 2)
    return st.reshape(nl, nb, 2, 2 * dv, LANES)


def _pairs_to_state(st, dk):
    nb = st.shape[0]
    st = st.reshape(nb, 4, B_VAL_DIM, LANES)
    if dk < LANES:
        st = jnp.stack([st[:, h, :, h * dk:(h + 1) * dk] for h in range(4)], 1)
    return jnp.swapaxes(st, 2, 3)


def _layer_tail(h, i, w, lb_all, p, final_norm, zb, zc, att, dils, states, s0_layer, decode):
    depth = w["w_in"].shape[0]
    if decode:
        nb = h.shape[0]
        pad = lambda a: jnp.pad(a.reshape(nb, 1, -1), ((0, 0), (0, DEC_PAD - 1), (0, 0)))
        zb3, zc3, n_valid = pad(zb), pad(zc), 1
    else:
        zb3, zc3, n_valid = zb[None], zc[None], None
    rec = _recurrences(dict(b=zb3, c=zc3), i, dict(b=[lb_all], c=[w["c_w_gate"], w["c_gate_bias"]]),
                       states, s0_layer, n_valid)
    (ob, sb), (oc, sc) = rec["b"], rec["c"]
    ob, oc = (ob[:, 0], oc[:, 0]) if decode else (ob[0], oc[0])
    h = _mix(h, i, att, dils, ob, zb, oc, zc, w["b_out_norm"], w["c_out_norm"], w["w_out"])
    ple = (p, w["ple_norm"], w["ple_w_gate"], w["ple_w_proj"])
    last = final_norm if i == depth - 1 else None
    h, *y = _ffn(h, i, w["ffn2_norm"], w["ffn2_w_gate"], w["ffn2_w_up"], w["ffn2_w_down"], ple, last)
    return h, (y[0] if y else None), _pairs_to_state(sb, B_KEY_DIM), _pairs_to_state(sc, C_KEY_DIM)


def kernel(x_prompt, x_sample, cache_k_a, cache_v_a, state_b, state_c, p_prompt, p_sample, ffn1_norm, ffn1_w_gate, ffn1_w_up, ffn1_w_down, mix_norm, w_in, lb_logits, b_out_norm, c_w_gate, c_gate_bias, c_out_norm, w_out, ffn2_norm, ffn2_w_gate, ffn2_w_up, ffn2_w_down, ple_norm, ple_w_gate, ple_w_proj, final_norm):
    depth = w_in.shape[0]
    rows = lambda a: a.reshape(depth, 1, -1)
    w = {name: a.astype(BF16) for name, a in dict(
        ffn1_w_gate=ffn1_w_gate, ffn1_w_up=ffn1_w_up, ffn1_w_down=ffn1_w_down, w_out=w_out, ffn2_w_gate=ffn2_w_gate,
        ffn2_w_up=ffn2_w_up, ffn2_w_down=ffn2_w_down, ple_w_gate=ple_w_gate, ple_w_proj=ple_w_proj).items()}
    w.update(
        ffn1_norm=rows(ffn1_norm), mix_norm=rows(mix_norm), w_in=jnp.swapaxes(w_in, 1, 2).astype(BF16),
        b_out_norm=rows(b_out_norm),
        c_w_gate=jnp.pad(c_w_gate, ((0, 0), (0, LANES - C_GATE_RANK), (0, 0))).astype(BF16),
        c_gate_bias=rows(c_gate_bias), c_out_norm=rows(c_out_norm), ffn2_norm=rows(ffn2_norm), ple_norm=rows(ple_norm))
    sm = jax.nn.softmax(lb_logits.astype(F32), axis=0)
    lb_all = jnp.maximum(jnp.cumsum(sm, axis=0) - sm[0], 0.0).reshape(depth, 1, B_KW)
    fn = final_norm.reshape(1, -1)

    (one, t, d), (nb, one_tok, _) = x_prompt.shape, x_sample.shape
    assert one == 1 and one_tok == 1
    hp, hs = x_prompt.reshape(t, d), x_sample.reshape(nb, d)
    pp, ps = p_prompt.reshape(depth, t, -1), p_sample.reshape(depth, nb, -1)
    keep = min(A_MAX_WINDOW, t)
    k_cache_t = jnp.transpose(cache_k_a, (0, 1, 3, 4, 2))
    v_cache_t = jnp.transpose(cache_v_a, (0, 1, 3, 4, 2))
    states_s = dict(b=_state_to_pairs(state_b), c=_state_to_pairs(state_c))
    states_p = dict(b=jnp.zeros((1, 1, 2, LANES, LANES), F32), c=jnp.zeros((1, 1, 2, LANES, LANES), F32))
    dils = tuple(dil for _, dil in A_CONFIGS)
    assert dils[0] == 1
    cols = lambda a: jnp.swapaxes(a.reshape(nb, A_HEADS, A_HEAD_DIM), 1, 2)
    ffn1 = lambda h, i, side=None: _ffn(h, i, w["ffn1_norm"], w["ffn1_w_gate"], w["ffn1_w_up"], w["ffn1_w_down"],
                                        side=side)
    outs = {key: [] for key in ("k_p", "v_p", "sb_p", "sc_p", "k_s", "v_s", "sb_s", "sc_s")}
    for i in range(depth):
        hs, = ffn1(hs, i)
        qkv_s, kf_s, vf_s, zb_s, zc_s = _proj(hs, i, w["mix_norm"], w["w_in"], nb, (), F32)
        side = (cols(qkv_s[0, :, 0:A_WIDTH]), cols(kf_s), cols(vf_s), k_cache_t, v_cache_t, i)
        if t // min(ROW_TILE, t) == nb:
            hp, o_t = ffn1(hp, i, side)
        else:
            hp, = ffn1(hp, i)
            o_t = _attn_decode(*side)
        *qkvs, kf_p, vf_p, zb_p, zc_p = _proj(hp, i, w["mix_norm"], w["w_in"], keep, dils[1:], BF16)
        res = [_attn_prompt(a, dil) for a, dil in zip(qkvs, dils)]
        att_p = [r[0] for r in res] + [r[1] for r in res]
        hp, y_p, sb, sc = _layer_tail(hp, i, w, lb_all, pp, fn, zb_p, zc_p, att_p, dils, states_p, 0, False)
        outs["k_p"].append(kf_p.reshape(1, keep, A_HEADS, A_HEAD_DIM))
        outs["v_p"].append(vf_p.reshape(1, keep, A_HEADS, A_HEAD_DIM))
        outs["sb_p"].append(sb)
        outs["sc_p"].append(sc)
        att_s = [jnp.swapaxes(o_t, 1, 2).reshape(nb, A_WIDTH)]
        hs, y_s, sb, sc = _layer_tail(hs, i, w, lb_all, ps, fn, zb_s, zc_s, att_s, (), states_s, i, True)
        outs["k_s"].append(kf_s.reshape(nb, 1, A_HEADS, A_HEAD_DIM))
        outs["v_s"].append(vf_s.reshape(nb, 1, A_HEADS, A_HEAD_DIM))
        outs["sb_s"].append(sb)
        outs["sc_s"].append(sc)
    st = {key: jnp.stack(val) for key, val in outs.items()}
    return (y_p.reshape(1, t, d), y_s.reshape(nb, 1, d), st["k_p"], st["v_p"], st["sb_p"], st["sc_p"],
            st["k_s"], st["v_s"], st["sb_s"], st["sc_s"])
```

```python
import functools
import math

import numpy as np
import jax
import jax.numpy as jnp
from jax import lax
from jax.experimental import pallas as pl
from jax.experimental.pallas import tpu as pltpu

F32 = jnp.float32
BF16 = jnp.bfloat16

RMS_EPS = 1e-6
NEG_BIG = -1e30
MIN_F = 1e-30
A_HEADS = 8
A_HEAD_DIM = 64
A_CONFIGS = ((128, 1), (512, 4), (2048, 16))
A_MAX_WINDOW = 2048
A_SUB = 128
B_HEADS = 4
B_KEY_DIM = 128
B_VAL_DIM = 64
C_HEADS = 4
C_KEY_DIM = 32
C_VAL_DIM = 64
C_GATE_RANK = 16
C_GATE_TEMP = 16.0
A_WIDTH = A_HEADS * A_HEAD_DIM
B_KW = B_HEADS * B_KEY_DIM
B_WIDTH = B_HEADS * B_VAL_DIM
C_KW = C_HEADS * C_KEY_DIM
C_WIDTH = C_HEADS * C_VAL_DIM

LANES = 128
MXU_WIDTH = 256

ZB_WIDTH = 2 * B_KW + 2 * B_WIDTH
ZC_WIDTH = 2 * C_KW + 2 * C_WIDTH + LANES
VMEM_LIMIT_BYTES = 56 * 1024 * 1024

ROW_TILE = 512
MIX_ROW_TILE = 1024
FF_CHUNK = 1536
REC_CHUNK = 128
REC_CHUNKS_PER_STEP = 4
REC_SEQS_PER_STEP = 8
ATT_ROWS = 1024
DEC_PAD = 16


def _params(n_axes):
    return pltpu.CompilerParams(dimension_semantics=("arbitrary",) * n_axes,
                                vmem_limit_bytes=VMEM_LIMIT_BYTES)


def _const_spec(shape):
    zeros = (0,) * len(shape)
    return pl.BlockSpec(shape, lambda *_: zeros, pipeline_mode=pl.Buffered(1))


def _layer_spec(stacked, layer):
    shape = tuple(stacked.shape[1:])
    zeros = (0,) * len(shape)
    return pl.BlockSpec((None,) + shape, lambda *_: (layer,) + zeros, pipeline_mode=pl.Buffered(1))


def _dot(a, b):
    return jnp.dot(a, b, preferred_element_type=F32)


def _dot_nt(a, b):
    return lax.dot_general(a, b, (((1,), (1,)), ((), ())), preferred_element_type=F32)


def _dot_tn(a, b):
    return lax.dot_general(a, b, (((0,), (0,)), ((), ())), preferred_element_type=F32)


def _dot_split2(x, w):
    hi = x.astype(BF16)
    lo = (x - hi.astype(F32)).astype(BF16)
    if w.shape[0] == 2 * x.shape[1]:
        return _dot(jnp.concatenate([hi, lo], axis=1), w)
    return _dot(hi, w) + _dot(lo, w)


def _rms(x, g):
    ms = jnp.mean(x * x, axis=-1, keepdims=True)
    return x * lax.rsqrt(ms + RMS_EPS) * g


def _sigmoid(x):
    return 1.0 / (1.0 + jnp.exp(-x))


def _ff_chunks(dff):
    unit = MXU_WIDTH if dff % MXU_WIDTH == 0 else LANES
    assert dff % unit == 0
    tiles = dff // unit
    n = -(-dff // FF_CHUNK)
    sizes = [(tiles // n + (1 if i < tiles % n else 0)) * unit for i in range(n)]
    starts = np.cumsum([0] + sizes[:-1]).tolist()
    return tuple(zip(starts, sizes))


def _ffn_body(*refs, chunks, with_ple, with_final, n_side_in):
    refs = list(refs)
    if n_side_in:
        _attn_decode_body(*refs[:n_side_in], refs[-1])
        refs = refs[n_side_in:-1]
    h_ref, nrm_ref, wg_ref, wu_ref, wd_ref = refs[:5]
    pos = 5
    if with_ple:
        p_ref, pn_ref, pwg_ref, pwp_ref = refs[pos:pos + 4]
        pos += 4
    if with_final:
        fn_ref = refs[pos]
        pos += 1
    o_ref = refs[pos]
    h = h_ref[...]
    xn = _rms(h, nrm_ref[...]).astype(BF16)
    acc = None
    for lo, chunk in chunks:
        g = _dot(xn, wg_ref[:, lo:lo + chunk])
        u = _dot(xn, wu_ref[:, lo:lo + chunk])
        a = (g * _sigmoid(g) * u).astype(BF16)
        d = _dot(a, wd_ref[lo:lo + chunk, :])
        acc = d if acc is None else acc + d
    h = h + 0.5 * acc
    if with_ple:
        hn = _rms(h, pn_ref[...]).astype(BF16)
        gate = _sigmoid(_dot(hn, pwg_ref[...]))
        proj = _dot(p_ref[...].astype(BF16), pwp_ref[...])
        h = h + gate * proj
    o_ref[...] = h
    if with_final:
        refs[pos + 1][...] = _rms(h, fn_ref[...])


def _ffn(h, layer, nrm, wg, wu, wd, ple=None, final_norm=None, side=None):
    m, d = h.shape
    dff = wg.shape[-1]
    tm = min(ROW_TILE, m)
    assert m % tm == 0
    row = lambda w: pl.BlockSpec((tm, w), lambda i: (i, 0))
    ins = [h, nrm, wg, wu, wd]
    specs = [row(d)] + [_layer_spec(a, layer) for a in ins[1:]]
    if ple is not None:
        ins += list(ple)
        specs += [pl.BlockSpec((None, tm, ple[0].shape[-1]), lambda i: (layer, i, 0))]
        specs += [_layer_spec(a, layer) for a in ple[1:]]
    if final_norm is not None:
        ins.append(final_norm)
        specs.append(_const_spec((1, d)))
    n_out = 2 if final_norm is not None else 1
    out_specs = [row(d)] * n_out
    out_shape = [jax.ShapeDtypeStruct((m, d), F32)] * n_out
    n_side_in = 0
    if side is not None:
        side_args, side_specs, side_out_spec, side_out_shape = _attn_decode_operands(*side)
        assert side_out_shape.shape[0] == m // tm
        ins, specs, n_side_in = side_args + ins, side_specs + specs, len(side_args)
        out_specs, out_shape = out_specs + [side_out_spec], out_shape + [side_out_shape]
    return pl.pallas_call(
        functools.partial(_ffn_body, chunks=_ff_chunks(dff), with_ple=ple is not None,
                          with_final=final_norm is not None, n_side_in=n_side_in),
        grid=(m // tm,),
        in_specs=specs,
        out_specs=out_specs,
        out_shape=out_shape,
        compiler_params=_params(1),
        name="ffn",
    )(*ins)


def _proj_body(h_ref, nrm_ref, w_ref, *refs, tm, dils):
    n_a = 1 + len(dils)
    a_refs = refs[:n_a]
    kf_ref, vf_ref, zb_ref, zc_ref = refs[n_a:n_a + 4]
    xn = _rms(h_ref[...], nrm_ref[...]).astype(BF16)
    a = A_WIDTH
    b0 = 3 * a
    for part in range(3):
        cols = slice(part * a, (part + 1) * a)
        z = _dot_nt(xn, w_ref[cols, :])
        if part == 0:
            z = z * (A_HEAD_DIM ** -0.5)
        elif part == 1:
            kf_ref[...] = z
        else:
            vf_ref[...] = z
        a_refs[0][0, :, cols] = z.astype(a_refs[0].dtype)
        if dils:
            stage = refs[-1]
            for cb in range(a // LANES):
                stage[cb] = z[:, cb * LANES:(cb + 1) * LANES]
        for ref, d in zip(a_refs[1:], dils):
            for r in range(d):
                for cb in range(a // LANES):
                    lo = part * a + cb * LANES
                    ref[r, :, lo:lo + LANES] = stage[cb, pl.ds(r, tm // d, stride=d), :].astype(ref.dtype)
    zb_ref[...] = _dot_nt(xn, w_ref[b0:b0 + ZB_WIDTH, :])
    c0 = b0 + ZB_WIDTH
    qkv_c = 2 * C_KW + C_WIDTH
    zc = _dot_nt(xn, w_ref[c0:c0 + qkv_c, :])
    zc_ref[:, 0:C_KW] = zc[:, 0:C_KW] * (C_KEY_DIM ** -0.5)
    zc_ref[:, C_KW:qkv_c] = zc[:, C_KW:]
    r0 = c0 + qkv_c
    zc_ref[:, qkv_c:qkv_c + C_WIDTH] = _dot_nt(xn, w_ref[r0 + C_GATE_RANK:r0 + C_GATE_RANK + C_WIDTH, :])
    lane = lax.broadcasted_iota(jnp.int32, (tm, LANES), 1)
    rank = _dot_nt(xn, w_ref[r0:r0 + LANES, :])
    zc_ref[:, qkv_c + C_WIDTH:] = jnp.where(lane < C_GATE_RANK, rank, 0.0)


def _proj(h, layer, nrm, w, keep, dils, a_dtype):
    m, d = h.shape
    tm = min(ROW_TILE, m)
    assert m % tm == 0 and keep % tm == 0
    nt, nk = m // tm, keep // tm
    row = lambda wd: pl.BlockSpec((tm, wd), lambda i: (i, 0))
    tail = pl.BlockSpec((tm, A_WIDTH), lambda i: (jnp.maximum(i - (nt - nk), 0), 0))
    sds = jax.ShapeDtypeStruct
    all_d = (1,) + tuple(dils)
    return pl.pallas_call(
        functools.partial(_proj_body, tm=tm, dils=tuple(dils)),
        grid=(nt,),
        in_specs=[row(d), _layer_spec(nrm, layer), _layer_spec(w, layer)],
        out_specs=[pl.BlockSpec((dd, tm // dd, 3 * A_WIDTH), lambda i: (0, i, 0)) for dd in all_d]
                  + [tail, tail, row(ZB_WIDTH), row(ZC_WIDTH)],
        out_shape=[sds((dd, m // dd, 3 * A_WIDTH), a_dtype) for dd in all_d]
                  + [sds((keep, A_WIDTH), F32), sds((keep, A_WIDTH), F32),
                     sds((m, ZB_WIDTH), F32), sds((m, ZC_WIDTH), F32)],
        scratch_shapes=[pltpu.VMEM((A_WIDTH // LANES, tm, LANES), F32)] if dils else [],
        compiler_params=_params(1),
        name="proj",
    )(h, nrm, w)


def _alibi_slopes():
    return (2.0 ** (-8.0 * np.arange(1, A_HEADS + 1, dtype=np.float32) / A_HEADS)).astype(np.float32)


def _attn_bias(dil):
    qi = np.arange(A_SUB)[:, None]
    ki = np.arange(2 * A_SUB)[None, :]
    j = qi + A_SUB - ki
    band = (j >= 0) & (j <= A_SUB)
    bias = -_alibi_slopes()[:, None, None] * (j * dil).astype(np.float32)[None]
    return jnp.asarray(np.where(band[None], bias, np.float32(NEG_BIG)).astype(np.float32))


def _attn_body(q_ref, kp_ref, kc_ref, vp_ref, vc_ref, bias_ref, o_ref, lse_ref, kk, vv, *, n_sub):
    qb = A_SUB
    kk[0:A_SUB, :] = kp_ref[...]
    vv[0:A_SUB, :] = vp_ref[...]
    kk[A_SUB:, :] = kc_ref[...]
    vv[A_SUB:, :] = vc_ref[...]
    lane = lax.broadcasted_iota(jnp.int32, (qb, LANES), 1)
    low = lane < A_HEAD_DIM
    low_f = low.astype(F32)
    head_lanes = (low_f.astype(BF16), (1.0 - low_f).astype(BF16))
    first = jnp.where(pl.program_id(1) > 0, 0, A_SUB)
    key_ok = lax.broadcasted_iota(jnp.int32, (qb, 2 * A_SUB), 1) >= first
    for i in range(n_sub):
        rows = slice(i * qb, (i + 1) * qb)
        krows = slice(i * qb, i * qb + 2 * A_SUB)
        lse_tile = jnp.zeros((qb, LANES), F32)
        for hp in range(A_HEADS // 2):
            cols = slice(hp * LANES, (hp + 1) * LANES)
            q2 = q_ref[rows, cols]
            k2 = kk[krows, cols]
            v2 = vv[krows, cols]
            outs = []
            for hh in range(2):
                h = 2 * hp + hh
                s = _dot_nt(q2 * head_lanes[hh], k2) + bias_ref[h]
                if i == 0:
                    s = jnp.where(key_ok, s, NEG_BIG)
                mx = jnp.max(s, axis=-1, keepdims=True)
                p = jnp.exp(s - mx)
                den = jnp.sum(p, axis=-1, keepdims=True)
                outs.append(_dot(p.astype(BF16), v2) / den)
                lse_tile = jnp.where(lane == h, mx + jnp.log(den), lse_tile)
            o_ref[rows, cols] = jnp.where(low, outs[0], outs[1]).astype(o_ref.dtype)
        lse_ref[rows, :] = lse_tile


def _attn_prompt(qkv, dil):
    _, tv, _ = qkv.shape
    assert qkv.shape[0] == dil and tv % A_SUB == 0
    bq = min(ATT_ROWS, tv)
    assert tv % bq == 0 and bq % A_SUB == 0
    cur = lambda part: pl.BlockSpec((None, bq, A_WIDTH), lambda r, n: (r, n, part))
    prev = lambda part: pl.BlockSpec((None, A_SUB, A_WIDTH),
                                     lambda r, n: (r, jnp.maximum(n * (bq // A_SUB) - 1, 0), part))
    return pl.pallas_call(
        functools.partial(_attn_body, n_sub=bq // A_SUB),
        grid=(dil, tv // bq),
        in_specs=[cur(0), prev(1), cur(1), prev(2), cur(2), _const_spec((A_HEADS, A_SUB, 2 * A_SUB))],
        out_specs=[cur(0), pl.BlockSpec((None, bq, LANES), lambda r, n: (r, n, 0))],
        out_shape=[jax.ShapeDtypeStruct((dil, tv, A_WIDTH), BF16), jax.ShapeDtypeStruct((dil, tv, LANES), F32)],
        scratch_shapes=[pltpu.VMEM((A_SUB + bq, A_WIDTH), BF16), pltpu.VMEM((A_SUB + bq, A_WIDTH), BF16)],
        compiler_params=_params(2),
        name=f"attn_prompt_d{dil}",
    )(qkv, qkv, qkv, qkv, qkv, _attn_bias(dil))


def _decode_tables(n_past):
    dist = n_past - np.arange(n_past)
    mult = np.zeros(n_past, np.float32)
    for window, dil in A_CONFIGS:
        mult += ((dist % dil == 0) & (dist // dil <= window // dil)).astype(np.float32)
    bias = -_alibi_slopes()[:, None] * dist.astype(np.float32)[None, :]
    return jnp.asarray(bias), jnp.asarray(mult[None, :])


def _attn_decode_body(q_ref, kn_ref, vn_ref, kc_ref, vc_ref, bias_ref, mult_ref, o_ref):
    q_t = q_ref[...]
    s_new = jnp.sum(q_t * kn_ref[...], axis=0, keepdims=True)
    mult = mult_ref[...]
    live = mult > 0.0
    n_cfg = float(len(A_CONFIGS))
    lane = lax.broadcasted_iota(jnp.int32, q_t.shape, 1)
    out = jnp.zeros(q_t.shape, F32)
    for h in range(A_HEADS):
        s = jnp.sum(kc_ref[h] * q_t[:, h:h + 1], axis=0, keepdims=True) + bias_ref[h:h + 1, :]
        s = jnp.where(live, s, NEG_BIG)
        s0 = s_new[:, h:h + 1]
        mx = jnp.maximum(jnp.max(s, axis=-1, keepdims=True), s0)
        p = jnp.exp(s - mx) * mult
        p0 = jnp.exp(s0 - mx) * n_cfg
        den = jnp.sum(p, axis=-1, keepdims=True) + p0
        acc = jnp.sum(vc_ref[h] * p, axis=-1, keepdims=True) + p0 * vn_ref[:, h:h + 1]
        out = jnp.where(lane == h, acc / den, out)
    o_ref[...] = out


def _attn_decode_operands(q_t, kn_t, vn_t, k_cache_t, v_cache_t, layer):
    nb = q_t.shape[0]
    n_past = k_cache_t.shape[-1]
    assert n_past == A_MAX_WINDOW
    bias, mult = _decode_tables(n_past)
    new = pl.BlockSpec((None, A_HEAD_DIM, A_HEADS), lambda b: (b, 0, 0))
    past = pl.BlockSpec((None, None, A_HEADS, A_HEAD_DIM, n_past), lambda b: (layer, b, 0, 0, 0))
    return ([q_t, kn_t, vn_t, k_cache_t, v_cache_t, bias, mult],
            [new, new, new, past, past, _const_spec(bias.shape), _const_spec(mult.shape)],
            new, jax.ShapeDtypeStruct((nb, A_HEAD_DIM, A_HEADS), F32))


def _attn_decode(*side):
    args, in_specs, out_spec, out_shape = _attn_decode_operands(*side)
    return pl.pallas_call(
        _attn_decode_body,
        grid=(out_shape.shape[0],),
        in_specs=in_specs,
        out_specs=out_spec,
        out_shape=out_shape,
        compiler_params=_params(1),
        name="attn_decode",
    )(*args)


def _rec_tables(c, single, stacked_heads):
    nl = 0 if single else int(math.log2(c))
    assert single or 2 ** nl == c
    t = np.arange(c)[:, None]
    i = np.arange(c)[None, :]
    e_rows, masks = [], []
    for lvl in range(nl):
        half = c >> (lvl + 1)
        mid = (t // (2 * half)) * (2 * half) + half
        right = t >= mid
        e_rows.append(np.where(right, (i >= mid) & (i <= t), (i > t) & (i < mid)))
        same = (t // (2 * half)) == (i // (2 * half))
        masks.append(same & right & (i < mid))
    masks.append(t == i)
    e_rows.append(i <= t)
    e_rows.append(i > t)
    e = np.concatenate(e_rows, 0).astype(np.float32)
    m = np.stack(masks, 0).astype(np.float32)
    if stacked_heads > 1:
        m = np.tile(m, (1, stacked_heads, 1))
    return jnp.asarray(np.concatenate([e, e], 1), dtype=BF16), jnp.asarray(m)


def _rec_core(q, k, lf, v, e_ref, m_ref, st_ref, o_ref, *, c, kind, single):
    nl = 0 if single else int(math.log2(c))
    n_win = q.shape[1] // LANES
    win = lambda x, w: x[:, w * LANES:(w + 1) * LANES]
    if single:
        e_cum = jnp.exp(lf)
        ke_f = k
    else:
        l1 = lf.astype(BF16)
        l2 = (lf - l1.astype(F32)).astype(BF16)
        decay = jnp.exp(_dot(e_ref[...], jnp.concatenate([l1, l2], axis=0)))
        seg = lambda n: decay[n * c:(n + 1) * c]
        e_cum = seg(nl)
        ke_f = k * seg(nl + 1)
    yield
    row = lax.broadcasted_iota(jnp.int32, (c, LANES), 0)

    def level_operand(n, w):
        half = c >> (n + 1)
        qw, kw = win(q, w), win(k, w)
        if half >= 8:
            pieces = [(qw if j % 2 else kw)[j * half:(j + 1) * half] for j in range(c // half)]
            side = jnp.concatenate(pieces, axis=0)
        else:
            side = jnp.where((row & half) != 0, qw, kw)
        return (side * win(seg(n), w)).astype(BF16)

    u = [[level_operand(n, w) for w in range(n_win)] for n in range(nl)]
    uq = [win(q, w).astype(BF16) for w in range(n_win)]
    uk = [win(k, w).astype(BF16) for w in range(n_win)]
    qg = (q * e_cum).astype(BF16)
    ke = ke_f.astype(BF16)
    e_tot = e_cum[0:1, :] if single else e_cum[c - 1:c, :]
    vb = v.astype(BF16)
    yield
    lane = lax.broadcasted_iota(jnp.int32, (c, LANES), 1)
    low = lane < B_VAL_DIM
    if kind == "c":
        head = lax.broadcasted_iota(jnp.int32, (1, LANES), 1) // C_KEY_DIM
        hm = [(head == h).astype(F32).astype(BF16) for h in range(C_HEADS)]
        stack = lambda x: jnp.concatenate([x * hm[h] for h in range(C_HEADS)], axis=0)
        att_all = _dot_nt(stack(uq[0]), uk[0]) * m_ref[nl]
        for n in range(nl):
            att_all = att_all + _dot_nt(stack(u[n][0]), u[n][0]) * m_ref[n]
        att_all = att_all.astype(BF16)
        atts = [att_all[h * c:(h + 1) * c] for h in range(C_HEADS)]
    else:
        atts = []
        for h in range(B_HEADS):
            att = _dot_nt(uq[h], uk[h]) * m_ref[nl]
            for n in range(nl):
                att = att + _dot_nt(u[n][h], u[n][h]) * m_ref[n]
            atts.append(att.astype(BF16))
    intra = [jnp.where(low, _dot(atts[2 * pair], win(vb, pair)), _dot(atts[2 * pair + 1], win(vb, pair)))
             for pair in range(2)]
    half_rows = [slice(hh * B_VAL_DIM, (hh + 1) * B_VAL_DIM) for hh in range(2)]
    if kind == "c":
        srow = lax.broadcasted_iota(jnp.int32, (LANES, LANES), 0) // B_VAL_DIM
        slane = lax.broadcasted_iota(jnp.int32, (LANES, LANES), 1) // C_KEY_DIM
        added = [_dot_tn(win(vb, pair), ke) * (slane == srow + 2 * pair).astype(F32) for pair in range(2)]
    else:
        added = [jnp.concatenate([_dot_tn(win(vb, pair), win(ke, 2 * pair + hh))[half_rows[hh], :]
                                  for hh in range(2)], axis=0) for pair in range(2)]
    yield
    for pair in range(2):
        st = st_ref[pair]
        stb = st.astype(BF16)
        if kind == "c":
            o_ref[:, pair * LANES:(pair + 1) * LANES] = intra[pair] + _dot_nt(qg, stb)
            st_ref[pair] = st * e_tot + added[pair]
            continue
        carried = [_dot_nt(win(qg, 2 * pair + hh), stb) for hh in range(2)]
        o_ref[:, pair * LANES:(pair + 1) * LANES] = intra[pair] + jnp.where(low, carried[0], carried[1])
        keep = jnp.concatenate([st[half_rows[hh], :] * win(e_tot, 2 * pair + hh) for hh in range(2)], axis=0)
        st_ref[pair] = keep + added[pair]


_REC_STAGES = 4
_REC_KINDS = ("b", "c")
_REC_N_PARAMS = {"b": 1, "c": 2}


def _rec_body(*refs, c, n_valid, n_b, n_sub):
    ins, pos = {}, 0
    for kind in _REC_KINDS:
        n = 4 + _REC_N_PARAMS[kind]
        ins[kind] = refs[pos:pos + n]
        pos += n
    outs = {kind: refs[pos + 2 * i:pos + 2 * i + 2] for i, kind in enumerate(_REC_KINDS)}
    pos += 2 * len(_REC_KINDS)
    states = {kind: refs[pos + i] for i, kind in enumerate(_REC_KINDS)}

    @pl.when(pl.program_id(1) == 0)
    def _():
        for kind in _REC_KINDS:
            states[kind][...] = ins[kind][-1][...]

    chunks = []
    for bb in range(n_b):
        for j in range(n_sub):
            for kind in _REC_KINDS:
                z_ref, *params, e_ref, m_ref, _ = ins[kind]
                chunks.append(_rec_chunk(z_ref.at[bb, pl.ds(j * c, c)], params, e_ref, m_ref, states[kind].at[bb],
                                         outs[kind][0].at[bb, pl.ds(j * c, c)], kind=kind, c=c, n_valid=n_valid))
    for _ in range(_REC_STAGES - 1):
        for chunk in chunks:
            next(chunk)
    for chunk in chunks:
        for _ in chunk:
            raise AssertionError("more stages than _REC_STAGES")

    @pl.when(pl.program_id(1) == pl.num_programs(1) - 1)
    def _():
        for kind in _REC_KINDS:
            outs[kind][1][...] = states[kind][...]


def _rec_chunk(z_ref, params, e_ref, m_ref, st_ref, o_ref, *, kind, c, n_valid):
    z = z_ref[...]
    if kind == "b":
        lb = params[0][...]
        f = lb + (1.0 - lb) * _sigmoid(z[:, B_KW:2 * B_KW])
        lf = jnp.log(jnp.maximum(f, MIN_F))
        k = 1.0 - f
        zq = z[:, 0:B_KW]
        q = zq * _sigmoid(zq)
        v = z[:, 2 * B_KW:2 * B_KW + B_WIDTH]
    else:
        cwg_ref, cb_ref = params
        q = z[:, 0:C_KW]
        k = z[:, C_KW:2 * C_KW]
        v = z[:, 2 * C_KW:2 * C_KW + C_WIDTH]
        x = _dot(z[:, ZC_WIDTH - LANES:].astype(BF16), cwg_ref[...]) + cb_ref[...]
        lf = (jnp.minimum(x, 0.0) - jnp.log1p(jnp.exp(-jnp.abs(x)))) / C_GATE_TEMP
    if n_valid < c:
        live = lax.broadcasted_iota(jnp.int32, (c, 1), 0) < n_valid
        q = jnp.where(live, q, 0.0)
        k = jnp.where(live, k, 0.0)
        lf = jnp.where(live, lf, 0.0)
    yield from _rec_core(q, k, lf, v, e_ref, m_ref, st_ref, o_ref, c=c, kind=kind, single=n_valid == 1)


def _recurrences(zs, layer, params, s0s, s0_layer, n_valid=None):
    nb, t, _ = zs["b"].shape
    c = min(REC_CHUNK, t)
    assert t % c == 0
    n_valid = c if n_valid is None else n_valid
    assert n_valid in (1, c)
    n_sub = math.gcd(t // c, REC_CHUNKS_PER_STEP)
    n_b = math.gcd(nb, REC_SEQS_PER_STEP)
    st_spec = pl.BlockSpec((n_b, 2, LANES, LANES), lambda b, j: (b, 0, 0, 0))
    s0_spec = pl.BlockSpec((None, n_b, 2, LANES, LANES), lambda b, j: (s0_layer, b, 0, 0, 0))
    args, in_specs = [], []
    for kind in _REC_KINDS:
        assert len(params[kind]) == _REC_N_PARAMS[kind] and zs[kind].shape[:2] == (nb, t)
        e, m = _rec_tables(c, n_valid == 1, C_HEADS if kind == "c" else 1)
        args += [zs[kind], *params[kind], e, m, s0s[kind]]
        in_specs += [pl.BlockSpec((n_b, n_sub * c, zs[kind].shape[2]), lambda b, j: (b, j, 0))]
        in_specs += [_layer_spec(x, layer) for x in params[kind]]
        in_specs += [_const_spec(e.shape), _const_spec(m.shape), s0_spec]
    n_kinds = len(_REC_KINDS)
    res = pl.pallas_call(
        functools.partial(_rec_body, c=c, n_valid=n_valid, n_b=n_b, n_sub=n_sub),
        grid=(nb // n_b, t // (c * n_sub)),
        in_specs=in_specs,
        out_specs=[pl.BlockSpec((n_b, n_sub * c, 2 * LANES), lambda b, j: (b, j, 0)), st_spec] * n_kinds,
        out_shape=[jax.ShapeDtypeStruct((nb, t, 2 * LANES), F32),
                   jax.ShapeDtypeStruct((nb, 2, LANES, LANES), F32)] * n_kinds,
        scratch_shapes=[pltpu.VMEM((n_b, 2, LANES, LANES), F32)] * n_kinds,
        compiler_params=_params(2),
        name="recurrences",
    )(*args)
    return {kind: (res[2 * i], res[2 * i + 1]) for i, kind in enumerate(_REC_KINDS)}


def _mix_body(*refs, tm, dils):
    h_ref = refs[0]
    n_att = 2 * len(dils) if dils else 1
    att_refs = refs[1:1 + n_att]
    ob_ref, bg_ref, oc_ref, cg_ref, bn_ref, cn_ref, ex_ref, bd_ref, w_ref, out_ref, mixed = refs[1 + n_att:12 + n_att]
    if dils:
        o_refs, l_refs = att_refs[:len(dils)], att_refs[len(dils):]
        o_nat, l_nat = refs[12 + n_att], refs[13 + n_att]
        n_cb = A_WIDTH // LANES
        lses = []
        for c, d in enumerate(dils):
            if d == 1:
                lses.append(l_refs[c][0])
                continue
            for r in range(d):
                l_nat[c, pl.ds(r, tm // d, stride=d), :] = l_refs[c][r]
                for cb in range(n_cb):
                    o_nat[c, cb, pl.ds(r, tm // d, stride=d), :] = (
                        o_refs[c][r, :, cb * LANES:(cb + 1) * LANES].astype(F32))
            lses.append(l_nat[c])
        mx = functools.reduce(jnp.maximum, lses)
        es = [jnp.exp(l - mx) for l in lses]
        den = functools.reduce(lambda a, b: a + b, es)
        ex = ex_ref[...]
        wts = [_dot_split2(e / den, ex) for e in es]
        for cb in range(n_cb):
            cs = slice(cb * LANES, (cb + 1) * LANES)
            terms = [wts[c][:, cs] * (o_refs[c][0, :, cs] if d == 1 else o_nat[c, cb]) for c, d in enumerate(dils)]
            mixed[:, cs] = functools.reduce(lambda a, b: a + b, terms).astype(BF16)
    else:
        mixed[:, 0:A_WIDTH] = att_refs[0][...].astype(BF16)
    bd = bd_ref[...]
    ob = ob_ref[...]
    obn = ob * lax.rsqrt(_dot_split2(ob * ob, bd) + RMS_EPS) * bn_ref[...]
    mixed[:, A_WIDTH:A_WIDTH + B_WIDTH] = (obn * _sigmoid(bg_ref[...])).astype(BF16)
    oc = oc_ref[...]
    ocn = oc * lax.rsqrt(_dot_split2(oc * oc, bd) + RMS_EPS) * cn_ref[...]
    cg = cg_ref[...]
    mixed[:, A_WIDTH + B_WIDTH:] = (ocn * (cg * _sigmoid(cg))).astype(BF16)
    out_ref[...] = h_ref[...] + _dot(mixed[...], w_ref[...])


def _mix(h, layer, att, dils, ob, zb, oc, zc, b_norm, c_norm, w_out):
    m, d = h.shape
    tm = min(MIX_ROW_TILE, m)
    assert m % tm == 0
    row = lambda w, blk=0: pl.BlockSpec((tm, w), lambda i: (i, blk))
    if dils:
        att_specs = [pl.BlockSpec((dd, tm // dd, wd), lambda i: (0, i, 0))
                     for wd in (A_WIDTH, LANES) for dd in dils]
        att_scratch = [pltpu.VMEM((len(dils), A_WIDTH // LANES, tm, LANES), F32),
                       pltpu.VMEM((len(dils), tm, LANES), F32)]
    else:
        att_specs, att_scratch = [row(A_WIDTH)], []
    expand = np.zeros((LANES, A_WIDTH), np.float32)
    for hd in range(A_HEADS):
        expand[hd, hd * A_HEAD_DIM:(hd + 1) * A_HEAD_DIM] = 1.0
    expand = np.concatenate([expand, expand], axis=0)
    grp = np.arange(B_WIDTH) // B_VAL_DIM
    blockmean = (grp[:, None] == grp[None, :]).astype(np.float32) / B_VAL_DIM
    blockmean = np.concatenate([blockmean, blockmean], axis=0)
    d_mix = A_WIDTH + B_WIDTH + C_WIDTH
    return pl.pallas_call(
        functools.partial(_mix_body, tm=tm, dils=tuple(dils)),
        grid=(m // tm,),
        in_specs=[row(d)] + att_specs
                 + [row(B_WIDTH), row(B_WIDTH, (2 * B_KW + B_WIDTH) // B_WIDTH),
                    row(C_WIDTH), row(C_WIDTH, (2 * C_KW + C_WIDTH) // C_WIDTH)]
                 + [_layer_spec(b_norm, layer), _layer_spec(c_norm, layer), _const_spec(expand.shape),
                    _const_spec(blockmean.shape), _layer_spec(w_out, layer)],
        out_specs=row(d),
        out_shape=jax.ShapeDtypeStruct((m, d), F32),
        scratch_shapes=[pltpu.VMEM((tm, d_mix), BF16)] + att_scratch,
        compiler_params=_params(1),
        name="mix_out",
    )(h, *att, ob, zb, oc, zc, b_norm, c_norm,
      jnp.asarray(expand, dtype=BF16), jnp.asarray(blockmean, dtype=BF16), w_out)


def _state_to_pairs(s):
    nl, nb, nh, dk, dv = s.shape
    st = jnp.swapaxes(s, 3, 4)
    if dk < LANES:
        st = jnp.stack([jnp.pad(st[:, :, h], ((0, 0), (0, 0), (0, 0), (h * dk, LANES - (h + 1) * dk)))
                        for h in range(nh)], 2)
    return st.reshape(nl, nb, 2, 2 * dv, LANES)


def _pairs_to_state(st, dk):
    nb = st.shape[0]
    st = st.reshape(nb, 4, B_VAL_DIM, LANES)
    if dk < LANES:
        st = jnp.stack([st[:, h, :, h * dk:(h + 1) * dk] for h in range(4)], 1)
    return jnp.swapaxes(st, 2, 3)


def _layer_tail(h, i, w, lb_all, p, final_norm, zb, zc, att, dils, states, s0_layer, decode):
    depth = w["w_in"].shape[0]
    if decode:
        nb = h.shape[0]
        pad = lambda a: jnp.pad(a.reshape(nb, 1, -1), ((0, 0), (0, DEC_PAD - 1), (0, 0)))
        zb3, zc3, n_valid = pad(zb), pad(zc), 1
    else:
        zb3, zc3, n_valid = zb[None], zc[None], None
    rec = _recurrences(dict(b=zb3, c=zc3), i, dict(b=[lb_all], c=[w["c_w_gate"], w["c_gate_bias"]]),
                       states, s0_layer, n_valid)
    (ob, sb), (oc, sc) = rec["b"], rec["c"]
    ob, oc = (ob[:, 0], oc[:, 0]) if decode else (ob[0], oc[0])
    h = _mix(h, i, att, dils, ob, zb, oc, zc, w["b_out_norm"], w["c_out_norm"], w["w_out"])
    ple = (p, w["ple_norm"], w["ple_w_gate"], w["ple_w_proj"])
    last = final_norm if i == depth - 1 else None
    h, *y = _ffn(h, i, w["ffn2_norm"], w["ffn2_w_gate"], w["ffn2_w_up"], w["ffn2_w_down"], ple, last)
    return h, (y[0] if y else None), _pairs_to_state(sb, B_KEY_DIM), _pairs_to_state(sc, C_KEY_DIM)


def kernel(x_prompt, x_sample, cache_k_a, cache_v_a, state_b, state_c, p_prompt, p_sample, ffn1_norm, ffn1_w_gate, ffn1_w_up, ffn1_w_down, mix_norm, w_in, lb_logits, b_out_norm, c_w_gate, c_gate_bias, c_out_norm, w_out, ffn2_norm, ffn2_w_gate, ffn2_w_up, ffn2_w_down, ple_norm, ple_w_gate, ple_w_proj, final_norm):
    depth = w_in.shape[0]
    rows = lambda a: a.reshape(depth, 1, -1)
    w = {name: a.astype(BF16) for name, a in dict(
        ffn1_w_gate=ffn1_w_gate, ffn1_w_up=ffn1_w_up, ffn1_w_down=ffn1_w_down, w_out=w_out, ffn2_w_gate=ffn2_w_gate,
        ffn2_w_up=ffn2_w_up, ffn2_w_down=ffn2_w_down, ple_w_gate=ple_w_gate, ple_w_proj=ple_w_proj).items()}
    w.update(
        ffn1_norm=rows(ffn1_norm), mix_norm=rows(mix_norm), w_in=jnp.swapaxes(w_in, 1, 2).astype(BF16),
        b_out_norm=rows(b_out_norm),
        c_w_gate=jnp.pad(c_w_gate, ((0, 0), (0, LANES - C_GATE_RANK), (0, 0))).astype(BF16),
        c_gate_bias=rows(c_gate_bias), c_out_norm=rows(c_out_norm), ffn2_norm=rows(ffn2_norm), ple_norm=rows(ple_norm))
    sm = jax.nn.softmax(lb_logits.astype(F32), axis=0)
    lb_all = jnp.maximum(jnp.cumsum(sm, axis=0) - sm[0], 0.0).reshape(depth, 1, B_KW)
    fn = final_norm.reshape(1, -1)

    (one, t, d), (nb, one_tok, _) = x_prompt.shape, x_sample.shape
    assert one == 1 and one_tok == 1
    hp, hs = x_prompt.reshape(t, d), x_sample.reshape(nb, d)
    pp, ps = p_prompt.reshape(depth, t, -1), p_sample.reshape(depth, nb, -1)
    keep = min(A_MAX_WINDOW, t)
    k_cache_t = jnp.transpose(cache_k_a, (0, 1, 3, 4, 2))
    v_cache_t = jnp.transpose(cache_v_a, (0, 1, 3, 4, 2))
    states_s = dict(b=_state_to_pairs(state_b), c=_state_to_pairs(state_c))
    states_p = dict(b=jnp.zeros((1, 1, 2, LANES, LANES), F32), c=jnp.zeros((1, 1, 2, LANES, LANES), F32))
    dils = tuple(dil for _, dil in A_CONFIGS)
    assert dils[0] == 1
    cols = lambda a: jnp.swapaxes(a.reshape(nb, A_HEADS, A_HEAD_DIM), 1, 2)
    ffn1 = lambda h, i, side=None: _ffn(h, i, w["ffn1_norm"], w["ffn1_w_gate"], w["ffn1_w_up"], w["ffn1_w_down"],
                                        side=side)
    outs = {key: [] for key in ("k_p", "v_p", "sb_p", "sc_p", "k_s", "v_s", "sb_s", "sc_s")}
    for i in range(depth):
        hs, = ffn1(hs, i)
        qkv_s, kf_s, vf_s, zb_s, zc_s = _proj(hs, i, w["mix_norm"], w["w_in"], nb, (), F32)
        side = (cols(qkv_s[0, :, 0:A_WIDTH]), cols(kf_s), cols(vf_s), k_cache_t, v_cache_t, i)
        if t // min(ROW_TILE, t) == nb:
            hp, o_t = ffn1(hp, i, side)
        else:
            hp, = ffn1(hp, i)
            o_t = _attn_decode(*side)
        *qkvs, kf_p, vf_p, zb_p, zc_p = _proj(hp, i, w["mix_norm"], w["w_in"], keep, dils[1:], BF16)
        res = [_attn_prompt(a, dil) for a, dil in zip(qkvs, dils)]
        att_p = [r[0] for r in res] + [r[1] for r in res]
        hp, y_p, sb, sc = _layer_tail(hp, i, w, lb_all, pp, fn, zb_p, zc_p, att_p, dils, states_p, 0, False)
        outs["k_p"].append(kf_p.reshape(1, keep, A_HEADS, A_HEAD_DIM))
        outs["v_p"].append(vf_p.reshape(1, keep, A_HEADS, A_HEAD_DIM))
        outs["sb_p"].append(sb)
        outs["sc_p"].append(sc)
        att_s = [jnp.swapaxes(o_t, 1, 2).reshape(nb, A_WIDTH)]
        hs, y_s, sb, sc = _layer_tail(hs, i, w, lb_all, ps, fn, zb_s, zc_s, att_s, (), states_s, i, True)
        outs["k_s"].append(kf_s.reshape(nb, 1, A_HEADS, A_HEAD_DIM))
        outs["v_s"].append(vf_s.reshape(nb, 1, A_HEADS, A_HEAD_DIM))
        outs["sb_s"].append(sb)
        outs["sc_s"].append(sc)
    st = {key: jnp.stack(val) for key, val in outs.items()}
    return (y_p.reshape(1, t, d), y_s.reshape(nb, 1, d), st["k_p"], st["v_p"], st["sb_p"], st["sc_p"],
            st["k_s"], st["v_s"], st["sb_s"], st["sc_s"])
```

```python
import functools
import math

import numpy as np
import jax
import jax.numpy as jnp
from jax import lax
from jax.experimental import pallas as pl
from jax.experimental.pallas import tpu as pltpu

F32 = jnp.float32
BF16 = jnp.bfloat16

RMS_EPS = 1e-6
NEG_BIG = -1e30
MIN_F = 1e-30
A_HEADS = 8
A_HEAD_DIM = 64
A_CONFIGS = ((128, 1), (512, 4), (2048, 16))
A_MAX_WINDOW = 2048
A_SUB = 128
B_HEADS = 4
B_KEY_DIM = 128
B_VAL_DIM = 64
C_HEADS = 4
C_KEY_DIM = 32
C_VAL_DIM = 64
C_GATE_RANK = 16
C_GATE_TEMP = 16.0
A_WIDTH = A_HEADS * A_HEAD_DIM
B_KW = B_HEADS * B_KEY_DIM
B_WIDTH = B_HEADS * B_VAL_DIM
C_KW = C_HEADS * C_KEY_DIM
C_WIDTH = C_HEADS * C_VAL_DIM

LANES = 128
MXU_WIDTH = 256

ZB_WIDTH = 2 * B_KW + 2 * B_WIDTH
ZC_WIDTH = 2 * C_KW + 2 * C_WIDTH + LANES
VMEM_LIMIT_BYTES = 56 * 1024 * 1024

ROW_TILE = 512
MIX_ROW_TILE = 1024
FF_CHUNK = 1536
REC_CHUNK = 128
REC_CHUNKS_PER_STEP = 4
REC_SEQS_PER_STEP = 8
ATT_ROWS = 2048
DEC_PAD = 16


def _params(n_axes):
    return pltpu.CompilerParams(dimension_semantics=("arbitrary",) * n_axes,
                                vmem_limit_bytes=VMEM_LIMIT_BYTES)


def _const_spec(shape):
    zeros = (0,) * len(shape)
    return pl.BlockSpec(shape, lambda *_: zeros, pipeline_mode=pl.Buffered(1))


def _layer_spec(stacked, layer):
    shape = tuple(stacked.shape[1:])
    zeros = (0,) * len(shape)
    return pl.BlockSpec((None,) + shape, lambda *_: (layer,) + zeros, pipeline_mode=pl.Buffered(1))


def _dot(a, b):
    return jnp.dot(a, b, preferred_element_type=F32)


def _dot_nt(a, b):
    return lax.dot_general(a, b, (((1,), (1,)), ((), ())), preferred_element_type=F32)


def _dot_tn(a, b):
    return lax.dot_general(a, b, (((0,), (0,)), ((), ())), preferred_element_type=F32)


def _dot_split2(x, w):
    hi = x.astype(BF16)
    lo = (x - hi.astype(F32)).astype(BF16)
    if w.shape[0] == 2 * x.shape[1]:
        return _dot(jnp.concatenate([hi, lo], axis=1), w)
    return _dot(hi, w) + _dot(lo, w)


def _rms(x, g):
    ms = jnp.mean(x * x, axis=-1, keepdims=True)
    return x * lax.rsqrt(ms + RMS_EPS) * g


def _sigmoid(x):
    return 1.0 / (1.0 + jnp.exp(-x))


def _ff_chunks(dff):
    unit = MXU_WIDTH if dff % MXU_WIDTH == 0 else LANES
    assert dff % unit == 0
    tiles = dff // unit
    n = -(-dff // FF_CHUNK)
    sizes = [(tiles // n + (1 if i < tiles % n else 0)) * unit for i in range(n)]
    starts = np.cumsum([0] + sizes[:-1]).tolist()
    return tuple(zip(starts, sizes))


def _ffn_body(*refs, chunks, with_ple, with_final, n_side_in):
    refs = list(refs)
    if n_side_in:
        _attn_decode_body(*refs[:n_side_in], refs[-1])
        refs = refs[n_side_in:-1]
    h_ref, nrm_ref, wg_ref, wu_ref, wd_ref = refs[:5]
    pos = 5
    if with_ple:
        p_ref, pn_ref, pwg_ref, pwp_ref = refs[pos:pos + 4]
        pos += 4
    if with_final:
        fn_ref = refs[pos]
        pos += 1
    o_ref = refs[pos]
    h = h_ref[...]
    xn = _rms(h, nrm_ref[...]).astype(BF16)
    acc = None
    for lo, chunk in chunks:
        g = _dot(xn, wg_ref[:, lo:lo + chunk])
        u = _dot(xn, wu_ref[:, lo:lo + chunk])
        a = (g * _sigmoid(g) * u).astype(BF16)
        d = _dot(a, wd_ref[lo:lo + chunk, :])
        acc = d if acc is None else acc + d
    h = h + 0.5 * acc
    if with_ple:
        hn = _rms(h, pn_ref[...]).astype(BF16)
        gate = _sigmoid(_dot(hn, pwg_ref[...]))
        proj = _dot(p_ref[...].astype(BF16), pwp_ref[...])
        h = h + gate * proj
    o_ref[...] = h
    if with_final:
        refs[pos + 1][...] = _rms(h, fn_ref[...])


def _ffn(h, layer, nrm, wg, wu, wd, ple=None, final_norm=None, side=None):
    m, d = h.shape
    dff = wg.shape[-1]
    tm = min(ROW_TILE, m)
    assert m % tm == 0
    row = lambda w: pl.BlockSpec((tm, w), lambda i: (i, 0))
    ins = [h, nrm, wg, wu, wd]
    specs = [row(d)] + [_layer_spec(a, layer) for a in ins[1:]]
    if ple is not None:
        ins += list(ple)
        specs += [pl.BlockSpec((None, tm, ple[0].shape[-1]), lambda i: (layer, i, 0))]
        specs += [_layer_spec(a, layer) for a in ple[1:]]
    if final_norm is not None:
        ins.append(final_norm)
        specs.append(_const_spec((1, d)))
    n_out = 2 if final_norm is not None else 1
    out_specs = [row(d)] * n_out
    out_shape = [jax.ShapeDtypeStruct((m, d), F32)] * n_out
    n_side_in = 0
    if side is not None:
        side_args, side_specs, side_out_spec, side_out_shape = _attn_decode_operands(*side)
        assert side_out_shape.shape[0] == m // tm
        ins, specs, n_side_in = side_args + ins, side_specs + specs, len(side_args)
        out_specs, out_shape = out_specs + [side_out_spec], out_shape + [side_out_shape]
    return pl.pallas_call(
        functools.partial(_ffn_body, chunks=_ff_chunks(dff), with_ple=ple is not None,
                          with_final=final_norm is not None, n_side_in=n_side_in),
        grid=(m // tm,),
        in_specs=specs,
        out_specs=out_specs,
        out_shape=out_shape,
        compiler_params=_params(1),
        name="ffn",
    )(*ins)


def _proj_body(h_ref, nrm_ref, w_ref, *refs, tm, dils):
    n_a = 1 + len(dils)
    a_refs = refs[:n_a]
    kf_ref, vf_ref, zb_ref, zc_ref = refs[n_a:n_a + 4]
    xn = _rms(h_ref[...], nrm_ref[...]).astype(BF16)
    a = A_WIDTH
    b0 = 3 * a
    for part in range(3):
        cols = slice(part * a, (part + 1) * a)
        z = _dot_nt(xn, w_ref[cols, :])
        if part == 0:
            z = z * (A_HEAD_DIM ** -0.5)
        elif part == 1:
            kf_ref[...] = z
        else:
            vf_ref[...] = z
        a_refs[0][0, :, cols] = z.astype(a_refs[0].dtype)
        if dils:
            stage = refs[-1]
            for cb in range(a // LANES):
                stage[cb] = z[:, cb * LANES:(cb + 1) * LANES]
        for ref, d in zip(a_refs[1:], dils):
            for r in range(d):
                for cb in range(a // LANES):
                    lo = part * a + cb * LANES
                    ref[r, :, lo:lo + LANES] = stage[cb, pl.ds(r, tm // d, stride=d), :].astype(ref.dtype)
    zb_ref[...] = _dot_nt(xn, w_ref[b0:b0 + ZB_WIDTH, :])
    c0 = b0 + ZB_WIDTH
    qkv_c = 2 * C_KW + C_WIDTH
    zc = _dot_nt(xn, w_ref[c0:c0 + qkv_c, :])
    zc_ref[:, 0:C_KW] = zc[:, 0:C_KW] * (C_KEY_DIM ** -0.5)
    zc_ref[:, C_KW:qkv_c] = zc[:, C_KW:]
    r0 = c0 + qkv_c
    zc_ref[:, qkv_c:qkv_c + C_WIDTH] = _dot_nt(xn, w_ref[r0 + C_GATE_RANK:r0 + C_GATE_RANK + C_WIDTH, :])
    lane = lax.broadcasted_iota(jnp.int32, (tm, LANES), 1)
    rank = _dot_nt(xn, w_ref[r0:r0 + LANES, :])
    zc_ref[:, qkv_c + C_WIDTH:] = jnp.where(lane < C_GATE_RANK, rank, 0.0)


def _proj(h, layer, nrm, w, keep, dils, a_dtype):
    m, d = h.shape
    tm = min(ROW_TILE, m)
    assert m % tm == 0 and keep % tm == 0
    nt, nk = m // tm, keep // tm
    row = lambda wd: pl.BlockSpec((tm, wd), lambda i: (i, 0))
    tail = pl.BlockSpec((tm, A_WIDTH), lambda i: (jnp.maximum(i - (nt - nk), 0), 0))
    sds = jax.ShapeDtypeStruct
    all_d = (1,) + tuple(dils)
    return pl.pallas_call(
        functools.partial(_proj_body, tm=tm, dils=tuple(dils)),
        grid=(nt,),
        in_specs=[row(d), _layer_spec(nrm, layer), _layer_spec(w, layer)],
        out_specs=[pl.BlockSpec((dd, tm // dd, 3 * A_WIDTH), lambda i: (0, i, 0)) for dd in all_d]
                  + [tail, tail, row(ZB_WIDTH), row(ZC_WIDTH)],
        out_shape=[sds((dd, m // dd, 3 * A_WIDTH), a_dtype) for dd in all_d]
                  + [sds((keep, A_WIDTH), F32), sds((keep, A_WIDTH), F32),
                     sds((m, ZB_WIDTH), F32), sds((m, ZC_WIDTH), F32)],
        scratch_shapes=[pltpu.VMEM((A_WIDTH // LANES, tm, LANES), F32)] if dils else [],
        compiler_params=_params(1),
        name="proj",
    )(h, nrm, w)


def _alibi_slopes():
    return (2.0 ** (-8.0 * np.arange(1, A_HEADS + 1, dtype=np.float32) / A_HEADS)).astype(np.float32)


def _attn_bias(dil):
    qi = np.arange(A_SUB)[:, None]
    ki = np.arange(2 * A_SUB)[None, :]
    j = qi + A_SUB - ki
    band = (j >= 0) & (j <= A_SUB)
    bias = -_alibi_slopes()[:, None, None] * (j * dil).astype(np.float32)[None]
    return jnp.asarray(np.where(band[None], bias, np.float32(NEG_BIG)).astype(np.float32))


def _attn_body(q_ref, kp_ref, kc_ref, vp_ref, vc_ref, bias_ref, o_ref, lse_ref, kk, vv, *, n_sub):
    qb = A_SUB
    kk[0:A_SUB, :] = kp_ref[...]
    vv[0:A_SUB, :] = vp_ref[...]
    kk[A_SUB:, :] = kc_ref[...]
    vv[A_SUB:, :] = vc_ref[...]
    lane = lax.broadcasted_iota(jnp.int32, (qb, LANES), 1)
    low = lane < A_HEAD_DIM
    low_f = low.astype(F32)
    head_lanes = (low_f.astype(BF16), (1.0 - low_f).astype(BF16))
    first = jnp.where(pl.program_id(1) > 0, 0, A_SUB)
    key_ok = lax.broadcasted_iota(jnp.int32, (qb, 2 * A_SUB), 1) >= first
    for i in range(n_sub):
        rows = slice(i * qb, (i + 1) * qb)
        krows = slice(i * qb, i * qb + 2 * A_SUB)
        lse_tile = jnp.zeros((qb, LANES), F32)
        for hp in range(A_HEADS // 2):
            cols = slice(hp * LANES, (hp + 1) * LANES)
            q2 = q_ref[rows, cols]
            k2 = kk[krows, cols]
            v2 = vv[krows, cols]
            outs = []
            for hh in range(2):
                h = 2 * hp + hh
                s = _dot_nt(q2 * head_lanes[hh], k2) + bias_ref[h]
                if i == 0:
                    s = jnp.where(key_ok, s, NEG_BIG)
                mx = jnp.max(s, axis=-1, keepdims=True)
                p = jnp.exp(s - mx)
                den = jnp.sum(p, axis=-1, keepdims=True)
                outs.append(_dot(p.astype(BF16), v2) / den)
                lse_tile = jnp.where(lane == h, mx + jnp.log(den), lse_tile)
            o_ref[rows, cols] = jnp.where(low, outs[0], outs[1]).astype(o_ref.dtype)
        lse_ref[rows, :] = lse_tile


def _attn_prompt(qkv, dil):
    _, tv, _ = qkv.shape
    assert qkv.shape[0] == dil and tv % A_SUB == 0
    bq = min(ATT_ROWS, tv)
    assert tv % bq == 0 and bq % A_SUB == 0
    cur = lambda part: pl.BlockSpec((None, bq, A_WIDTH), lambda r, n: (r, n, part))
    prev = lambda part: pl.BlockSpec((None, A_SUB, A_WIDTH),
                                     lambda r, n: (r, jnp.maximum(n * (bq // A_SUB) - 1, 0), part))
    return pl.pallas_call(
        functools.partial(_attn_body, n_sub=bq // A_SUB),
        grid=(dil, tv // bq),
        in_specs=[cur(0), prev(1), cur(1), prev(2), cur(2), _const_spec((A_HEADS, A_SUB, 2 * A_SUB))],
        out_specs=[cur(0), pl.BlockSpec((None, bq, LANES), lambda r, n: (r, n, 0))],
        out_shape=[jax.ShapeDtypeStruct((dil, tv, A_WIDTH), BF16), jax.ShapeDtypeStruct((dil, tv, LANES), F32)],
        scratch_shapes=[pltpu.VMEM((A_SUB + bq, A_WIDTH), BF16), pltpu.VMEM((A_SUB + bq, A_WIDTH), BF16)],
        compiler_params=_params(2),
        name=f"attn_prompt_d{dil}",
    )(qkv, qkv, qkv, qkv, qkv, _attn_bias(dil))


def _decode_tables(n_past):
    dist = n_past - np.arange(n_past)
    mult = np.zeros(n_past, np.float32)
    for window, dil in A_CONFIGS:
        mult += ((dist % dil == 0) & (dist // dil <= window // dil)).astype(np.float32)
    bias = -_alibi_slopes()[:, None] * dist.astype(np.float32)[None, :]
    return jnp.asarray(bias), jnp.asarray(mult[None, :])


def _attn_decode_body(q_ref, kn_ref, vn_ref, kc_ref, vc_ref, bias_ref, mult_ref, o_ref):
    q_t = q_ref[...]
    s_new = jnp.sum(q_t * kn_ref[...], axis=0, keepdims=True)
    mult = mult_ref[...]
    live = mult > 0.0
    n_cfg = float(len(A_CONFIGS))
    lane = lax.broadcasted_iota(jnp.int32, q_t.shape, 1)
    out = jnp.zeros(q_t.shape, F32)
    for h in range(A_HEADS):
        s = jnp.sum(kc_ref[h] * q_t[:, h:h + 1], axis=0, keepdims=True) + bias_ref[h:h + 1, :]
        s = jnp.where(live, s, NEG_BIG)
        s0 = s_new[:, h:h + 1]
        mx = jnp.maximum(jnp.max(s, axis=-1, keepdims=True), s0)
        p = jnp.exp(s - mx) * mult
        p0 = jnp.exp(s0 - mx) * n_cfg
        den = jnp.sum(p, axis=-1, keepdims=True) + p0
        acc = jnp.sum(vc_ref[h] * p, axis=-1, keepdims=True) + p0 * vn_ref[:, h:h + 1]
        out = jnp.where(lane == h, acc / den, out)
    o_ref[...] = out


def _attn_decode_operands(q_t, kn_t, vn_t, k_cache_t, v_cache_t, layer):
    nb = q_t.shape[0]
    n_past = k_cache_t.shape[-1]
    assert n_past == A_MAX_WINDOW
    bias, mult = _decode_tables(n_past)
    new = pl.BlockSpec((None, A_HEAD_DIM, A_HEADS), lambda b: (b, 0, 0))
    past = pl.BlockSpec((None, None, A_HEADS, A_HEAD_DIM, n_past), lambda b: (layer, b, 0, 0, 0))
    return ([q_t, kn_t, vn_t, k_cache_t, v_cache_t, bias, mult],
            [new, new, new, past, past, _const_spec(bias.shape), _const_spec(mult.shape)],
            new, jax.ShapeDtypeStruct((nb, A_HEAD_DIM, A_HEADS), F32))


def _attn_decode(*side):
    args, in_specs, out_spec, out_shape = _attn_decode_operands(*side)
    return pl.pallas_call(
        _attn_decode_body,
        grid=(out_shape.shape[0],),
        in_specs=in_specs,
        out_specs=out_spec,
        out_shape=out_shape,
        compiler_params=_params(1),
        name="attn_decode",
    )(*args)


def _rec_tables(c, single, stacked_heads):
    nl = 0 if single else int(math.log2(c))
    assert single or 2 ** nl == c
    t = np.arange(c)[:, None]
    i = np.arange(c)[None, :]
    e_rows, masks = [], []
    for lvl in range(nl):
        half = c >> (lvl + 1)
        mid = (t // (2 * half)) * (2 * half) + half
        right = t >= mid
        e_rows.append(np.where(right, (i >= mid) & (i <= t), (i > t) & (i < mid)))
        same = (t // (2 * half)) == (i // (2 * half))
        masks.append(same & right & (i < mid))
    masks.append(t == i)
    e_rows.append(i <= t)
    e_rows.append(i > t)
    e = np.concatenate(e_rows, 0).astype(np.float32)
    m = np.stack(masks, 0).astype(np.float32)
    if stacked_heads > 1:
        m = np.tile(m, (1, stacked_heads, 1))
    return jnp.asarray(np.concatenate([e, e], 1), dtype=BF16), jnp.asarray(m)


def _rec_core(q, k, lf, v, e_ref, m_ref, st_ref, o_ref, *, c, kind, single):
    nl = 0 if single else int(math.log2(c))
    n_win = q.shape[1] // LANES
    win = lambda x, w: x[:, w * LANES:(w + 1) * LANES]
    if single:
        e_cum = jnp.exp(lf)
        ke_f = k
    else:
        l1 = lf.astype(BF16)
        l2 = (lf - l1.astype(F32)).astype(BF16)
        decay = jnp.exp(_dot(e_ref[...], jnp.concatenate([l1, l2], axis=0)))
        seg = lambda n: decay[n * c:(n + 1) * c]
        e_cum = seg(nl)
        ke_f = k * seg(nl + 1)
    yield
    row = lax.broadcasted_iota(jnp.int32, (c, LANES), 0)

    def level_operand(n, w):
        half = c >> (n + 1)
        qw, kw = win(q, w), win(k, w)
        if half >= 8:
            pieces = [(qw if j % 2 else kw)[j * half:(j + 1) * half] for j in range(c // half)]
            side = jnp.concatenate(pieces, axis=0)
        else:
            side = jnp.where((row & half) != 0, qw, kw)
        return (side * win(seg(n), w)).astype(BF16)

    u = [[level_operand(n, w) for w in range(n_win)] for n in range(nl)]
    uq = [win(q, w).astype(BF16) for w in range(n_win)]
    uk = [win(k, w).astype(BF16) for w in range(n_win)]
    qg = (q * e_cum).astype(BF16)
    ke = ke_f.astype(BF16)
    e_tot = e_cum[0:1, :] if single else e_cum[c - 1:c, :]
    vb = v.astype(BF16)
    yield
    lane = lax.broadcasted_iota(jnp.int32, (c, LANES), 1)
    low = lane < B_VAL_DIM
    if kind == "c":
        head = lax.broadcasted_iota(jnp.int32, (1, LANES), 1) // C_KEY_DIM
        hm = [(head == h).astype(F32).astype(BF16) for h in range(C_HEADS)]
        stack = lambda x: jnp.concatenate([x * hm[h] for h in range(C_HEADS)], axis=0)
        att_all = _dot_nt(stack(uq[0]), uk[0]) * m_ref[nl]
        for n in range(nl):
            att_all = att_all + _dot_nt(stack(u[n][0]), u[n][0]) * m_ref[n]
        att_all = att_all.astype(BF16)
        atts = [att_all[h * c:(h + 1) * c] for h in range(C_HEADS)]
    else:
        atts = []
        for h in range(B_HEADS):
            att = _dot_nt(uq[h], uk[h]) * m_ref[nl]
            for n in range(nl):
                att = att + _dot_nt(u[n][h], u[n][h]) * m_ref[n]
            atts.append(att.astype(BF16))
    intra = [jnp.where(low, _dot(atts[2 * pair], win(vb, pair)), _dot(atts[2 * pair + 1], win(vb, pair)))
             for pair in range(2)]
    half_rows = [slice(hh * B_VAL_DIM, (hh + 1) * B_VAL_DIM) for hh in range(2)]
    if kind == "c":
        srow = lax.broadcasted_iota(jnp.int32, (LANES, LANES), 0) // B_VAL_DIM
        slane = lax.broadcasted_iota(jnp.int32, (LANES, LANES), 1) // C_KEY_DIM
        added = [_dot_tn(win(vb, pair), ke) * (slane == srow + 2 * pair).astype(F32) for pair in range(2)]
    else:
        added = [jnp.concatenate([_dot_tn(win(vb, pair), win(ke, 2 * pair + hh))[half_rows[hh], :]
                                  for hh in range(2)], axis=0) for pair in range(2)]
    yield
    for pair in range(2):
        st = st_ref[pair]
        stb = st.astype(BF16)
        if kind == "c":
            o_ref[:, pair * LANES:(pair + 1) * LANES] = intra[pair] + _dot_nt(qg, stb)
            st_ref[pair] = st * e_tot + added[pair]
            continue
        carried = [_dot_nt(win(qg, 2 * pair + hh), stb) for hh in range(2)]
        o_ref[:, pair * LANES:(pair + 1) * LANES] = intra[pair] + jnp.where(low, carried[0], carried[1])
        keep = jnp.concatenate([st[half_rows[hh], :] * win(e_tot, 2 * pair + hh) for hh in range(2)], axis=0)
        st_ref[pair] = keep + added[pair]


_REC_STAGES = 4
_REC_KINDS = ("b", "c")
_REC_N_PARAMS = {"b": 1, "c": 2}


def _rec_body(*refs, c, n_valid, n_b, n_sub):
    ins, pos = {}, 0
    for kind in _REC_KINDS:
        n = 4 + _REC_N_PARAMS[kind]
        ins[kind] = refs[pos:pos + n]
        pos += n
    outs = {kind: refs[pos + 2 * i:pos + 2 * i + 2] for i, kind in enumerate(_REC_KINDS)}
    pos += 2 * len(_REC_KINDS)
    states = {kind: refs[pos + i] for i, kind in enumerate(_REC_KINDS)}

    @pl.when(pl.program_id(1) == 0)
    def _():
        for kind in _REC_KINDS:
            states[kind][...] = ins[kind][-1][...]

    chunks = []
    for bb in range(n_b):
        for j in range(n_sub):
            for kind in _REC_KINDS:
                z_ref, *params, e_ref, m_ref, _ = ins[kind]
                chunks.append(_rec_chunk(z_ref.at[bb, pl.ds(j * c, c)], params, e_ref, m_ref, states[kind].at[bb],
                                         outs[kind][0].at[bb, pl.ds(j * c, c)], kind=kind, c=c, n_valid=n_valid))
    for _ in range(_REC_STAGES - 1):
        for chunk in chunks:
            next(chunk)
    for chunk in chunks:
        for _ in chunk:
            raise AssertionError("more stages than _REC_STAGES")

    @pl.when(pl.program_id(1) == pl.num_programs(1) - 1)
    def _():
        for kind in _REC_KINDS:
            outs[kind][1][...] = states[kind][...]


def _rec_chunk(z_ref, params, e_ref, m_ref, st_ref, o_ref, *, kind, c, n_valid):
    z = z_ref[...]
    if kind == "b":
        lb = params[0][...]
        f = lb + (1.0 - lb) * _sigmoid(z[:, B_KW:2 * B_KW])
        lf = jnp.log(jnp.maximum(f, MIN_F))
        k = 1.0 - f
        zq = z[:, 0:B_KW]
        q = zq * _sigmoid(zq)
        v = z[:, 2 * B_KW:2 * B_KW + B_WIDTH]
    else:
        cwg_ref, cb_ref = params
        q = z[:, 0:C_KW]
        k = z[:, C_KW:2 * C_KW]
        v = z[:, 2 * C_KW:2 * C_KW + C_WIDTH]
        x = _dot(z[:, ZC_WIDTH - LANES:].astype(BF16), cwg_ref[...]) + cb_ref[...]
        lf = (jnp.minimum(x, 0.0) - jnp.log1p(jnp.exp(-jnp.abs(x)))) / C_GATE_TEMP
    if n_valid < c:
        live = lax.broadcasted_iota(jnp.int32, (c, 1), 0) < n_valid
        q = jnp.where(live, q, 0.0)
        k = jnp.where(live, k, 0.0)
        lf = jnp.where(live, lf, 0.0)
    yield from _rec_core(q, k, lf, v, e_ref, m_ref, st_ref, o_ref, c=c, kind=kind, single=n_valid == 1)


def _recurrences(zs, layer, params, s0s, s0_layer, n_valid=None):
    nb, t, _ = zs["b"].shape
    c = min(REC_CHUNK, t)
    assert t % c == 0
    n_valid = c if n_valid is None else n_valid
    assert n_valid in (1, c)
    n_sub = math.gcd(t // c, REC_CHUNKS_PER_STEP)
    n_b = math.gcd(nb, REC_SEQS_PER_STEP)
    st_spec = pl.BlockSpec((n_b, 2, LANES, LANES), lambda b, j: (b, 0, 0, 0))
    s0_spec = pl.BlockSpec((None, n_b, 2, LANES, LANES), lambda b, j: (s0_layer, b, 0, 0, 0))
    args, in_specs = [], []
    for kind in _REC_KINDS:
        assert len(params[kind]) == _REC_N_PARAMS[kind] and zs[kind].shape[:2] == (nb, t)
        e, m = _rec_tables(c, n_valid == 1, C_HEADS if kind == "c" else 1)
        args += [zs[kind], *params[kind], e, m, s0s[kind]]
        in_specs += [pl.BlockSpec((n_b, n_sub * c, zs[kind].shape[2]), lambda b, j: (b, j, 0))]
        in_specs += [_layer_spec(x, layer) for x in params[kind]]
        in_specs += [_const_spec(e.shape), _const_spec(m.shape), s0_spec]
    n_kinds = len(_REC_KINDS)
    res = pl.pallas_call(
        functools.partial(_rec_body, c=c, n_valid=n_valid, n_b=n_b, n_sub=n_sub),
        grid=(nb // n_b, t // (c * n_sub)),
        in_specs=in_specs,
        out_specs=[pl.BlockSpec((n_b, n_sub * c, 2 * LANES), lambda b, j: (b, j, 0)), st_spec] * n_kinds,
        out_shape=[jax.ShapeDtypeStruct((nb, t, 2 * LANES), F32),
                   jax.ShapeDtypeStruct((nb, 2, LANES, LANES), F32)] * n_kinds,
        scratch_shapes=[pltpu.VMEM((n_b, 2, LANES, LANES), F32)] * n_kinds,
        compiler_params=_params(2),
        name="recurrences",
    )(*args)
    return {kind: (res[2 * i], res[2 * i + 1]) for i, kind in enumerate(_REC_KINDS)}


def _mix_body(*refs, tm, dils):
    h_ref = refs[0]
    n_att = 2 * len(dils) if dils else 1
    att_refs = refs[1:1 + n_att]
    ob_ref, bg_ref, oc_ref, cg_ref, bn_ref, cn_ref, ex_ref, bd_ref, w_ref, out_ref, mixed = refs[1 + n_att:12 + n_att]
    if dils:
        o_refs, l_refs = att_refs[:len(dils)], att_refs[len(dils):]
        o_nat, l_nat = refs[12 + n_att], refs[13 + n_att]
        n_cb = A_WIDTH // LANES
        lses = []
        for c, d in enumerate(dils):
            if d == 1:
                lses.append(l_refs[c][0])
                continue
            for r in range(d):
                l_nat[c, pl.ds(r, tm // d, stride=d), :] = l_refs[c][r]
                for cb in range(n_cb):
                    o_nat[c, cb, pl.ds(r, tm // d, stride=d), :] = (
                        o_refs[c][r, :, cb * LANES:(cb + 1) * LANES].astype(F32))
            lses.append(l_nat[c])
        mx = functools.reduce(jnp.maximum, lses)
        es = [jnp.exp(l - mx) for l in lses]
        den = functools.reduce(lambda a, b: a + b, es)
        ex = ex_ref[...]
        wts = [_dot_split2(e / den, ex) for e in es]
        for cb in range(n_cb):
            cs = slice(cb * LANES, (cb + 1) * LANES)
            terms = [wts[c][:, cs] * (o_refs[c][0, :, cs] if d == 1 else o_nat[c, cb]) for c, d in enumerate(dils)]
            mixed[:, cs] = functools.reduce(lambda a, b: a + b, terms).astype(BF16)
    else:
        mixed[:, 0:A_WIDTH] = att_refs[0][...].astype(BF16)
    bd = bd_ref[...]
    ob = ob_ref[...]
    obn = ob * lax.rsqrt(_dot_split2(ob * ob, bd) + RMS_EPS) * bn_ref[...]
    mixed[:, A_WIDTH:A_WIDTH + B_WIDTH] = (obn * _sigmoid(bg_ref[...])).astype(BF16)
    oc = oc_ref[...]
    ocn = oc * lax.rsqrt(_dot_split2(oc * oc, bd) + RMS_EPS) * cn_ref[...]
    cg = cg_ref[...]
    mixed[:, A_WIDTH + B_WIDTH:] = (ocn * (cg * _sigmoid(cg))).astype(BF16)
    out_ref[...] = h_ref[...] + _dot(mixed[...], w_ref[...])


def _mix(h, layer, att, dils, ob, zb, oc, zc, b_norm, c_norm, w_out):
    m, d = h.shape
    tm = min(MIX_ROW_TILE, m)
    assert m % tm == 0
    row = lambda w, blk=0: pl.BlockSpec((tm, w), lambda i: (i, blk))
    if dils:
        att_specs = [pl.BlockSpec((dd, tm // dd, wd), lambda i: (0, i, 0))
                     for wd in (A_WIDTH, LANES) for dd in dils]
        att_scratch = [pltpu.VMEM((len(dils), A_WIDTH // LANES, tm, LANES), F32),
                       pltpu.VMEM((len(dils), tm, LANES), F32)]
    else:
        att_specs, att_scratch = [row(A_WIDTH)], []
    expand = np.zeros((LANES, A_WIDTH), np.float32)
    for hd in range(A_HEADS):
        expand[hd, hd * A_HEAD_DIM:(hd + 1) * A_HEAD_DIM] = 1.0
    expand = np.concatenate([expand, expand], axis=0)
    grp = np.arange(B_WIDTH) // B_VAL_DIM
    blockmean = (grp[:, None] == grp[None, :]).astype(np.float32) / B_VAL_DIM
    blockmean = np.concatenate([blockmean, blockmean], axis=0)
    d_mix = A_WIDTH + B_WIDTH + C_WIDTH
    return pl.pallas_call(
        functools.partial(_mix_body, tm=tm, dils=tuple(dils)),
        grid=(m // tm,),
        in_specs=[row(d)] + att_specs
                 + [row(B_WIDTH), row(B_WIDTH, (2 * B_KW + B_WIDTH) // B_WIDTH),
                    row(C_WIDTH), row(C_WIDTH, (2 * C_KW + C_WIDTH) // C_WIDTH)]
                 + [_layer_spec(b_norm, layer), _layer_spec(c_norm, layer), _const_spec(expand.shape),
                    _const_spec(blockmean.shape), _layer_spec(w_out, layer)],
        out_specs=row(d),
        out_shape=jax.ShapeDtypeStruct((m, d), F32),
        scratch_shapes=[pltpu.VMEM((tm, d_mix), BF16)] + att_scratch,
        compiler_params=_params(1),
        name="mix_out",
    )(h, *att, ob, zb, oc, zc, b_norm, c_norm,
      jnp.asarray(expand, dtype=BF16), jnp.asarray(blockmean, dtype=BF16), w_out)


def _state_to_pairs(s):
    nl, nb, nh, dk, dv = s.shape
    st = jnp.swapaxes(s, 3, 4)
    if dk < LANES:
        st = jnp.stack([jnp.pad(st[:, :, h], ((0, 0), (0, 0), (0, 0), (h * dk, LANES - (h + 1) * dk)))
                        for h in range(nh)], 2)
    return st.reshape(nl, nb, 2, 2 * dv, LANES)


def _pairs_to_state(st, dk):
    nb = st.shape[0]
    st = st.reshape(nb, 4, B_VAL_DIM, LANES)
    if dk < LANES:
        st = jnp.stack([st[:, h, :, h * dk:(h + 1) * dk] for h in range(4)], 1)
    return jnp.swapaxes(st, 2, 3)


def _layer_tail(h, i, w, lb_all, p, final_norm, zb, zc, att, dils, states, s0_layer, decode):
    depth = w["w_in"].shape[0]
    if decode:
        nb = h.shape[0]
        pad = lambda a: jnp.pad(a.reshape(nb, 1, -1), ((0, 0), (0, DEC_PAD - 1), (0, 0)))
        zb3, zc3, n_valid = pad(zb), pad(zc), 1
    else:
        zb3, zc3, n_valid = zb[None], zc[None], None
    rec = _recurrences(dict(b=zb3, c=zc3), i, dict(b=[lb_all], c=[w["c_w_gate"], w["c_gate_bias"]]),
                       states, s0_layer, n_valid)
    (ob, sb), (oc, sc) = rec["b"], rec["c"]
    ob, oc = (ob[:, 0], oc[:, 0]) if decode else (ob[0], oc[0])
    h = _mix(h, i, att, dils, ob, zb, oc, zc, w["b_out_norm"], w["c_out_norm"], w["w_out"])
    ple = (p, w["ple_norm"], w["ple_w_gate"], w["ple_w_proj"])
    last = final_norm if i == depth - 1 else None
    h, *y = _ffn(h, i, w["ffn2_norm"], w["ffn2_w_gate"], w["ffn2_w_up"], w["ffn2_w_down"], ple, last)
    return h, (y[0] if y else None), _pairs_to_state(sb, B_KEY_DIM), _pairs_to_state(sc, C_KEY_DIM)


def kernel(x_prompt, x_sample, cache_k_a, cache_v_a, state_b, state_c, p_prompt, p_sample, ffn1_norm, ffn1_w_gate, ffn1_w_up, ffn1_w_down, mix_norm, w_in, lb_logits, b_out_norm, c_w_gate, c_gate_bias, c_out_norm, w_out, ffn2_norm, ffn2_w_gate, ffn2_w_up, ffn2_w_down, ple_norm, ple_w_gate, ple_w_proj, final_norm):
    depth = w_in.shape[0]
    rows = lambda a: a.reshape(depth, 1, -1)
    w = {name: a.astype(BF16) for name, a in dict(
        ffn1_w_gate=ffn1_w_gate, ffn1_w_up=ffn1_w_up, ffn1_w_down=ffn1_w_down, w_out=w_out, ffn2_w_gate=ffn2_w_gate,
        ffn2_w_up=ffn2_w_up, ffn2_w_down=ffn2_w_down, ple_w_gate=ple_w_gate, ple_w_proj=ple_w_proj).items()}
    w.update(
        ffn1_norm=rows(ffn1_norm), mix_norm=rows(mix_norm), w_in=jnp.swapaxes(w_in, 1, 2).astype(BF16),
        b_out_norm=rows(b_out_norm),
        c_w_gate=jnp.pad(c_w_gate, ((0, 0), (0, LANES - C_GATE_RANK), (0, 0))).astype(BF16),
        c_gate_bias=rows(c_gate_bias), c_out_norm=rows(c_out_norm), ffn2_norm=rows(ffn2_norm), ple_norm=rows(ple_norm))
    sm = jax.nn.softmax(lb_logits.astype(F32), axis=0)
    lb_all = jnp.maximum(jnp.cumsum(sm, axis=0) - sm[0], 0.0).reshape(depth, 1, B_KW)
    fn = final_norm.reshape(1, -1)

    (one, t, d), (nb, one_tok, _) = x_prompt.shape, x_sample.shape
    assert one == 1 and one_tok == 1
    hp, hs = x_prompt.reshape(t, d), x_sample.reshape(nb, d)
    pp, ps = p_prompt.reshape(depth, t, -1), p_sample.reshape(depth, nb, -1)
    keep = min(A_MAX_WINDOW, t)
    k_cache_t = jnp.transpose(cache_k_a, (0, 1, 3, 4, 2))
    v_cache_t = jnp.transpose(cache_v_a, (0, 1, 3, 4, 2))
    states_s = dict(b=_state_to_pairs(state_b), c=_state_to_pairs(state_c))
    states_p = dict(b=jnp.zeros((1, 1, 2, LANES, LANES), F32), c=jnp.zeros((1, 1, 2, LANES, LANES), F32))
    dils = tuple(dil for _, dil in A_CONFIGS)
    assert dils[0] == 1
    cols = lambda a: jnp.swapaxes(a.reshape(nb, A_HEADS, A_HEAD_DIM), 1, 2)
    ffn1 = lambda h, i, side=None: _ffn(h, i, w["ffn1_norm"], w["ffn1_w_gate"], w["ffn1_w_up"], w["ffn1_w_down"],
                                        side=side)
    outs = {key: [] for key in ("k_p", "v_p", "sb_p", "sc_p", "k_s", "v_s", "sb_s", "sc_s")}
    for i in range(depth):
        hs, = ffn1(hs, i)
        qkv_s, kf_s, vf_s, zb_s, zc_s = _proj(hs, i, w["mix_norm"], w["w_in"], nb, (), F32)
        side = (cols(qkv_s[0, :, 0:A_WIDTH]), cols(kf_s), cols(vf_s), k_cache_t, v_cache_t, i)
        if t // min(ROW_TILE, t) == nb:
            hp, o_t = ffn1(hp, i, side)
        else:
            hp, = ffn1(hp, i)
            o_t = _attn_decode(*side)
        *qkvs, kf_p, vf_p, zb_p, zc_p = _proj(hp, i, w["mix_norm"], w["w_in"], keep, dils[1:], BF16)
        res = [_attn_prompt(a, dil) for a, dil in zip(qkvs, dils)]
        att_p = [r[0] for r in res] + [r[1] for r in res]
        hp, y_p, sb, sc = _layer_tail(hp, i, w, lb_all, pp, fn, zb_p, zc_p, att_p, dils, states_p, 0, False)
        outs["k_p"].append(kf_p.reshape(1, keep, A_HEADS, A_HEAD_DIM))
        outs["v_p"].append(vf_p.reshape(1, keep, A_HEADS, A_HEAD_DIM))
        outs["sb_p"].append(sb)
        outs["sc_p"].append(sc)
        att_s = [jnp.swapaxes(o_t, 1, 2).reshape(nb, A_WIDTH)]
        hs, y_s, sb, sc = _layer_tail(hs, i, w, lb_all, ps, fn, zb_s, zc_s, att_s, (), states_s, i, True)
        outs["k_s"].append(kf_s.reshape(nb, 1, A_HEADS, A_HEAD_DIM))
        outs["v_s"].append(vf_s.reshape(nb, 1, A_HEADS, A_HEAD_DIM))
        outs["sb_s"].append(sb)
        outs["sc_s"].append(sc)
    st = {key: jnp.stack(val) for key, val in outs.items()}
    return (y_p.reshape(1, t, d), y_s.reshape(nb, 1, d), st["k_p"], st["v_p"], st["sb_p"], st["sc_p"],
            st["k_s"], st["v_s"], st["sb_s"], st["sc_s"])
```
